```python
import math
import jax
import jax.numpy as jnp
from jax import lax
import numpy as np

D_MODEL = 1024
BATCH = 8
SEQ = 2048
DEPTH = 4
DEC_BATCH = 128
DEC_SEQ = 4
PAST_LEN = 16384
PAGE_SIZE = 128

N_EVEN = (DEPTH + 1) // 2
N_ODD = DEPTH // 2
D_A = D_MODEL // 2
S5_GROUP = 16
G_A = D_A // S5_GROUP
S5_STATE = 64
D_B = D_MODEL // 2
H_B = 8
HD_B = D_B // H_B
CHUNK = 128
D_IN_EVEN = D_A + 2 * D_B
D_C = D_MODEL // 2
POOL_WINDOWS = (2, 4, 8, 16)
N_POOL = len(POOL_WINDOWS)
C_GROUP = D_C // N_POOL
POOL_BUF = max(POOL_WINDOWS) - 1
D_D = D_MODEL // 2
CONV_W = 3
D_IN_ODD = D_C + 3 * D_D
D_FF = ((8 * D_MODEL // 3 + 255) // 256) * 256
EPS = 1e-6

kernel_name = 'hybrid_s5_gmlp_pool_conv_decoder_step'


def rmsnorm(x, g):
    xf = x.astype(jnp.float32)
    y = xf * lax.rsqrt(jnp.mean(xf * xf, axis=-1, keepdims=True) + EPS)
    return (y * g.astype(jnp.float32)).astype(x.dtype)


def swiglu(x, w_gate, w_up, w_down):
    return (jax.nn.silu(x @ w_gate) * (x @ w_up)) @ w_down


def _complex_affine_combine(e1, e2):
    a1r, a1i, b1r, b1i = e1
    a2r, a2i, b2r, b2i = e2
    return (a2r * a1r - a2i * a1i,
            a2r * a1i + a2i * a1r,
            a2r * b1r - a2i * b1i + b2r,
            a2r * b1i + a2i * b1r + b2i)


def s5_mixer(u, h0_re, h0_im, lam_re, lam_im, log_dt, b_re, b_im, c_re, c_im, d_skip, glu_w, glu_b):
    f32 = jnp.float32
    bsz, l, _ = u.shape
    ug = u.astype(f32).reshape(bsz, l, G_A, S5_GROUP)
    lr = lam_re.astype(f32)
    li = lam_im.astype(f32)
    dt = jnp.exp(log_dt.astype(f32))[:, None]
    mag = jnp.exp(lr * dt)
    ang = li * dt
    ab_re = mag * jnp.cos(ang)
    ab_im = mag * jnp.sin(ang)
    den = lr * lr + li * li
    f_re = ((ab_re - 1.0) * lr + ab_im * li) / den
    f_im = (ab_im * lr - (ab_re - 1.0) * li) / den
    br = b_re.astype(f32)
    bi = b_im.astype(f32)
    bb_re = f_re[..., None] * br - f_im[..., None] * bi
    bb_im = f_re[..., None] * bi + f_im[..., None] * br
    x_re = jnp.einsum('blgp,gnp->blgn', ug, bb_re)
    x_im = jnp.einsum('blgp,gnp->blgn', ug, bb_im)
    a_re = jnp.broadcast_to(ab_re, x_re.shape)
    a_im = jnp.broadcast_to(ab_im, x_im.shape)
    p_re, p_im, s_re, s_im = lax.associative_scan(_complex_affine_combine, (a_re, a_im, x_re, x_im), axis=1)
    h0r = h0_re.astype(f32)[:, None]
    h0i = h0_im.astype(f32)[:, None]
    h_re = p_re * h0r - p_im * h0i + s_re
    h_im = p_re * h0i + p_im * h0r + s_im
    y = (jnp.einsum('blgn,gpn->blgp', h_re, c_re.astype(f32))
         - jnp.einsum('blgn,gpn->blgp', h_im, c_im.astype(f32))
         + d_skip.astype(f32) * ug)
    y = y.reshape(bsz, l, D_A)
    z = jax.nn.gelu(y)
    out = z * jax.nn.sigmoid(z @ glu_w.astype(f32) + glu_b.astype(f32))
    return out.astype(u.dtype), h_re[:, -1].astype(h0_re.dtype), h_im[:, -1].astype(h0_im.dtype)


def sgu_mixer(u, v, norm_g, w_s, b_s):
    vn = rmsnorm(v, norm_g)
    bsz, l, _ = u.shape
    cl = min(l, CHUNK)
    nch = l // cl
    mask = jnp.tril(jnp.ones((cl, cl), dtype=bool))
    w = jnp.where(mask, w_s[:, :cl, :cl], 0)
    vc = vn.reshape(bsz, nch, cl, H_B, HD_B)
    mixed = jnp.einsum('hij,bcjhd->bcihd', w, vc) + b_s[:, :cl].T[None, None, :, :, None]
    return u * mixed.reshape(bsz, l, D_B), vn


def pool_mixer(xc, buf, prefix_valid, pool_w, pool_scale):
    f32 = jnp.float32
    bsz, l, _ = xc.shape
    full = jnp.concatenate([buf.astype(xc.dtype), xc], axis=1)
    cs = jnp.pad(jnp.cumsum(full.astype(f32), axis=1), ((0, 0), (1, 0), (0, 0)))
    valid = jnp.concatenate([jnp.full((POOL_BUF,), prefix_valid, f32), jnp.ones((l,), f32)])
    cnt = jnp.pad(jnp.cumsum(valid), (1, 0))
    end = POOL_BUF + 1
    groups = []
    for gi, w in enumerate(POOL_WINDOWS):
        lo, hi = gi * C_GROUP, (gi + 1) * C_GROUP
        s = cs[:, end:end + l, lo:hi] - cs[:, end - w:end - w + l, lo:hi]
        n = cnt[end:end + l] - cnt[end - w:end - w + l]
        groups.append(s / n[None, :, None])
    pooled = jnp.concatenate(groups, axis=-1)
    diff = (pooled - xc.astype(f32)).reshape(bsz, l, N_POOL, C_GROUP)
    y = jnp.einsum('blgc,gcd->blgd', diff, pool_w.astype(f32)).reshape(bsz, l, D_C) * pool_scale.astype(f32)
    return y.astype(xc.dtype), full[:, -POOL_BUF:]


def conv_mixer(xd, b_gate, c_gate, buf, conv_w, conv_b):
    z = c_gate * xd
    l = z.shape[1]
    full = jnp.concatenate([buf.astype(z.dtype), z], axis=1)
    conv = conv_b
    for k in range(CONV_W):
        conv = conv + full[:, k:k + l] * conv_w[k]
    return b_gate * conv, full[:, -(CONV_W - 1):]


def run_trunk(x, s5_re, s5_im, pool_buf, conv_buf, prefix_valid, W):
    h = x
    new_re, new_im, new_v, new_pool, new_conv = [], [], [], [], []
    for layer in range(DEPTH):
        i = layer // 2
        hn = rmsnorm(h, W['norm_mix'][layer])
        if layer % 2 == 0:
            proj = hn @ W['w_in_even'][i]
            u_a = proj[..., :D_A]
            u_b = proj[..., D_A:D_A + D_B]
            v_b = proj[..., D_A + D_B:]
            y_a, hr, hi = s5_mixer(u_a, s5_re[i], s5_im[i], W['s5_lambda_re'][i], W['s5_lambda_im'][i],
                                   W['s5_log_dt'][i], W['s5_b_re'][i], W['s5_b_im'][i], W['s5_c_re'][i],
                                   W['s5_c_im'][i], W['s5_d'][i], W['s5_glu_w'][i], W['s5_glu_b'][i])
            y_b, vn = sgu_mixer(u_b, v_b, W['sgu_norm'][i], W['sgu_w'][i], W['sgu_b'][i])
            mix = jnp.concatenate([y_a, y_b], axis=-1) @ W['w_out_even'][i]
            new_re.append(hr)
            new_im.append(hi)
            new_v.append(vn)
        else:
            proj = hn @ W['w_in_odd'][i]
            x_c = proj[..., :D_C]
            x_d = proj[..., D_C:D_C + D_D]
            b_g = proj[..., D_C + D_D:D_C + 2 * D_D]
            c_g = proj[..., D_C + 2 * D_D:]
            y_c, nb_pool = pool_mixer(x_c, pool_buf[i], prefix_valid, W['pool_w'][i], W['pool_scale'][i])
            y_d, nb_conv = conv_mixer(x_d, b_g, c_g, conv_buf[i], W['conv_w'][i], W['conv_b'][i])
            mix = jnp.concatenate([y_c, y_d], axis=-1) @ W['w_out_odd'][i]
            new_pool.append(nb_pool)
            new_conv.append(nb_conv)
        h = h + mix
        h = h + swiglu(rmsnorm(h, W['norm_ffn'][layer]), W['ffn_w_gate'][layer], W['ffn_w_up'][layer],
                       W['ffn_w_down'][layer])
    y = rmsnorm(h, W['norm_final'])
    return y, jnp.stack(new_re), jnp.stack(new_im), jnp.stack(new_v), jnp.stack(new_pool), jnp.stack(new_conv)


def setup_inputs(seed: int = 0) -> dict:
    key = jax.random.key(seed)
    keys = jax.random.split(key, 40)
    f32 = jnp.float32

    def nrm(idx, shape, scale):
        return scale * jax.random.normal(keys[idx], shape, f32)

    n_idx = jnp.arange(S5_STATE, dtype=f32)
    return {
        'x_prompt': nrm(0, (BATCH, SEQ, D_MODEL), 1.0),
        'x_sample': nrm(1, (DEC_BATCH, DEC_SEQ, D_MODEL), 1.0),
        'state_s5_re': nrm(2, (N_EVEN, DEC_BATCH, G_A, S5_STATE), 0.5),
        'state_s5_im': nrm(3, (N_EVEN, DEC_BATCH, G_A, S5_STATE), 0.5),
        'state_pool': nrm(4, (N_ODD, DEC_BATCH, POOL_BUF, D_C), 1.0),
        'state_conv': nrm(5, (N_ODD, DEC_BATCH, CONV_W - 1, D_D), 0.5),
        'norm_mix': 1.0 + nrm(6, (DEPTH, D_MODEL), 0.02),
        'norm_ffn': 1.0 + nrm(7, (DEPTH, D_MODEL), 0.02),
        'norm_final': 1.0 + nrm(8, (D_MODEL,), 0.02),
        'w_in_even': nrm(9, (N_EVEN, D_MODEL, D_IN_EVEN), D_MODEL ** -0.5),
        'w_out_even': nrm(10, (N_EVEN, D_A + D_B, D_MODEL), (D_A + D_B) ** -0.5),
        's5_lambda_re': -0.5 + nrm(11, (N_EVEN, G_A, S5_STATE), 0.01),
        's5_lambda_im': jnp.broadcast_to(math.pi * n_idx, (N_EVEN, G_A, S5_STATE)) + nrm(12, (N_EVEN, G_A, S5_STATE), 0.01),
        's5_log_dt': jax.random.uniform(keys[13], (N_EVEN, G_A), f32, math.log(1e-3), math.log(1e-1)),
        's5_b_re': nrm(14, (N_EVEN, G_A, S5_STATE, S5_GROUP), S5_GROUP ** -0.5),
        's5_b_im': nrm(15, (N_EVEN, G_A, S5_STATE, S5_GROUP), S5_GROUP ** -0.5),
        's5_c_re': nrm(16, (N_EVEN, G_A, S5_GROUP, S5_STATE), (2 * S5_STATE) ** -0.5),
        's5_c_im': nrm(17, (N_EVEN, G_A, S5_GROUP, S5_STATE), (2 * S5_STATE) ** -0.5),
        's5_d': nrm(18, (N_EVEN, G_A, S5_GROUP), 1.0),
        's5_glu_w': nrm(19, (N_EVEN, D_A, D_A), D_A ** -0.5),
        's5_glu_b': nrm(20, (N_EVEN, D_A), 0.01),
        'sgu_norm': 1.0 + nrm(21, (N_EVEN, D_B), 0.02),
        'sgu_w': nrm(22, (N_EVEN, H_B, CHUNK, CHUNK), CHUNK ** -0.5),
        'sgu_b': 1.0 + nrm(23, (N_EVEN, H_B, CHUNK), 0.02),
        'w_in_odd': nrm(24, (N_ODD, D_MODEL, D_IN_ODD), D_MODEL ** -0.5),
        'w_out_odd': nrm(25, (N_ODD, D_C + D_D, D_MODEL), (D_C + D_D) ** -0.5),
        'pool_w': nrm(26, (N_ODD, N_POOL, C_GROUP, C_GROUP), C_GROUP ** -0.5),
        'pool_scale': 1.0 + nrm(27, (N_ODD, D_C), 0.02),
        'conv_w': nrm(28, (N_ODD, CONV_W, D_D), CONV_W ** -0.5),
        'conv_b': nrm(29, (N_ODD, D_D), 0.01),
        'ffn_w_gate': nrm(30, (DEPTH, D_MODEL, D_FF), D_MODEL ** -0.5),
        'ffn_w_up': nrm(31, (DEPTH, D_MODEL, D_FF), D_MODEL ** -0.5),
        'ffn_w_down': nrm(32, (DEPTH, D_FF, D_MODEL), D_FF ** -0.5),
    }


def reference(x_prompt, x_sample, state_s5_re, state_s5_im, state_pool, state_conv,
              norm_mix, norm_ffn, norm_final, w_in_even, w_out_even,
              s5_lambda_re, s5_lambda_im, s5_log_dt, s5_b_re, s5_b_im, s5_c_re, s5_c_im, s5_d,
              s5_glu_w, s5_glu_b, sgu_norm, sgu_w, sgu_b, w_in_odd, w_out_odd,
              pool_w, pool_scale, conv_w, conv_b, ffn_w_gate, ffn_w_up, ffn_w_down):
    W = dict(norm_mix=norm_mix, norm_ffn=norm_ffn, norm_final=norm_final,
             w_in_even=w_in_even, w_out_even=w_out_even,
             s5_lambda_re=s5_lambda_re, s5_lambda_im=s5_lambda_im, s5_log_dt=s5_log_dt,
             s5_b_re=s5_b_re, s5_b_im=s5_b_im, s5_c_re=s5_c_re, s5_c_im=s5_c_im, s5_d=s5_d,
             s5_glu_w=s5_glu_w, s5_glu_b=s5_glu_b, sgu_norm=sgu_norm, sgu_w=sgu_w, sgu_b=sgu_b,
             w_in_odd=w_in_odd, w_out_odd=w_out_odd, pool_w=pool_w, pool_scale=pool_scale,
             conv_w=conv_w, conv_b=conv_b, ffn_w_gate=ffn_w_gate, ffn_w_up=ffn_w_up, ffn_w_down=ffn_w_down)
    bp = x_prompt.shape[0]
    z_re = jnp.zeros((N_EVEN, bp, G_A, S5_STATE), state_s5_re.dtype)
    z_pool = jnp.zeros((N_ODD, bp, POOL_BUF, D_C), x_prompt.dtype)
    z_conv = jnp.zeros((N_ODD, bp, CONV_W - 1, D_D), x_prompt.dtype)
    y_prompt, p_s5_re, p_s5_im, _, p_pool, p_conv = run_trunk(x_prompt, z_re, z_re, z_pool, z_conv, 0.0, W)
    y_sample, s_s5_re, s_s5_im, s_sgu_v, s_pool, s_conv = run_trunk(
        x_sample, state_s5_re, state_s5_im, state_pool, state_conv, 1.0, W)
    return (y_prompt, y_sample, p_s5_re, p_s5_im, p_pool, p_conv, s_s5_re, s_s5_im, s_sgu_v, s_pool, s_conv)
```

```python
import functools
import math

import jax
import jax.numpy as jnp
from jax import lax
from jax.experimental import pallas as pl
from jax.experimental.pallas import tpu as pltpu

F32 = jnp.float32
BF16 = jnp.bfloat16

EPS = 1e-6
S5_GROUP = 16
S5_STATE = 64
N_HEADS_SGU = 8
SGU_CHUNK = 128
POOL_WINDOWS = (2, 4, 8, 16)
POOL_BUF = max(POOL_WINDOWS) - 1
CONV_W = 3

LANES = 128
SUBLANES = 8
ROW_TILE = 512
FF_CHUNK = 512
SCAN_COLS = 1024
VMEM_LIMIT = 56 * 1024 * 1024


def _dot(a, b):
    return jnp.dot(a, b, preferred_element_type=F32)


def _rms(x, g):
    return x * lax.rsqrt(jnp.mean(x * x, axis=-1, keepdims=True) + EPS) * g


def _const_spec(shape):
    zeros = (0,) * len(shape)
    return pl.BlockSpec(shape, lambda *_: zeros, pipeline_mode=pl.Buffered(1))


class _Rows:
    def __init__(self, t, nb, batch_view, tile):
        self.t, self.nb, self.batch_view = t, nb, batch_view
        if batch_view:
            self.tile = min(tile, t)
            self.grid = (t // self.tile, nb)
        else:
            self.tile = min(tile, t * nb)
            self.grid = (t * nb // self.tile,)

    def view(self, a):
        return a.reshape(self.t, -1) if self.batch_view else a

    def unview(self, a):
        return a.reshape(self.t * self.nb, -1) if self.batch_view else a

    def view_shape(self, c, dtype):
        if self.batch_view:
            return jax.ShapeDtypeStruct((self.t, self.nb * c), dtype)
        return jax.ShapeDtypeStruct((self.t * self.nb, c), dtype)

    def spec(self, c):
        if self.batch_view:
            return pl.BlockSpec((self.tile, c), lambda i, b: (i, b))
        return pl.BlockSpec((self.tile, c), lambda i: (i, 0))

    def spec_bld(self, c):
        return pl.BlockSpec((None, self.tile, c), lambda i, b: (b, i, 0))

    @property
    def semantics(self):
        return ("arbitrary",) * len(self.grid)


def _trunk_kernel(*refs, has_mix, nxt, d_half):
    refs = list(refs)
    h_ref = refs.pop(0)
    if has_mix:
        ya_ref, yb_ref, wo_ref, gf_ref, wg_ref, wu_ref, wd_ref = refs[:7]
        del refs[:7]
    gn_ref = refs.pop(0)
    if nxt != "final":
        win_ref = refs.pop(0)
    if nxt == "even":
        gs_ref = refs.pop(0)
    outs = refs

    h = h_ref[...].astype(F32)
    if has_mix:
        h = h + _dot(ya_ref[...], wo_ref[:d_half, :]) + _dot(yb_ref[...], wo_ref[d_half:, :])
        hn = _rms(h, gf_ref[...]).astype(BF16)
        d_ff = wg_ref.shape[1]
        acc = None
        for c0 in range(0, d_ff, FF_CHUNK):
            c1 = min(c0 + FF_CHUNK, d_ff)
            g = _dot(hn, wg_ref[:, c0:c1])
            u = _dot(hn, wu_ref[:, c0:c1])
            a = (g * jax.nn.sigmoid(g) * u).astype(BF16)
            part = _dot(a, wd_ref[c0:c1, :])
            acc = part if acc is None else acc + part
        h = h + acc
        if nxt != "final":
            outs.pop(0)[...] = h
    hn = _rms(h, gn_ref[...])
    if nxt == "final":
        outs[0][...] = hn
        return
    proj = _dot(hn.astype(BF16), win_ref[...])
    if nxt == "even":
        ua_ref, ub_ref, vn_ref = outs
        ua_ref[...] = proj[:, :d_half]
        ub_ref[...] = proj[:, d_half:2 * d_half]
        vn_ref[...] = _rms(proj[:, 2 * d_half:], gs_ref[...])
    else:
        xc_ref, z_ref, bg_ref = outs
        xc_ref[...] = proj[:, :d_half]
        z_ref[...] = proj[:, 3 * d_half:] * proj[:, d_half:2 * d_half]
        bg_ref[...] = proj[:, 2 * d_half:3 * d_half]


def _trunk_call(rows, h, *, mix, ffn, g_next, w_in, g_sgu, nxt, h_is_bld=False, out_is_bld=False):
    d = h.shape[-1]
    d_half = d // 2
    args, specs = [], []
    if h_is_bld:
        args.append(h)
        specs.append(rows.spec_bld(d))
    else:
        args.append(rows.view(h))
        specs.append(rows.spec(d))
    if mix is not None:
        ya, yb, w_out, g_ffn = mix
        args += [rows.view(ya), rows.view(yb), w_out, g_ffn, *ffn]
        specs += [rows.spec(d_half), rows.spec(d_half), _const_spec(w_out.shape), _const_spec(g_ffn.shape)]
        specs += [_const_spec(w.shape) for w in ffn]
    args.append(g_next)
    specs.append(_const_spec(g_next.shape))
    if nxt != "final":
        args.append(w_in)
        specs.append(_const_spec(w_in.shape))
    if nxt == "even":
        args.append(g_sgu)
        specs.append(_const_spec(g_sgu.shape))

    out_shapes, out_specs = [], []
    if nxt == "final":
        if out_is_bld:
            out_shapes.append(jax.ShapeDtypeStruct((rows.nb, rows.t, d), F32))
            out_specs.append(rows.spec_bld(d))
        else:
            out_shapes.append(rows.view_shape(d, F32))
            out_specs.append(rows.spec(d))
    else:
        if mix is not None:
            out_shapes.append(rows.view_shape(d, F32))
            out_specs.append(rows.spec(d))
        out_shapes += [rows.view_shape(d_half, F32)] * 3
        out_specs += [rows.spec(d_half)] * 3

    outs = pl.pallas_call(
        functools.partial(_trunk_kernel, has_mix=mix is not None, nxt=nxt, d_half=d_half),
        grid=rows.grid,
        in_specs=specs,
        out_specs=out_specs,
        out_shape=out_shapes,
        compiler_params=pltpu.CompilerParams(dimension_semantics=rows.semantics,
                                             vmem_limit_bytes=VMEM_LIMIT),
        name=f"trunk_{'mix' if mix is not None else 'in'}_{nxt}",
    )(*args)
    if nxt == "final" and out_is_bld:
        return list(outs)
    return [rows.unview(o) for o in outs]


def _s5_prep_kernel(lr_ref, li_ref, ldt_ref, bre_ref, bim_ref, are_ref, aim_ref, bbre_ref, bbim_ref):
    lr = lr_ref[...]
    li = li_ref[...]
    dt = jnp.exp(ldt_ref[...])
    mag = jnp.exp(lr * dt)
    ang = li * dt
    ab_re = mag * jnp.cos(ang)
    ab_im = mag * jnp.sin(ang)
    den = lr * lr + li * li
    f_re = ((ab_re - 1.0) * lr + ab_im * li) / den
    f_im = (ab_im * lr - (ab_re - 1.0) * li) / den
    are_ref[...] = ab_re
    aim_ref[...] = ab_im
    for g in range(lr.shape[0]):
        fr = f_re[g:g + 1, :]
        fi = f_im[g:g + 1, :]
        br = bre_ref[g]
        bi = bim_ref[g]
        bbre_ref[g] = fr * br - fi * bi
        bbim_ref[g] = fr * bi + fi * br


def _s5_prep(lam_re, lam_im, log_dt, b_re, b_im):
    g, n = lam_re.shape
    p = b_re.shape[-1]
    bt_re = jnp.swapaxes(b_re, 1, 2)
    bt_im = jnp.swapaxes(b_im, 1, 2)
    a_re, a_im, bb_re, bb_im = pl.pallas_call(
        _s5_prep_kernel,
        out_shape=[jax.ShapeDtypeStruct((g, n), F32)] * 2 + [jax.ShapeDtypeStruct((g, p, n), F32)] * 2,
        name="s5_discretize",
    )(lam_re, lam_im, log_dt.reshape(g, 1), bt_re, bt_im)
    gs = LANES // p
    eye = jnp.eye(gs, dtype=F32)

    def blockdiag_in(bb):
        return jnp.einsum("kgpn,gh->kgphn", bb.reshape(g // gs, gs, p, n), eye).reshape(
            g // gs, gs * p, gs * n).astype(BF16)

    return a_re.reshape(1, g * n), a_im.reshape(1, g * n), blockdiag_in(bb_re), blockdiag_in(bb_im)


def _blockdiag_out(c):
    g, p, n = c.shape
    gs = LANES // p
    eye = jnp.eye(gs, dtype=F32)
    return jnp.einsum("kgpn,gh->kgnhp", c.reshape(g // gs, gs, p, n), eye).reshape(
        g // gs, gs * n, gs * p).astype(BF16)


def _s5_kernel(u_ref, h0re_ref, h0im_ref, are_ref, aim_ref, bbre_ref, bbim_ref, cre_ref, cim_ref,
               d_ref, gw_ref, gb_ref, y_ref, hre_out, him_out, xre, xim, hre_s, him_s, *, nb, tb):
    i = pl.program_id(0)

    @pl.when(i == 0)
    def _():
        hre_s[...] = h0re_ref[...]
        him_s[...] = h0im_ref[...]

    u = u_ref[...]
    ub = u.astype(BF16)
    n_slab, slab_in, slab_st = bbre_ref.shape
    for k in range(n_slab):
        uk = ub[:, k * slab_in:(k + 1) * slab_in]
        xre[:, k * slab_st:(k + 1) * slab_st] = _dot(uk, bbre_ref[k])
        xim[:, k * slab_st:(k + 1) * slab_st] = _dot(uk, bbim_ref[k])

    n_state = xre.shape[1]
    for c0 in range(0, n_state, SCAN_COLS):
        cols = slice(c0, c0 + SCAN_COLS)
        ar = jnp.broadcast_to(are_ref[:, cols], (SUBLANES, SCAN_COLS))
        ai = jnp.broadcast_to(aim_ref[:, cols], (SUBLANES, SCAN_COLS))
        for r0 in range(0, nb, SUBLANES):
            def step(t, carry, r0=r0, cols=cols, ar=ar, ai=ai):
                hr, hi = carry
                r = pl.multiple_of(t * nb + r0, SUBLANES)
                nr = ar * hr - ai * hi + xre[pl.ds(r, SUBLANES), cols]
                ni = ar * hi + ai * hr + xim[pl.ds(r, SUBLANES), cols]
                xre[pl.ds(r, SUBLANES), cols] = nr
                xim[pl.ds(r, SUBLANES), cols] = ni
                return nr, ni

            hr, hi = lax.fori_loop(0, tb, step, (hre_s[r0:r0 + SUBLANES, cols], him_s[r0:r0 + SUBLANES, cols]))
            hre_s[r0:r0 + SUBLANES, cols] = hr
            him_s[r0:r0 + SUBLANES, cols] = hi

    slab_out = cre_ref.shape[2]
    parts = []
    for k in range(n_slab):
        hr = xre[:, k * slab_st:(k + 1) * slab_st].astype(BF16)
        hi = xim[:, k * slab_st:(k + 1) * slab_st].astype(BF16)
        parts.append(_dot(hr, cre_ref[k]) - _dot(hi, cim_ref[k]))
    y = jnp.concatenate(parts, axis=-1) + d_ref[...] * u
    z = jax.nn.gelu(y)
    out = z * jax.nn.sigmoid(_dot(z.astype(BF16), gw_ref[...]) + gb_ref[...])
    y_ref[...] = out.astype(y_ref.dtype)
    hre_out[...] = hre_s[...]
    him_out[...] = him_s[...]


def _s5_call(u, h0_re, h0_im, prm, *, nb, t):
    a_re, a_im, bb_re, bb_im, c_re, c_im, d_skip, glu_w, glu_b = prm
    d_a = u.shape[1]
    n_state = a_re.shape[1]
    tb = max(1, min(t, ROW_TILE // nb))
    rows = tb * nb
    consts = [a_re, a_im, bb_re, bb_im, c_re, c_im, d_skip, glu_w, glu_b]
    state_spec = pl.BlockSpec((nb, n_state), lambda i: (0, 0))
    y, hre, him = pl.pallas_call(
        functools.partial(_s5_kernel, nb=nb, tb=tb),
        grid=(t // tb,),
        in_specs=[pl.BlockSpec((rows, d_a), lambda i: (i, 0)), state_spec, state_spec]
        + [_const_spec(c.shape) for c in consts],
        out_specs=[pl.BlockSpec((rows, d_a), lambda i: (i, 0)), state_spec, state_spec],
        out_shape=[jax.ShapeDtypeStruct((t * nb, d_a), BF16),
                   jax.ShapeDtypeStruct((nb, n_state), F32),
                   jax.ShapeDtypeStruct((nb, n_state), F32)],
        scratch_shapes=[pltpu.VMEM((rows, n_state), F32), pltpu.VMEM((rows, n_state), F32),
                        pltpu.VMEM((nb, n_state), F32), pltpu.VMEM((nb, n_state), F32)],
        compiler_params=pltpu.CompilerParams(dimension_semantics=("arbitrary",),
                                             vmem_limit_bytes=VMEM_LIMIT),
        name="s5_mixer",
    )(u, h0_re, h0_im, *consts)
    return y, hre, him


def _sgu_chunks_kernel(u_ref, v_ref, w_ref, bias_ref, y_ref):
    n_heads, cl, _ = w_ref.shape
    hd = v_ref.shape[1] // n_heads
    row = lax.broadcasted_iota(jnp.int32, (cl, cl), 0)
    col = lax.broadcasted_iota(jnp.int32, (cl, cl), 1)
    w = [jnp.where(col <= row, w_ref[h], 0.0).astype(BF16) for h in range(n_heads)]
    heads_per_slab = LANES // hd
    lane = lax.broadcasted_iota(jnp.int32, (cl, LANES), 1)
    for c in range(v_ref.shape[0] // cl):
        r = slice(c * cl, (c + 1) * cl)
        slabs = []
        for s in range(n_heads // heads_per_slab):
            v = v_ref[r, s * LANES:(s + 1) * LANES].astype(BF16)
            mixed = _dot(w[s * heads_per_slab], v)
            for j in range(1, heads_per_slab):
                mixed = jnp.where(lane >= j * hd, _dot(w[s * heads_per_slab + j], v), mixed)
            slabs.append(mixed)
        mixed = jnp.concatenate(slabs, axis=-1) + bias_ref[...]
        y_ref[r, :] = (u_ref[r, :] * mixed).astype(y_ref.dtype)


def _sgu_short_kernel(u_ref, v_ref, wl_ref, bias_ref, y_ref, *, nb, t):
    for i in range(t):
        mixed = bias_ref[i:i + 1, :]
        for j in range(i + 1):
            mixed = mixed + wl_ref[i, j:j + 1, :] * v_ref[j * nb:(j + 1) * nb, :]
        y_ref[i * nb:(i + 1) * nb, :] = (u_ref[i * nb:(i + 1) * nb, :] * mixed).astype(y_ref.dtype)


def _sgu_call(u, vn, w_s, b_s, *, nb, t):
    d_b = u.shape[1]
    n_heads = w_s.shape[0]
    hd = d_b // n_heads
    cl = min(t, SGU_CHUNK)
    bias = jnp.repeat(b_s[:, :cl].T, hd, axis=1)
    if cl == SGU_CHUNK:
        rows = _Rows(t, nb, True, ROW_TILE)
        y = pl.pallas_call(
            _sgu_chunks_kernel,
            grid=rows.grid,
            in_specs=[rows.spec(d_b), rows.spec(d_b), _const_spec(w_s.shape), _const_spec(bias.shape)],
            out_specs=rows.spec(d_b),
            out_shape=rows.view_shape(d_b, BF16),
            compiler_params=pltpu.CompilerParams(dimension_semantics=rows.semantics,
                                                 vmem_limit_bytes=VMEM_LIMIT),
            name="sgu_chunks",
        )(rows.view(u), rows.view(vn), w_s, bias)
        return rows.unview(y)
    wl = jnp.repeat(jnp.transpose(w_s[:, :cl, :cl], (1, 2, 0)), hd, axis=2)
    return pl.pallas_call(
        functools.partial(_sgu_short_kernel, nb=nb, t=t),
        out_shape=jax.ShapeDtypeStruct((t * nb, d_b), BF16),
        name="sgu_short",
    )(u, vn, wl, bias)


def _poolconv_kernel(xc_ref, z_ref, bg_ref, pbuf_ref, cbuf_ref, pw_ref, ps_ref, cw_ref, cb_ref,
                     yc_ref, yd_ref, pout_ref, cout_ref, fullc, fullz, *, nb, tb, prefix_valid):
    i = pl.program_id(0)
    hp = POOL_BUF * nb
    hc = (CONV_W - 1) * nb
    r = tb * nb

    @pl.when(i == 0)
    def _():
        fullc[0:hp, :] = pbuf_ref[...]
        fullz[0:hc, :] = cbuf_ref[...]

    xc = xc_ref[...]
    fullc[hp:hp + r, :] = xc
    fullz[hc:hc + r, :] = z_ref[...]

    conv = cb_ref[...]
    for k in range(CONV_W):
        conv = conv + fullz[k * nb:k * nb + r, :] * cw_ref[k:k + 1, :]
    yd_ref[...] = (bg_ref[...] * conv).astype(yd_ref.dtype)

    t_glob = i * tb + lax.broadcasted_iota(jnp.int32, (r, 1), 0) // nb
    cg = xc.shape[1] // len(POOL_WINDOWS)
    for gi, w in enumerate(POOL_WINDOWS):
        lanes = slice(gi * cg, (gi + 1) * cg)
        s = fullc[:, lanes]
        k = 1
        while k < w:
            s = s[k * nb:, :] + s[:-k * nb, :]
            k *= 2
        first = (POOL_BUF - (w - 1)) * nb
        s = s[first:first + r, :]
        if prefix_valid:
            n = jnp.full((r, 1), float(w), F32)
        else:
            n = jnp.minimum(t_glob + 1, w).astype(F32)
        diff = s / n - xc[:, lanes]
        yc = _dot(diff.astype(BF16), pw_ref[gi]) * ps_ref[:, lanes]
        yc_ref[:, lanes] = yc.astype(yc_ref.dtype)

    new_pool = fullc[r:r + hp, :]
    new_conv = fullz[r:r + hc, :]
    fullc[0:hp, :] = new_pool
    fullz[0:hc, :] = new_conv
    pout_ref[...] = new_pool
    cout_ref[...] = new_conv


def _poolconv_call(xc, z, bg, pool_buf, conv_buf, prm, *, nb, t, prefix_valid):
    pool_w, pool_scale, conv_w, conv_b = prm
    c = xc.shape[1]
    tb = max(1, min(t, ROW_TILE // nb))
    r = tb * nb
    hp, hc = POOL_BUF * nb, (CONV_W - 1) * nb
    row_spec = pl.BlockSpec((r, c), lambda i: (i, 0))
    consts = [pool_w, pool_scale, conv_w, conv_b]
    return pl.pallas_call(
        functools.partial(_poolconv_kernel, nb=nb, tb=tb, prefix_valid=prefix_valid),
        grid=(t // tb,),
        in_specs=[row_spec, row_spec, row_spec,
                  pl.BlockSpec((hp, c), lambda i: (0, 0)), pl.BlockSpec((hc, c), lambda i: (0, 0))]
        + [_const_spec(a.shape) for a in consts],
        out_specs=[row_spec, row_spec,
                   pl.BlockSpec((hp, c), lambda i: (0, 0)), pl.BlockSpec((hc, c), lambda i: (0, 0))],
        out_shape=[jax.ShapeDtypeStruct((t * nb, c), BF16), jax.ShapeDtypeStruct((t * nb, c), BF16),
                   jax.ShapeDtypeStruct((hp, c), F32), jax.ShapeDtypeStruct((hc, c), F32)],
        scratch_shapes=[pltpu.VMEM((hp + r, c), F32), pltpu.VMEM((hc + r, c), F32)],
        compiler_params=pltpu.CompilerParams(dimension_semantics=("arbitrary",),
                                             vmem_limit_bytes=VMEM_LIMIT),
        name="pool_conv",
    )(xc, z, bg, pool_buf, conv_buf, *consts)


def _to_time_major(a):
    return jnp.swapaxes(a, 0, 1).reshape(a.shape[0] * a.shape[1], a.shape[2])


def _to_batch_major(a, nb):
    return jnp.swapaxes(a.reshape(-1, nb, a.shape[1]), 0, 1)


def _run_trunk(x, s5_re, s5_im, pool_buf, conv_buf, prefix_valid, W, *, batch_view):
    nb, t, d = x.shape
    depth = W["norm_mix"].shape[0]
    rows = _Rows(t, nb, batch_view, ROW_TILE)
    g_mix = W["norm_mix"].reshape(depth, 1, d)
    g_ffn = W["norm_ffn"].reshape(depth, 1, d)
    g_fin = W["norm_final"].reshape(1, d)

    def next_in(layer):
        if layer == depth:
            return dict(g_next=g_fin, w_in=None, g_sgu=None, nxt="final")
        i = layer // 2
        if layer % 2 == 0:
            return dict(g_next=g_mix[layer], w_in=W["w_in_even"][i], g_sgu=W["sgu_norm"][i].reshape(1, -1),
                        nxt="even")
        return dict(g_next=g_mix[layer], w_in=W["w_in_odd"][i], g_sgu=None, nxt="odd")

    h_in = x if batch_view else _to_time_major(x)
    mixer_in = _trunk_call(rows, h_in, mix=None, ffn=None, h_is_bld=batch_view, **next_in(0))
    h = None
    new_re, new_im, new_v, new_pool, new_conv = [], [], [], [], []
    for layer in range(depth):
        i = layer // 2
        if layer % 2 == 0:
            u_a, u_b, vn = mixer_in
            y_a, hre, him = _s5_call(u_a, s5_re[i].reshape(nb, -1), s5_im[i].reshape(nb, -1), W["s5"][i],
                                     nb=nb, t=t)
            y_b = _sgu_call(u_b, vn, W["sgu_w"][i], W["sgu_b"][i], nb=nb, t=t)
            new_re.append(hre.reshape(s5_re[i].shape))
            new_im.append(him.reshape(s5_im[i].shape))
            new_v.append(_to_batch_major(vn, nb))
            w_out = W["w_out_even"][i]
        else:
            x_c, z, b_g = mixer_in
            y_a, y_b, pout, cout = _poolconv_call(
                x_c, z, b_g, _to_time_major(pool_buf[i]), _to_time_major(conv_buf[i]), W["poolconv"][i],
                nb=nb, t=t, prefix_valid=prefix_valid)
            new_pool.append(_to_batch_major(pout, nb))
            new_conv.append(_to_batch_major(cout, nb))
            w_out = W["w_out_odd"][i]
        h_src = h if layer > 0 else h_in
        outs = _trunk_call(rows, h_src, mix=(y_a, y_b, w_out, g_ffn[layer]), ffn=W["ffn"][layer],
                           h_is_bld=batch_view and layer == 0,
                           out_is_bld=batch_view and layer == depth - 1, **next_in(layer + 1))
        if layer == depth - 1:
            y = outs[0]
        else:
            h, mixer_in = outs[0], outs[1:]
    if not batch_view:
        y = _to_batch_major(y, nb)
    return (y, jnp.stack(new_re), jnp.stack(new_im), jnp.stack(new_v), jnp.stack(new_pool),
            jnp.stack(new_conv))


def kernel(x_prompt, x_sample, state_s5_re, state_s5_im, state_pool, state_conv, norm_mix, norm_ffn, norm_final, w_in_even, w_out_even, s5_lambda_re, s5_lambda_im, s5_log_dt, s5_b_re, s5_b_im, s5_c_re, s5_c_im, s5_d, s5_glu_w, s5_glu_b, sgu_norm, sgu_w, sgu_b, w_in_odd, w_out_odd, pool_w, pool_scale, conv_w, conv_b, ffn_w_gate, ffn_w_up, ffn_w_down):
    n_even, n_odd = w_in_even.shape[0], w_in_odd.shape[0]
    depth = norm_mix.shape[0]
    s5 = []
    for i in range(n_even):
        a_re, a_im, bb_re, bb_im = _s5_prep(s5_lambda_re[i], s5_lambda_im[i], s5_log_dt[i], s5_b_re[i], s5_b_im[i])
        s5.append((a_re, a_im, bb_re, bb_im, _blockdiag_out(s5_c_re[i]), _blockdiag_out(s5_c_im[i]),
                   s5_d[i].reshape(1, -1), s5_glu_w[i].astype(BF16), s5_glu_b[i].reshape(1, -1)))
    W = dict(
        norm_mix=norm_mix, norm_ffn=norm_ffn, norm_final=norm_final,
        w_in_even=w_in_even.astype(BF16), w_out_even=w_out_even.astype(BF16),
        w_in_odd=w_in_odd.astype(BF16), w_out_odd=w_out_odd.astype(BF16),
        sgu_norm=sgu_norm, sgu_w=sgu_w, sgu_b=sgu_b, s5=s5,
        poolconv=[(pool_w[i].astype(BF16), pool_scale[i].reshape(1, -1), conv_w[i], conv_b[i].reshape(1, -1))
                  for i in range(n_odd)],
        ffn=[(ffn_w_gate[l].astype(BF16), ffn_w_up[l].astype(BF16), ffn_w_down[l].astype(BF16))
             for l in range(depth)],
    )
    bp = x_prompt.shape[0]
    z_s5 = jnp.zeros((n_even, bp) + state_s5_re.shape[2:], state_s5_re.dtype)
    z_pool = jnp.zeros((n_odd, bp) + state_pool.shape[2:], x_prompt.dtype)
    z_conv = jnp.zeros((n_odd, bp) + state_conv.shape[2:], x_prompt.dtype)
    y_p, p_re, p_im, _, p_pool, p_conv = _run_trunk(x_prompt, z_s5, z_s5, z_pool, z_conv, False, W, batch_view=True)
    y_s, s_re, s_im, s_v, s_pool, s_conv = _run_trunk(
        x_sample, state_s5_re, state_s5_im, state_pool, state_conv, True, W, batch_view=False)
    return (y_p, y_s, p_re, p_im, p_pool, p_conv, s_re, s_im, s_v, s_pool, s_conv)
```

```python
import functools

import jax
import jax.numpy as jnp
from jax import lax
from jax.experimental import pallas as pl
from jax.experimental.pallas import tpu as pltpu

F32 = jnp.float32
BF16 = jnp.bfloat16

EPS = 1e-6
SGU_CHUNK = 128
POOL_WINDOWS = (2, 4, 8, 16)
POOL_BUF = max(POOL_WINDOWS) - 1
CONV_W = 3

LANES = 128
SUBLANES = 8
ROW_TILE = 512
FF_CHUNK = 512
SCAN_COLS = 1024
VMEM_LIMIT = 56 * 1024 * 1024


def _dot(a, b):
    return jnp.dot(a, b, preferred_element_type=F32)


def _rms(x, g):
    return x * lax.rsqrt(jnp.mean(x * x, axis=-1, keepdims=True) + EPS) * g


def _round_up(n, m):
    return -(-n // m) * m


def _const_spec(shape):
    zeros = (0,) * len(shape)
    return pl.BlockSpec(shape, lambda *_: zeros, pipeline_mode=pl.Buffered(1))


def _row_spec(tile, c):
    return pl.BlockSpec((None, tile, c), lambda g, i: (g, i, 0))


def _params(*semantics):
    return pltpu.CompilerParams(dimension_semantics=semantics, vmem_limit_bytes=VMEM_LIMIT)


def _trunk_kernel(*refs, has_mix, nxt, d_half):
    refs = list(refs)
    h_ref = refs.pop(0)
    if has_mix:
        ya_ref, yb_ref, wo_ref, gf_ref, wg_ref, wu_ref, wd_ref = refs[:7]
        del refs[:7]
    gn_ref = refs.pop(0)
    if nxt != "final":
        win_ref = refs.pop(0)
    if nxt == "even":
        gs_ref = refs.pop(0)
    outs = refs

    h = h_ref[...]
    if has_mix:
        h = h + _dot(ya_ref[...], wo_ref[:d_half, :]) + _dot(yb_ref[...], wo_ref[d_half:, :])
        hn = _rms(h, gf_ref[...]).astype(BF16)
        d_ff = wg_ref.shape[1]
        acc = None
        for c0 in range(0, d_ff, FF_CHUNK):
            c1 = min(c0 + FF_CHUNK, d_ff)
            g = _dot(hn, wg_ref[:, c0:c1])
            u = _dot(hn, wu_ref[:, c0:c1])
            a = (g * jax.nn.sigmoid(g) * u).astype(BF16)
            part = _dot(a, wd_ref[c0:c1, :])
            acc = part if acc is None else acc + part
        h = h + acc
        if nxt != "final":
            outs.pop(0)[...] = h
    hn = _rms(h, gn_ref[...])
    if nxt == "final":
        outs[0][...] = hn
        return
    proj = _dot(hn.astype(BF16), win_ref[...])
    if nxt == "even":
        ua_ref, ub_ref, vn_ref = outs
        ua_ref[...] = proj[:, :d_half]
        ub_ref[...] = proj[:, d_half:2 * d_half]
        vn_ref[...] = _rms(proj[:, 2 * d_half:], gs_ref[...])
    else:
        xc_ref, z_ref, bg_ref = outs
        xc_ref[...] = proj[:, :d_half]
        z_ref[...] = proj[:, 3 * d_half:] * proj[:, d_half:2 * d_half]
        bg_ref[...] = proj[:, 2 * d_half:3 * d_half]


def _trunk_call(h, *, mix, ffn, g_next, w_in, g_sgu, nxt):
    n_grp, r, d = h.shape
    d_half = d // 2
    tile = min(ROW_TILE, r)
    args, specs = [h], [_row_spec(tile, d)]
    if mix is not None:
        ya, yb, w_out, g_ffn = mix
        args += [ya, yb, w_out, g_ffn, *ffn]
        specs += [_row_spec(tile, d_half), _row_spec(tile, d_half), _const_spec(w_out.shape),
                  _const_spec(g_ffn.shape)]
        specs += [_const_spec(w.shape) for w in ffn]
    args.append(g_next)
    specs.append(_const_spec(g_next.shape))
    if nxt != "final":
        args.append(w_in)
        specs.append(_const_spec(w_in.shape))
    if nxt == "even":
        args.append(g_sgu)
        specs.append(_const_spec(g_sgu.shape))

    full = jax.ShapeDtypeStruct((n_grp, r, d), F32)
    half = jax.ShapeDtypeStruct((n_grp, r, d_half), F32)
    if nxt == "final":
        out_shapes, out_specs = [full], [_row_spec(tile, d)]
    else:
        out_shapes, out_specs = [half] * 3, [_row_spec(tile, d_half)] * 3
        if mix is not None:
            out_shapes, out_specs = [full] + out_shapes, [_row_spec(tile, d)] + out_specs

    return pl.pallas_call(
        functools.partial(_trunk_kernel, has_mix=mix is not None, nxt=nxt, d_half=d_half),
        grid=(n_grp, r // tile),
        in_specs=specs,
        out_specs=out_specs,
        out_shape=out_shapes,
        compiler_params=_params("arbitrary", "arbitrary"),
        name=f"trunk_{'mix' if mix is not None else 'in'}_{nxt}",
    )(*args)


def _s5_prep_kernel(lr_ref, li_ref, ldt_ref, bre_ref, bim_ref, are_ref, aim_ref, bbre_ref, bbim_ref):
    lr = lr_ref[...]
    li = li_ref[...]
    dt = jnp.exp(ldt_ref[...])
    mag = jnp.exp(lr * dt)
    ang = li * dt
    ab_re = mag * jnp.cos(ang)
    ab_im = mag * jnp.sin(ang)
    den = lr * lr + li * li
    f_re = ((ab_re - 1.0) * lr + ab_im * li) / den
    f_im = (ab_im * lr - (ab_re - 1.0) * li) / den
    are_ref[...] = ab_re
    aim_ref[...] = ab_im
    for g in range(lr.shape[0]):
        fr = f_re[g:g + 1, :]
        fi = f_im[g:g + 1, :]
        br = bre_ref[g]
        bi = bim_ref[g]
        bbre_ref[g] = fr * br - fi * bi
        bbim_ref[g] = fr * bi + fi * br


def _s5_prep(lam_re, lam_im, log_dt, b_re, b_im):
    g, n = lam_re.shape
    p = b_re.shape[-1]
    bt_re = jnp.swapaxes(b_re, 1, 2)
    bt_im = jnp.swapaxes(b_im, 1, 2)
    a_re, a_im, bb_re, bb_im = pl.pallas_call(
        _s5_prep_kernel,
        out_shape=[jax.ShapeDtypeStruct((g, n), F32)] * 2 + [jax.ShapeDtypeStruct((g, p, n), F32)] * 2,
        name="s5_discretize",
    )(lam_re, lam_im, log_dt.reshape(g, 1), bt_re, bt_im)
    gs = LANES // p
    eye = jnp.eye(gs, dtype=F32)

    def blockdiag_in(bb):
        return jnp.einsum("kgpn,gh->kgphn", bb.reshape(g // gs, gs, p, n), eye).reshape(
            g // gs, gs * p, gs * n).astype(BF16)

    return a_re.reshape(1, g * n), a_im.reshape(1, g * n), blockdiag_in(bb_re), blockdiag_in(bb_im)


def _blockdiag_out(c):
    g, p, n = c.shape
    gs = LANES // p
    eye = jnp.eye(gs, dtype=F32)
    return jnp.einsum("kgpn,gh->kgnhp", c.reshape(g // gs, gs, p, n), eye).reshape(
        g // gs, gs * n, gs * p).astype(BF16)


def _s5_kernel(u_ref, h0re_ref, h0im_ref, are_ref, aim_ref, bbre_ref, bbim_ref, cre_ref, cim_ref,
               d_ref, gw_ref, gb_ref, y_ref, hre_out, him_out, ut, yt, xre, xim, hre_s, him_s,
               *, n_grp, nb, tb):
    i = pl.program_id(0)
    n_seq = n_grp * nb

    @pl.when(i == 0)
    def _():
        hre_s[...] = h0re_ref[...]
        him_s[...] = h0im_ref[...]

    n_slab, slab_in, slab_st = bbre_ref.shape
    for k in range(n_slab):
        lanes = slice(k * slab_in, (k + 1) * slab_in)
        if n_grp == 1:
            ut[k] = u_ref[0, :, lanes]
        else:
            for b in range(n_grp):
                ut[k, pl.ds(b, tb, stride=n_grp), :] = u_ref[b, :, lanes]
    for k in range(n_slab):
        uk = ut[k].astype(BF16)
        xre[:, k * slab_st:(k + 1) * slab_st] = _dot(uk, bbre_ref[k])
        xim[:, k * slab_st:(k + 1) * slab_st] = _dot(uk, bbim_ref[k])

    n_state = xre.shape[1]
    for c0 in range(0, n_state, SCAN_COLS):
        cols = slice(c0, c0 + SCAN_COLS)
        ar = jnp.broadcast_to(are_ref[:, cols], (SUBLANES, SCAN_COLS))
        ai = jnp.broadcast_to(aim_ref[:, cols], (SUBLANES, SCAN_COLS))
        for r0 in range(0, n_seq, SUBLANES):
            def step(t, carry, r0=r0, cols=cols, ar=ar, ai=ai):
                hr, hi = carry
                r = pl.multiple_of(t * n_seq + r0, SUBLANES)
                nr = ar * hr - ai * hi + xre[pl.ds(r, SUBLANES), cols]
                ni = ar * hi + ai * hr + xim[pl.ds(r, SUBLANES), cols]
                xre[pl.ds(r, SUBLANES), cols] = nr
                xim[pl.ds(r, SUBLANES), cols] = ni
                return nr, ni

            hr, hi = lax.fori_loop(0, tb, step, (hre_s[r0:r0 + SUBLANES, cols], him_s[r0:r0 + SUBLANES, cols]))
            hre_s[r0:r0 + SUBLANES, cols] = hr
            him_s[r0:r0 + SUBLANES, cols] = hi

    parts = []
    for k in range(n_slab):
        hr = xre[:, k * slab_st:(k + 1) * slab_st].astype(BF16)
        hi = xim[:, k * slab_st:(k + 1) * slab_st].astype(BF16)
        lanes = slice(k * slab_in, (k + 1) * slab_in)
        parts.append(_dot(hr, cre_ref[k]) - _dot(hi, cim_ref[k]) + d_ref[:, lanes] * ut[k])
    z = jax.nn.gelu(jnp.concatenate(parts, axis=-1))
    out = z * jax.nn.sigmoid(_dot(z.astype(BF16), gw_ref[...]) + gb_ref[...])
    for k in range(n_slab):
        lanes = slice(k * slab_in, (k + 1) * slab_in)
        if n_grp == 1:
            y_ref[0, :, lanes] = out[:, lanes].astype(y_ref.dtype)
        else:
            yt[k] = out[:, lanes]
            for b in range(n_grp):
                y_ref[b, :, lanes] = yt[k, pl.ds(b, tb, stride=n_grp), :].astype(y_ref.dtype)
    hre_out[...] = hre_s[...]
    him_out[...] = him_s[...]


def _s5_call(u, h0_re, h0_im, prm, *, nb):
    a_re, a_im, bb_re, bb_im, c_re, c_im, d_skip, glu_w, glu_b = prm
    n_grp, r, d_a = u.shape
    assert n_grp == 1 or nb == 1
    t = r // nb
    n_seq = n_grp * nb
    n_state = a_re.shape[1]
    tb = max(1, min(t, ROW_TILE // n_seq))
    rows = tb * n_seq
    consts = [a_re, a_im, bb_re, bb_im, c_re, c_im, d_skip, glu_w, glu_b]
    state_spec = pl.BlockSpec((n_seq, n_state), lambda i: (0, 0))
    blk = pl.BlockSpec((n_grp, tb * nb, d_a), lambda i: (0, i, 0))
    return pl.pallas_call(
        functools.partial(_s5_kernel, n_grp=n_grp, nb=nb, tb=tb),
        grid=(t // tb,),
        in_specs=[blk, state_spec, state_spec] + [_const_spec(c.shape) for c in consts],
        out_specs=[blk, state_spec, state_spec],
        out_shape=[jax.ShapeDtypeStruct(u.shape, BF16),
                   jax.ShapeDtypeStruct((n_seq, n_state), F32),
                   jax.ShapeDtypeStruct((n_seq, n_state), F32)],
        scratch_shapes=[pltpu.VMEM((d_a // LANES, rows, LANES), F32), pltpu.VMEM((d_a // LANES, rows, LANES), F32),
                        pltpu.VMEM((rows, n_state), F32), pltpu.VMEM((rows, n_state), F32),
                        pltpu.VMEM((n_seq, n_state), F32), pltpu.VMEM((n_seq, n_state), F32)],
        compiler_params=_params("arbitrary"),
        name="s5_mixer",
    )(u, h0_re, h0_im, *consts)


def _sgu_chunks_kernel(u_ref, v_ref, w_ref, bias_ref, y_ref):
    n_heads, cl, _ = w_ref.shape
    hd = v_ref.shape[1] // n_heads
    row = lax.broadcasted_iota(jnp.int32, (cl, cl), 0)
    col = lax.broadcasted_iota(jnp.int32, (cl, cl), 1)
    w = [jnp.where(col <= row, w_ref[h], 0.0).astype(BF16) for h in range(n_heads)]
    heads_per_slab = LANES // hd
    lane = lax.broadcasted_iota(jnp.int32, (cl, LANES), 1)
    for c in range(v_ref.shape[0] // cl):
        r = slice(c * cl, (c + 1) * cl)
        slabs = []
        for s in range(n_heads // heads_per_slab):
            v = v_ref[r, s * LANES:(s + 1) * LANES].astype(BF16)
            mixed = _dot(w[s * heads_per_slab], v)
            for j in range(1, heads_per_slab):
                mixed = jnp.where(lane >= j * hd, _dot(w[s * heads_per_slab + j], v), mixed)
            slabs.append(mixed)
        mixed = jnp.concatenate(slabs, axis=-1) + bias_ref[...]
        y_ref[r, :] = (u_ref[r, :] * mixed).astype(y_ref.dtype)


def _sgu_short_kernel(u_ref, v_ref, wl_ref, bias_ref, y_ref, *, nb, t):
    for i in range(t):
        mixed = bias_ref[i:i + 1, :]
        for j in range(i + 1):
            mixed = mixed + wl_ref[i, j:j + 1, :] * v_ref[0, j * nb:(j + 1) * nb, :]
        y_ref[0, i * nb:(i + 1) * nb, :] = (u_ref[0, i * nb:(i + 1) * nb, :] * mixed).astype(y_ref.dtype)


def _sgu_call(u, vn, w_s, b_s, *, nb):
    n_grp, r, d_b = u.shape
    t = r // nb
    n_heads = w_s.shape[0]
    hd = d_b // n_heads
    cl = min(t, SGU_CHUNK)
    bias = jnp.repeat(b_s[:, :cl].T, hd, axis=1)
    if nb == 1 and t % SGU_CHUNK == 0:
        tile = min(ROW_TILE, r)
        return pl.pallas_call(
            _sgu_chunks_kernel,
            grid=(n_grp, r // tile),
            in_specs=[_row_spec(tile, d_b), _row_spec(tile, d_b), _const_spec(w_s.shape), _const_spec(bias.shape)],
            out_specs=_row_spec(tile, d_b),
            out_shape=jax.ShapeDtypeStruct(u.shape, BF16),
            compiler_params=_params("arbitrary", "arbitrary"),
            name="sgu_chunks",
        )(u, vn, w_s, bias)
    assert n_grp == 1 and t < SGU_CHUNK
    wl = jnp.repeat(jnp.transpose(w_s[:, :cl, :cl], (1, 2, 0)), hd, axis=2)
    return pl.pallas_call(
        functools.partial(_sgu_short_kernel, nb=nb, t=t),
        out_shape=jax.ShapeDtypeStruct(u.shape, BF16),
        name="sgu_short",
    )(u, vn, wl, bias)


def _poolconv_kernel(xc_ref, z_ref, bg_ref, pbuf_ref, cbuf_ref, pw_ref, ps_ref, cw_ref, cb_ref,
                     yc_ref, yd_ref, pout_ref, cout_ref, fullc, fullz, *, nb, tb, prefix_valid):
    i = pl.program_id(1)
    hp, hc = POOL_BUF * nb, (CONV_W - 1) * nb
    p0, c0 = pbuf_ref.shape[0], cbuf_ref.shape[0]
    r = tb * nb

    @pl.when(i == 0)
    def _():
        fullc[0:p0, :] = pbuf_ref[...]
        fullz[0:c0, :] = cbuf_ref[...]

    xc = xc_ref[...]
    fullc[p0:p0 + r, :] = xc
    fullz[c0:c0 + r, :] = z_ref[...]

    conv = cb_ref[...]
    for k in range(CONV_W):
        back = (CONV_W - 1 - k) * nb
        conv = conv + fullz[c0 - back:c0 - back + r, :] * cw_ref[k:k + 1, :]
    yd_ref[...] = (bg_ref[...] * conv).astype(yd_ref.dtype)

    t_glob = i * tb + lax.broadcasted_iota(jnp.int32, (r, 1), 0) // nb
    cg = xc.shape[1] // len(POOL_WINDOWS)
    for gi, w in enumerate(POOL_WINDOWS):
        lanes = slice(gi * cg, (gi + 1) * cg)
        s = xc[:, lanes]
        for k in range(1, w):
            s = s + fullc[p0 - k * nb:p0 - k * nb + r, lanes]
        if prefix_valid:
            n = jnp.full((r, 1), float(w), F32)
        else:
            n = jnp.minimum(t_glob + 1, w).astype(F32)
        diff = s / n - xc[:, lanes]
        yc = _dot(diff.astype(BF16), pw_ref[gi]) * ps_ref[:, lanes]
        yc_ref[:, lanes] = yc.astype(yc_ref.dtype)

    pout_ref[...] = fullc[p0 + r - hp:p0 + r, :]
    cout_ref[...] = fullz[c0 + r - hc:c0 + r, :]
    new_pool = fullc[r:r + p0, :]
    new_conv = fullz[r:r + c0, :]
    fullc[0:p0, :] = new_pool
    fullz[0:c0, :] = new_conv


def _poolconv_call(xc, z, bg, pool_buf, conv_buf, prm, *, nb, prefix_valid):
    pool_w, pool_scale, conv_w, conv_b = prm
    n_grp, rtot, c = xc.shape
    r = min(ROW_TILE, rtot)
    tb = r // nb
    hp, hc = POOL_BUF * nb, (CONV_W - 1) * nb
    p0, c0 = _round_up(hp, SUBLANES), _round_up(hc, SUBLANES)
    pool_buf = jnp.pad(pool_buf, ((0, 0), (p0 - hp, 0), (0, 0)))
    conv_buf = jnp.pad(conv_buf, ((0, 0), (c0 - hc, 0), (0, 0)))
    consts = [pool_w, pool_scale, conv_w, conv_b]

    def hist_spec(rows):
        return pl.BlockSpec((None, rows, c), lambda g, i: (g, 0, 0))

    return pl.pallas_call(
        functools.partial(_poolconv_kernel, nb=nb, tb=tb, prefix_valid=prefix_valid),
        grid=(n_grp, rtot // r),
        in_specs=[_row_spec(r, c)] * 3 + [hist_spec(p0), hist_spec(c0)] + [_const_spec(a.shape) for a in consts],
        out_specs=[_row_spec(r, c)] * 2 + [hist_spec(hp), hist_spec(hc)],
        out_shape=[jax.ShapeDtypeStruct(xc.shape, BF16), jax.ShapeDtypeStruct(xc.shape, BF16),
                   jax.ShapeDtypeStruct((n_grp, hp, c), F32), jax.ShapeDtypeStruct((n_grp, hc, c), F32)],
        scratch_shapes=[pltpu.VMEM((p0 + r, c), F32), pltpu.VMEM((c0 + r, c), F32)],
        compiler_params=_params("arbitrary", "arbitrary"),
        name="pool_conv",
    )(xc, z, bg, pool_buf, conv_buf, *consts)


def _interleave(a):
    return jnp.swapaxes(a, 0, 1).reshape(1, a.shape[0] * a.shape[1], a.shape[2])


def _deinterleave(a, nb):
    return jnp.swapaxes(a.reshape(-1, nb, a.shape[2]), 0, 1)


def _run_trunk(x, s5_re, s5_im, pool_buf, conv_buf, prefix_valid, W, *, interleaved):
    batch, t, d = x.shape
    depth = W["norm_mix"].shape[0]
    nb = batch if interleaved else 1
    pack = _interleave if interleaved else (lambda a: a)
    unpack = (lambda a: _deinterleave(a, nb)) if interleaved else (lambda a: a)
    g_mix = W["norm_mix"].reshape(depth, 1, d)
    g_ffn = W["norm_ffn"].reshape(depth, 1, d)
    g_fin = W["norm_final"].reshape(1, d)

    def next_in(layer):
        if layer == depth:
            return dict(g_next=g_fin, w_in=None, g_sgu=None, nxt="final")
        i = layer // 2
        if layer % 2 == 0:
            return dict(g_next=g_mix[layer], w_in=W["w_in_even"][i], g_sgu=W["sgu_norm"][i].reshape(1, -1),
                        nxt="even")
        return dict(g_next=g_mix[layer], w_in=W["w_in_odd"][i], g_sgu=None, nxt="odd")

    h = pack(x)
    mixer_in = _trunk_call(h, mix=None, ffn=None, **next_in(0))
    new_re, new_im, new_v, new_pool, new_conv = [], [], [], [], []
    for layer in range(depth):
        i = layer // 2
        if layer % 2 == 0:
            u_a, u_b, vn = mixer_in
            y_a, hre, him = _s5_call(u_a, s5_re[i].reshape(batch, -1), s5_im[i].reshape(batch, -1), W["s5"][i],
                                     nb=nb)
            y_b = _sgu_call(u_b, vn, W["sgu_w"][i], W["sgu_b"][i], nb=nb)
            new_re.append(hre.reshape(s5_re[i].shape))
            new_im.append(him.reshape(s5_im[i].shape))
            new_v.append(unpack(vn))
            w_out = W["w_out_even"][i]
        else:
            x_c, z, b_g = mixer_in
            y_a, y_b, pout, cout = _poolconv_call(x_c, z, b_g, pack(pool_buf[i]), pack(conv_buf[i]),
                                                  W["poolconv"][i], nb=nb, prefix_valid=prefix_valid)
            new_pool.append(unpack(pout))
            new_conv.append(unpack(cout))
            w_out = W["w_out_odd"][i]
        outs = _trunk_call(h, mix=(y_a, y_b, w_out, g_ffn[layer]), ffn=W["ffn"][layer], **next_in(layer + 1))
        h, mixer_in = outs[0], outs[1:]
    return (unpack(h), jnp.stack(new_re), jnp.stack(new_im), jnp.stack(new_v), jnp.stack(new_pool),
            jnp.stack(new_conv))


def kernel(x_prompt, x_sample, state_s5_re, state_s5_im, state_pool, state_conv, norm_mix, norm_ffn, norm_final, w_in_even, w_out_even, s5_lambda_re, s5_lambda_im, s5_log_dt, s5_b_re, s5_b_im, s5_c_re, s5_c_im, s5_d, s5_glu_w, s5_glu_b, sgu_norm, sgu_w, sgu_b, w_in_odd, w_out_odd, pool_w, pool_scale, conv_w, conv_b, ffn_w_gate, ffn_w_up, ffn_w_down):
    n_even, n_odd = w_in_even.shape[0], w_in_odd.shape[0]
    depth = norm_mix.shape[0]
    s5 = []
    for i in range(n_even):
        a_re, a_im, bb_re, bb_im = _s5_prep(s5_lambda_re[i], s5_lambda_im[i], s5_log_dt[i], s5_b_re[i], s5_b_im[i])
        s5.append((a_re, a_im, bb_re, bb_im, _blockdiag_out(s5_c_re[i]), _blockdiag_out(s5_c_im[i]),
                   s5_d[i].reshape(1, -1), s5_glu_w[i].astype(BF16), s5_glu_b[i].reshape(1, -1)))
    W = dict(
        norm_mix=norm_mix, norm_ffn=norm_ffn, norm_final=norm_final,
        w_in_even=w_in_even.astype(BF16), w_out_even=w_out_even.astype(BF16),
        w_in_odd=w_in_odd.astype(BF16), w_out_odd=w_out_odd.astype(BF16),
        sgu_norm=sgu_norm, sgu_w=sgu_w, sgu_b=sgu_b, s5=s5,
        poolconv=[(pool_w[i].astype(BF16), pool_scale[i].reshape(1, -1), conv_w[i], conv_b[i].reshape(1, -1))
                  for i in range(n_odd)],
        ffn=[(ffn_w_gate[l].astype(BF16), ffn_w_up[l].astype(BF16), ffn_w_down[l].astype(BF16))
             for l in range(depth)],
    )
    bp = x_prompt.shape[0]
    z_s5 = jnp.zeros((n_even, bp) + state_s5_re.shape[2:], state_s5_re.dtype)
    z_pool = jnp.zeros((n_odd, bp) + state_pool.shape[2:], x_prompt.dtype)
    z_conv = jnp.zeros((n_odd, bp) + state_conv.shape[2:], x_prompt.dtype)
    y_p, p_re, p_im, _, p_pool, p_conv = _run_trunk(x_prompt, z_s5, z_s5, z_pool, z_conv, False, W,
                                                    interleaved=False)
    y_s, s_re, s_im, s_v, s_pool, s_conv = _run_trunk(
        x_sample, state_s5_re, state_s5_im, state_pool, state_conv, True, W, interleaved=True)
    return (y_p, y_s, p_re, p_im, p_pool, p_conv, s_re, s_im, s_v, s_pool, s_conv)
```

```python
import functools

import jax
import jax.numpy as jnp
from jax import lax
from jax.experimental import pallas as pl
from jax.experimental.pallas import tpu as pltpu

F32 = jnp.float32
BF16 = jnp.bfloat16

EPS = 1e-6
SGU_CHUNK = 128
POOL_WINDOWS = (2, 4, 8, 16)
POOL_BUF = max(POOL_WINDOWS) - 1
CONV_W = 3

LANES = 128
SUBLANES = 8
ROW_TILE = 512
FF_CHUNK = 512
SCAN_COLS = 1024
SCAN_UNROLL = 4
VMEM_LIMIT = 56 * 1024 * 1024


def _dot(a, b):
    return jnp.dot(a, b, preferred_element_type=F32)


def _rms(x, g):
    return x * lax.rsqrt(jnp.mean(x * x, axis=-1, keepdims=True) + EPS) * g


def _round_up(n, m):
    return -(-n // m) * m


def _const_spec(shape):
    zeros = (0,) * len(shape)
    return pl.BlockSpec(shape, lambda *_: zeros, pipeline_mode=pl.Buffered(1))


def _layer_spec(stack, layer):
    zeros = (0,) * (stack.ndim - 1)
    return pl.BlockSpec((None,) + stack.shape[1:], lambda *_: (layer,) + zeros, pipeline_mode=pl.Buffered(1))


def _row_spec(tile, c):
    return pl.BlockSpec((None, tile, c), lambda g, i: (g, i, 0))


def _params(*semantics):
    return pltpu.CompilerParams(dimension_semantics=semantics, vmem_limit_bytes=VMEM_LIMIT)


def _poolconv(xc, z, bg, i, pbuf_ref, cbuf_ref, pw_ref, ps_ref, cw_ref, cb_ref,
              yc_ref, yd_ref, pout_ref, cout_ref, fullc, fullz, diff, *, nb, prefix_valid):
    hp, hc = POOL_BUF * nb, (CONV_W - 1) * nb
    p0, c0 = pbuf_ref.shape[0], cbuf_ref.shape[0]
    r = xc.shape[0]

    @pl.when(i == 0)
    def _():
        fullc[0:p0, :] = pbuf_ref[...]
        fullz[0:c0, :] = cbuf_ref[...]

    fullc[p0:p0 + r, :] = xc
    fullz[c0:c0 + r, :] = z

    conv = cb_ref[...]
    for k in range(CONV_W):
        back = (CONV_W - 1 - k) * nb
        conv = conv + fullz[c0 - back:c0 - back + r, :] * cw_ref[k:k + 1, :]
    yd_ref[...] = (bg * conv).astype(yd_ref.dtype)

    cg = xc.shape[1] // len(POOL_WINDOWS)

    def window_sum(rows, lanes, w):
        s = fullc[p0:p0 + rows, lanes]
        for k in range(1, w):
            s = s + fullc[p0 - k * nb:p0 - k * nb + rows, lanes]
        return s

    for gi, w in enumerate(POOL_WINDOWS):
        lanes = slice(gi * cg, (gi + 1) * cg)
        diff[:, lanes] = window_sum(r, lanes, w) * (1.0 / w) - xc[:, lanes]
    if not prefix_valid:
        @pl.when(i == 0)
        def _():
            head = min(p0, r)
            t = lax.broadcasted_iota(jnp.int32, (head, 1), 0) // nb
            for gi, w in enumerate(POOL_WINDOWS):
                lanes = slice(gi * cg, (gi + 1) * cg)
                n = jnp.minimum(t + 1, w).astype(F32)
                diff[0:head, lanes] = window_sum(head, lanes, w) / n - fullc[p0:p0 + head, lanes]
    for gi in range(len(POOL_WINDOWS)):
        lanes = slice(gi * cg, (gi + 1) * cg)
        yc = _dot(diff[:, lanes].astype(BF16), pw_ref[gi]) * ps_ref[:, lanes]
        yc_ref[:, lanes] = yc.astype(yc_ref.dtype)

    pout_ref[...] = fullc[p0 + r - hp:p0 + r, :]
    cout_ref[...] = fullz[c0 + r - hc:c0 + r, :]
    new_pool = fullc[r:r + p0, :]
    new_conv = fullz[r:r + c0, :]
    fullc[0:p0, :] = new_pool
    fullz[0:c0, :] = new_conv


def _trunk_kernel(*refs, has_mix, nxt, d_half, nb, prefix_valid):
    refs = list(refs)
    h_ref = refs.pop(0)
    if has_mix:
        ya_ref, yb_ref, wo_ref, gf_ref, wg_ref, wu_ref, wd_ref = refs[:7]
        del refs[:7]
    gn_ref = refs.pop(0)
    if nxt != "final":
        win_ref = refs.pop(0)
    if nxt == "even":
        gs_ref = refs.pop(0)
    if nxt == "odd":
        odd_in = refs[:6]
        del refs[:6]
    outs = refs

    h = h_ref[...]
    if has_mix:
        h = h + _dot(ya_ref[...], wo_ref[:d_half, :]) + _dot(yb_ref[...], wo_ref[d_half:, :])
        hn = _rms(h, gf_ref[...]).astype(BF16)
        d_ff = wg_ref.shape[1]
        acc = None
        for c0 in range(0, d_ff, FF_CHUNK):
            c1 = min(c0 + FF_CHUNK, d_ff)
            g = _dot(hn, wg_ref[:, c0:c1])
            u = _dot(hn, wu_ref[:, c0:c1])
            a = (g * jax.nn.sigmoid(g) * u).astype(BF16)
            part = _dot(a, wd_ref[c0:c1, :])
            acc = part if acc is None else acc + part
        h = h + acc
        if nxt != "final":
            outs.pop(0)[...] = h
    hn = _rms(h, gn_ref[...])
    if nxt == "final":
        outs[0][...] = hn
        return
    proj = _dot(hn.astype(BF16), win_ref[...])
    if nxt == "even":
        ua_ref, ub_ref, vn_ref = outs
        ua_ref[...] = proj[:, :d_half]
        ub_ref[...] = proj[:, d_half:2 * d_half]
        vn_ref[...] = _rms(proj[:, 2 * d_half:], gs_ref[...])
    else:
        xc = proj[:, :d_half]
        z = proj[:, 3 * d_half:] * proj[:, d_half:2 * d_half]
        bg = proj[:, 2 * d_half:3 * d_half]
        _poolconv(xc, z, bg, pl.program_id(1), *odd_in, *outs, nb=nb, prefix_valid=prefix_valid)


def _trunk_call(h, layer, W, *, mix, nxt, nb, prefix_valid, odd_hist=None):
    n_grp, r, d = h.shape
    d_half = d // 2
    tile = min(ROW_TILE, r)
    i = layer // 2
    args, specs = [h], [_row_spec(tile, d)]
    if mix is not None:
        ya, yb, w_out, wi = mix
        ffn = [W["norm_ffn"], W["ffn_gate"], W["ffn_up"], W["ffn_down"]]
        args += [ya, yb, w_out] + ffn
        specs += [_row_spec(tile, d_half), _row_spec(tile, d_half), _layer_spec(w_out, wi)]
        specs += [_layer_spec(w, layer - 1) for w in ffn]
    if nxt == "final":
        args.append(W["norm_final"])
        specs.append(_const_spec(W["norm_final"].shape))
    else:
        w_in = W["w_in_even"] if nxt == "even" else W["w_in_odd"]
        args += [W["norm_mix"], w_in]
        specs += [_layer_spec(W["norm_mix"], layer), _layer_spec(w_in, i)]
    scratch = []
    if nxt == "even":
        args.append(W["sgu_norm"])
        specs.append(_layer_spec(W["sgu_norm"], i))
        out_shapes = [jax.ShapeDtypeStruct((n_grp, r, d_half), F32)] * 3
        out_specs = [_row_spec(tile, d_half)] * 3
    elif nxt == "odd":
        pool_buf, conv_buf = odd_hist
        hp, hc = POOL_BUF * nb, (CONV_W - 1) * nb
        p0, c0 = _round_up(hp, SUBLANES), _round_up(hc, SUBLANES)
        pool_buf = jnp.pad(pool_buf, ((0, 0), (p0 - hp, 0), (0, 0)))
        conv_buf = jnp.pad(conv_buf, ((0, 0), (c0 - hc, 0), (0, 0)))

        def hist_spec(rows):
            return pl.BlockSpec((None, rows, d_half), lambda g, j: (g, 0, 0))

        odd_w = [W["pool_w"], W["pool_scale"], W["conv_w"], W["conv_b"]]
        args += [pool_buf, conv_buf] + odd_w
        specs += [hist_spec(p0), hist_spec(c0)] + [_layer_spec(w, i) for w in odd_w]
        scratch = [pltpu.VMEM((p0 + tile, d_half), F32), pltpu.VMEM((c0 + tile, d_half), F32),
                   pltpu.VMEM((tile, d_half), F32)]
        out_shapes = [jax.ShapeDtypeStruct((n_grp, r, d_half), BF16)] * 2 + [
            jax.ShapeDtypeStruct((n_grp, hp, d_half), F32), jax.ShapeDtypeStruct((n_grp, hc, d_half), F32)]
        out_specs = [_row_spec(tile, d_half)] * 2 + [hist_spec(hp), hist_spec(hc)]
    else:
        out_shapes, out_specs = [], []
    if mix is not None or nxt == "final":
        out_shapes = [jax.ShapeDtypeStruct((n_grp, r, d), F32)] + out_shapes
        out_specs = [_row_spec(tile, d)] + out_specs

    return pl.pallas_call(
        functools.partial(_trunk_kernel, has_mix=mix is not None, nxt=nxt, d_half=d_half, nb=nb,
                          prefix_valid=prefix_valid),
        grid=(n_grp, r // tile),
        in_specs=specs,
        out_specs=out_specs,
        out_shape=out_shapes,
        scratch_shapes=scratch,
        compiler_params=_params("arbitrary", "arbitrary"),
        name=f"trunk_{'mix' if mix is not None else 'in'}_{nxt}",
    )(*args)


def _s5_prep_kernel(lr_ref, li_ref, ldt_ref, bre_ref, bim_ref, are_ref, aim_ref, bbre_ref, bbim_ref):
    lr = lr_ref[...]
    li = li_ref[...]
    dt = jnp.exp(ldt_ref[...])
    mag = jnp.exp(lr * dt)
    ang = li * dt
    ab_re = mag * jnp.cos(ang)
    ab_im = mag * jnp.sin(ang)
    den = lr * lr + li * li
    f_re = ((ab_re - 1.0) * lr + ab_im * li) / den
    f_im = (ab_im * lr - (ab_re - 1.0) * li) / den
    are_ref[...] = ab_re
    aim_ref[...] = ab_im
    for g in range(lr.shape[0]):
        fr = f_re[g:g + 1, :]
        fi = f_im[g:g + 1, :]
        br = bre_ref[g]
        bi = bim_ref[g]
        bbre_ref[g] = fr * br - fi * bi
        bbim_ref[g] = fr * bi + fi * br


def _s5_prep(lam_re, lam_im, log_dt, b_re, b_im):
    g, n = lam_re.shape
    p = b_re.shape[-1]
    bt_re = jnp.swapaxes(b_re, 1, 2)
    bt_im = jnp.swapaxes(b_im, 1, 2)
    a_re, a_im, bb_re, bb_im = pl.pallas_call(
        _s5_prep_kernel,
        out_shape=[jax.ShapeDtypeStruct((g, n), F32)] * 2 + [jax.ShapeDtypeStruct((g, p, n), F32)] * 2,
        name="s5_discretize",
    )(lam_re, lam_im, log_dt.reshape(g, 1), bt_re, bt_im)
    gs = LANES // p
    eye = jnp.eye(gs, dtype=F32)

    def blockdiag_in(bb):
        return jnp.einsum("kgpn,gh->kgphn", bb.reshape(g // gs, gs, p, n), eye).reshape(
            g // gs, gs * p, gs * n).astype(BF16)

    def sublane_rows(a):
        return jnp.broadcast_to(a.reshape(1, g * n), (SUBLANES, g * n))

    return sublane_rows(a_re), sublane_rows(a_im), blockdiag_in(bb_re), blockdiag_in(bb_im)


def _blockdiag_out(c):
    g, p, n = c.shape
    gs = LANES // p
    eye = jnp.eye(gs, dtype=F32)
    return jnp.einsum("kgpn,gh->kgnhp", c.reshape(g // gs, gs, p, n), eye).reshape(
        g // gs, gs * n, gs * p).astype(BF16)


def _s5_kernel(u_ref, h0re_ref, h0im_ref, are_ref, aim_ref, bbre_ref, bbim_ref, cre_ref, cim_ref,
               d_ref, gw_ref, gb_ref, y_ref, hre_out, him_out, ut, yt, xre, xim, hre_s, him_s,
               *, n_grp, nb, tb):
    i = pl.program_id(0)
    n_seq = n_grp * nb

    @pl.when(i == 0)
    def _():
        hre_s[...] = h0re_ref[...]
        him_s[...] = h0im_ref[...]

    n_slab, slab_in, slab_st = bbre_ref.shape
    for k in range(n_slab):
        lanes = slice(k * slab_in, (k + 1) * slab_in)
        if n_grp == 1:
            ut[k] = u_ref[0, :, lanes]
        else:
            for b in range(n_grp):
                ut[k, pl.ds(b, tb, stride=n_grp), :] = u_ref[b, :, lanes]
    for k in range(n_slab):
        uk = ut[k].astype(BF16)
        xre[:, k * slab_st:(k + 1) * slab_st] = _dot(uk, bbre_ref[k])
        xim[:, k * slab_st:(k + 1) * slab_st] = _dot(uk, bbim_ref[k])

    n_state = xre.shape[1]
    for c0 in range(0, n_state, SCAN_COLS):
        cols = slice(c0, c0 + SCAN_COLS)
        for r0 in range(0, n_seq, SUBLANES):
            def step(t, carry, r0=r0, cols=cols):
                hr, hi = carry
                ar = are_ref[:, cols]
                ai = aim_ref[:, cols]
                r = pl.multiple_of(t * n_seq + r0, SUBLANES)
                nr = ar * hr - ai * hi + xre[pl.ds(r, SUBLANES), cols]
                ni = ar * hi + ai * hr + xim[pl.ds(r, SUBLANES), cols]
                xre[pl.ds(r, SUBLANES), cols] = nr
                xim[pl.ds(r, SUBLANES), cols] = ni
                return nr, ni

            hr, hi = lax.fori_loop(0, tb, step, (hre_s[r0:r0 + SUBLANES, cols], him_s[r0:r0 + SUBLANES, cols]),
                                   unroll=min(tb, SCAN_UNROLL))
            hre_s[r0:r0 + SUBLANES, cols] = hr
            him_s[r0:r0 + SUBLANES, cols] = hi

    parts = []
    for k in range(n_slab):
        hr = xre[:, k * slab_st:(k + 1) * slab_st].astype(BF16)
        hi = xim[:, k * slab_st:(k + 1) * slab_st].astype(BF16)
        lanes = slice(k * slab_in, (k + 1) * slab_in)
        parts.append(_dot(hr, cre_ref[k]) - _dot(hi, cim_ref[k]) + d_ref[:, lanes] * ut[k])
    z = jax.nn.gelu(jnp.concatenate(parts, axis=-1))
    out = z * jax.nn.sigmoid(_dot(z.astype(BF16), gw_ref[...]) + gb_ref[...])
    for k in range(n_slab):
        lanes = slice(k * slab_in, (k + 1) * slab_in)
        if n_grp == 1:
            y_ref[0, :, lanes] = out[:, lanes].astype(y_ref.dtype)
        else:
            yt[k] = out[:, lanes]
            for b in range(n_grp):
                y_ref[b, :, lanes] = yt[k, pl.ds(b, tb, stride=n_grp), :].astype(y_ref.dtype)
    hre_out[...] = hre_s[...]
    him_out[...] = him_s[...]


def _s5_call(u, h0_re, h0_im, prm, *, nb):
    a_re, a_im, bb_re, bb_im, c_re, c_im, d_skip, glu_w, glu_b = prm
    n_grp, r, d_a = u.shape
    assert n_grp == 1 or nb == 1
    t = r // nb
    n_seq = n_grp * nb
    n_state = a_re.shape[1]
    tb = max(1, min(t, ROW_TILE // n_seq))
    rows = tb * n_seq
    consts = [a_re, a_im, bb_re, bb_im, c_re, c_im, d_skip, glu_w, glu_b]
    state_spec = pl.BlockSpec((n_seq, n_state), lambda i: (0, 0))
    blk = pl.BlockSpec((n_grp, tb * nb, d_a), lambda i: (0, i, 0))
    return pl.pallas_call(
        functools.partial(_s5_kernel, n_grp=n_grp, nb=nb, tb=tb),
        grid=(t // tb,),
        in_specs=[blk, state_spec, state_spec] + [_const_spec(c.shape) for c in consts],
        out_specs=[blk, state_spec, state_spec],
        out_shape=[jax.ShapeDtypeStruct(u.shape, BF16),
                   jax.ShapeDtypeStruct((n_seq, n_state), F32),
                   jax.ShapeDtypeStruct((n_seq, n_state), F32)],
        scratch_shapes=[pltpu.VMEM((d_a // LANES, rows, LANES), F32), pltpu.VMEM((d_a // LANES, rows, LANES), F32),
                        pltpu.VMEM((rows, n_state), F32), pltpu.VMEM((rows, n_state), F32),
                        pltpu.VMEM((n_seq, n_state), F32), pltpu.VMEM((n_seq, n_state), F32)],
        compiler_params=_params("arbitrary"),
        name="s5_mixer",
    )(u, h0_re, h0_im, *consts)


def _sgu_chunks_kernel(u_ref, v_ref, w_ref, bias_ref, y_ref):
    n_heads, cl, _ = w_ref.shape
    hd = v_ref.shape[1] // n_heads
    row = lax.broadcasted_iota(jnp.int32, (cl, cl), 0)
    col = lax.broadcasted_iota(jnp.int32, (cl, cl), 1)
    w = [jnp.where(col <= row, w_ref[h], 0.0).astype(BF16) for h in range(n_heads)]
    heads_per_slab = LANES // hd
    lane = lax.broadcasted_iota(jnp.int32, (cl, LANES), 1)
    for c in range(v_ref.shape[0] // cl):
        r = slice(c * cl, (c + 1) * cl)
        slabs = []
        for s in range(n_heads // heads_per_slab):
            v = v_ref[r, s * LANES:(s + 1) * LANES].astype(BF16)
            mixed = _dot(w[s * heads_per_slab], v)
            for j in range(1, heads_per_slab):
                mixed = jnp.where(lane >= j * hd, _dot(w[s * heads_per_slab + j], v), mixed)
            slabs.append(mixed)
        mixed = jnp.concatenate(slabs, axis=-1) + bias_ref[...]
        y_ref[r, :] = (u_ref[r, :] * mixed).astype(y_ref.dtype)


def _sgu_short_kernel(u_ref, v_ref, wl_ref, bias_ref, y_ref, *, nb, t):
    for i in range(t):
        mixed = bias_ref[i:i + 1, :]
        for j in range(i + 1):
            mixed = mixed + wl_ref[i, j:j + 1, :] * v_ref[0, j * nb:(j + 1) * nb, :]
        y_ref[0, i * nb:(i + 1) * nb, :] = (u_ref[0, i * nb:(i + 1) * nb, :] * mixed).astype(y_ref.dtype)


def _sgu_call(u, vn, w_s, b_s, *, nb):
    n_grp, r, d_b = u.shape
    t = r // nb
    n_heads = w_s.shape[0]
    hd = d_b // n_heads
    cl = min(t, SGU_CHUNK)
    bias = jnp.repeat(b_s[:, :cl].T, hd, axis=1)
    if nb == 1 and t % SGU_CHUNK == 0:
        tile = min(ROW_TILE, r)
        return pl.pallas_call(
            _sgu_chunks_kernel,
            grid=(n_grp, r // tile),
            in_specs=[_row_spec(tile, d_b), _row_spec(tile, d_b), _const_spec(w_s.shape), _const_spec(bias.shape)],
            out_specs=_row_spec(tile, d_b),
            out_shape=jax.ShapeDtypeStruct(u.shape, BF16),
            compiler_params=_params("arbitrary", "arbitrary"),
            name="sgu_chunks",
        )(u, vn, w_s, bias)
    assert n_grp == 1 and t < SGU_CHUNK
    wl = jnp.repeat(jnp.transpose(w_s[:, :cl, :cl], (1, 2, 0)), hd, axis=2)
    return pl.pallas_call(
        functools.partial(_sgu_short_kernel, nb=nb, t=t),
        out_shape=jax.ShapeDtypeStruct(u.shape, BF16),
        name="sgu_short",
    )(u, vn, wl, bias)


def _interleave(a):
    return jnp.swapaxes(a, 0, 1).reshape(1, a.shape[0] * a.shape[1], a.shape[2])


def _deinterleave(a, nb):
    return jnp.swapaxes(a.reshape(-1, nb, a.shape[2]), 0, 1)


def _run_trunk(x, s5_re, s5_im, pool_buf, conv_buf, prefix_valid, W, *, interleaved):
    batch, t, d = x.shape
    depth = W["norm_mix"].shape[0]
    nb = batch if interleaved else 1
    pack = _interleave if interleaved else (lambda a: a)
    unpack = (lambda a: _deinterleave(a, nb)) if interleaved else (lambda a: a)
    common = dict(nb=nb, prefix_valid=prefix_valid)

    def kind(layer):
        return "final" if layer == depth else ("even" if layer % 2 == 0 else "odd")

    def hist(layer):
        return (pack(pool_buf[layer // 2]), pack(conv_buf[layer // 2])) if kind(layer) == "odd" else None

    h = pack(x)
    mixer_in = _trunk_call(h, 0, W, mix=None, nxt=kind(0), odd_hist=hist(0), **common)
    new_re, new_im, new_v, new_pool, new_conv = [], [], [], [], []
    for layer in range(depth):
        i = layer // 2
        if layer % 2 == 0:
            u_a, u_b, vn = mixer_in
            y_a, hre, him = _s5_call(u_a, s5_re[i].reshape(batch, -1), s5_im[i].reshape(batch, -1), W["s5"][i],
                                     nb=nb)
            y_b = _sgu_call(u_b, vn, W["sgu_w"][i], W["sgu_b"][i], nb=nb)
            new_re.append(hre.reshape(s5_re[i].shape))
            new_im.append(him.reshape(s5_im[i].shape))
            new_v.append(unpack(vn))
            mix = (y_a, y_b, W["w_out_even"], i)
        else:
            y_c, y_d, pout, cout = mixer_in
            new_pool.append(unpack(pout))
            new_conv.append(unpack(cout))
            mix = (y_c, y_d, W["w_out_odd"], i)
        outs = _trunk_call(h, layer + 1, W, mix=mix, nxt=kind(layer + 1), odd_hist=hist(layer + 1), **common)
        h, mixer_in = outs[0], outs[1:]
    return (unpack(h), jnp.stack(new_re), jnp.stack(new_im), jnp.stack(new_v), jnp.stack(new_pool),
            jnp.stack(new_conv))


def kernel(x_prompt, x_sample, state_s5_re, state_s5_im, state_pool, state_conv, norm_mix, norm_ffn, norm_final, w_in_even, w_out_even, s5_lambda_re, s5_lambda_im, s5_log_dt, s5_b_re, s5_b_im, s5_c_re, s5_c_im, s5_d, s5_glu_w, s5_glu_b, sgu_norm, sgu_w, sgu_b, w_in_odd, w_out_odd, pool_w, pool_scale, conv_w, conv_b, ffn_w_gate, ffn_w_up, ffn_w_down):
    n_even, n_odd = w_in_even.shape[0], w_in_odd.shape[0]
    depth = norm_mix.shape[0]
    s5 = []
    for i in range(n_even):
        a_re, a_im, bb_re, bb_im = _s5_prep(s5_lambda_re[i], s5_lambda_im[i], s5_log_dt[i], s5_b_re[i], s5_b_im[i])
        s5.append((a_re, a_im, bb_re, bb_im, _blockdiag_out(s5_c_re[i]), _blockdiag_out(s5_c_im[i]),
                   s5_d[i].reshape(1, -1), s5_glu_w[i].astype(BF16), s5_glu_b[i].reshape(1, -1)))
    d = norm_mix.shape[1]
    W = dict(
        norm_mix=norm_mix.reshape(depth, 1, d), norm_ffn=norm_ffn.reshape(depth, 1, d),
        norm_final=norm_final.reshape(1, d),
        w_in_even=w_in_even.astype(BF16), w_out_even=w_out_even.astype(BF16),
        w_in_odd=w_in_odd.astype(BF16), w_out_odd=w_out_odd.astype(BF16),
        sgu_norm=sgu_norm.reshape(n_even, 1, -1), sgu_w=sgu_w, sgu_b=sgu_b, s5=s5,
        pool_w=pool_w.astype(BF16), pool_scale=pool_scale.reshape(n_odd, 1, -1), conv_w=conv_w,
        conv_b=conv_b.reshape(n_odd, 1, -1),
        ffn_gate=ffn_w_gate.astype(BF16), ffn_up=ffn_w_up.astype(BF16), ffn_down=ffn_w_down.astype(BF16),
    )
    bp = x_prompt.shape[0]
    z_s5 = jnp.zeros((n_even, bp) + state_s5_re.shape[2:], state_s5_re.dtype)
    z_pool = jnp.zeros((n_odd, bp) + state_pool.shape[2:], x_prompt.dtype)
    z_conv = jnp.zeros((n_odd, bp) + state_conv.shape[2:], x_prompt.dtype)
    y_p, p_re, p_im, _, p_pool, p_conv = _run_trunk(x_prompt, z_s5, z_s5, z_pool, z_conv, False, W,
                                                    interleaved=False)
    y_s, s_re, s_im, s_v, s_pool, s_conv = _run_trunk(
        x_sample, state_s5_re, state_s5_im, state_pool, state_conv, True, W, interleaved=True)
    return (y_p, y_s, p_re, p_im, p_pool, p_conv, s_re, s_im, s_v, s_pool, s_conv)
```

```python
import functools

import jax
import jax.numpy as jnp
from jax import lax
from jax.experimental import pallas as pl
from jax.experimental.pallas import tpu as pltpu

F32 = jnp.float32
BF16 = jnp.bfloat16

EPS = 1e-6
SGU_CHUNK = 128
POOL_WINDOWS = (2, 4, 8, 16)
POOL_BUF = max(POOL_WINDOWS) - 1
CONV_W = 3

LANES = 128
SUBLANES = 8
ROW_TILE = 512
FF_CHUNK = 512
SCAN_PIECES = 4
VMEM_LIMIT = 56 * 1024 * 1024


def _dot(a, b):
    return jnp.dot(a, b, preferred_element_type=F32)


def _rms(x, g):
    return x * lax.rsqrt(jnp.mean(x * x, axis=-1, keepdims=True) + EPS) * g


def _round_up(n, m):
    return -(-n // m) * m


def _const_spec(shape):
    zeros = (0,) * len(shape)
    return pl.BlockSpec(shape, lambda *_: zeros, pipeline_mode=pl.Buffered(1))


def _layer_spec(stack, layer):
    zeros = (0,) * (stack.ndim - 1)
    return pl.BlockSpec((None,) + stack.shape[1:], lambda *_: (layer,) + zeros, pipeline_mode=pl.Buffered(1))


def _row_spec(tile, c):
    return pl.BlockSpec((None, tile, c), lambda g, i: (g, i, 0))


def _params(*semantics):
    return pltpu.CompilerParams(dimension_semantics=semantics, vmem_limit_bytes=VMEM_LIMIT)


def _poolconv_pieces(xc, z, bg, i, pbuf_ref, cbuf_ref, pw_ref, ps_ref, cw_ref, cb_ref,
                     yc_ref, yd_ref, pout_ref, cout_ref, fullc, fullz, diff, *, nb, prefix_valid):
    hp, hc = POOL_BUF * nb, (CONV_W - 1) * nb
    p0, c0 = pbuf_ref.shape[0], cbuf_ref.shape[0]
    r = xc.shape[0]
    cg = xc.shape[1] // len(POOL_WINDOWS)
    first = i == 0

    def load_tile():
        fullc[0:p0, :] = jnp.where(first, pbuf_ref[...], fullc[0:p0, :])
        fullz[0:c0, :] = jnp.where(first, cbuf_ref[...], fullz[0:c0, :])
        fullc[p0:p0 + r, :] = xc[...]
        fullz[c0:c0 + r, :] = z[...]

    def conv():
        acc = cb_ref[...]
        for k in range(CONV_W):
            back = (CONV_W - 1 - k) * nb
            acc = acc + fullz[c0 - back:c0 - back + r, :] * cw_ref[k:k + 1, :]
        yd_ref[...] = (bg[...] * acc).astype(yd_ref.dtype)

    def window_sum(rows, lanes, w):
        s = fullc[p0:p0 + rows, lanes]
        for k in range(1, w):
            s = s + fullc[p0 - k * nb:p0 - k * nb + rows, lanes]
        return s

    def pool(gi, w):
        lanes = slice(gi * cg, (gi + 1) * cg)
        diff[:, lanes] = window_sum(r, lanes, w) * (1.0 / w) - fullc[p0:p0 + r, lanes]
        if not prefix_valid:
            head = min(p0, r)
            t = lax.broadcasted_iota(jnp.int32, (head, 1), 0) // nb
            n = jnp.minimum(t + 1, w).astype(F32)
            short = window_sum(head, lanes, w) / n - fullc[p0:p0 + head, lanes]
            diff[0:head, lanes] = jnp.where(first, short, diff[0:head, lanes])

    def project():
        for gi in range(len(POOL_WINDOWS)):
            lanes = slice(gi * cg, (gi + 1) * cg)
            yc = _dot(diff[:, lanes].astype(BF16), pw_ref[gi]) * ps_ref[:, lanes]
            yc_ref[:, lanes] = yc.astype(yc_ref.dtype)

    def shift_history():
        pout_ref[...] = fullc[p0 + r - hp:p0 + r, :]
        cout_ref[...] = fullz[c0 + r - hc:c0 + r, :]
        new_pool = fullc[r:r + p0, :]
        new_conv = fullz[r:r + c0, :]
        fullc[0:p0, :] = new_pool
        fullz[0:c0, :] = new_conv

    pools = [functools.partial(pool, gi, w) for gi, w in enumerate(POOL_WINDOWS)]
    return [load_tile, conv] + pools + [project, shift_history]


def _trunk_kernel(*refs, has_mix, nxt, d_half, nb, prefix_valid, tiles_per_group):
    refs = list(refs)
    h_ref = refs.pop(0)
    if has_mix:
        ya_ref, yb_ref, wo_ref, gf_ref, wg_ref, wu_ref, wd_ref = refs[:7]
        del refs[:7]
    gn_ref = refs.pop(0)
    if nxt != "final":
        win_ref = refs.pop(0)
    if nxt == "even":
        gs_ref = refs.pop(0)
    if nxt == "odd":
        odd_in = refs[:6]
        del refs[:6]
    outs = refs
    s = pl.program_id(0)
    if nxt == "odd":
        odd_out = outs[-7:]

        @pl.when(s == 0)
        def _():
            for ref in odd_out[-3:]:
                ref[...] = jnp.zeros(ref.shape, ref.dtype)

    h = h_ref[...]
    if has_mix:
        h = h + _dot(ya_ref[...], wo_ref[:d_half, :]) + _dot(yb_ref[...], wo_ref[d_half:, :])
        hn = _rms(h, gf_ref[...]).astype(BF16)
        d_ff = wg_ref.shape[1]
        acc = None
        for c0 in range(0, d_ff, FF_CHUNK):
            c1 = min(c0 + FF_CHUNK, d_ff)
            g = _dot(hn, wg_ref[:, c0:c1])
            u = _dot(hn, wu_ref[:, c0:c1])
            a = (g * jax.nn.sigmoid(g) * u).astype(BF16)
            part = _dot(a, wd_ref[c0:c1, :])
            acc = part if acc is None else acc + part
        h = h + acc
        if nxt != "final":
            outs.pop(0)[...] = h
    hn = _rms(h, gn_ref[...])
    if nxt == "final":
        outs[0][...] = hn
        return
    proj = _dot(hn.astype(BF16), win_ref[...])
    if nxt == "even":
        ua_ref, ub_ref, vn_ref = outs
        ua_ref[...] = proj[:, :d_half]
        ub_ref[...] = proj[:, d_half:2 * d_half]
        vn_ref[...] = _rms(proj[:, 2 * d_half:], gs_ref[...])
    else:
        xc = proj[:, :d_half]
        z = proj[:, 3 * d_half:] * proj[:, d_half:2 * d_half]
        bg = proj[:, 2 * d_half:3 * d_half]
        for piece in _poolconv_pieces(xc, z, bg, s % tiles_per_group, *odd_in, *odd_out, nb=nb,
                                      prefix_valid=prefix_valid):
            piece()


def _trunk_call(h, layer, W, *, mix, nxt, nb, prefix_valid, odd_hist=None):
    n_grp, r, d = h.shape
    d_half = d // 2
    tile = min(ROW_TILE, r)
    i = layer // 2
    tpg = r // tile
    n_tiles = n_grp * tpg

    def rows(c):
        return pl.BlockSpec((None, tile, c), lambda s: (s // tpg, s % tpg, 0))

    def hist_spec(n_rows):
        return pl.BlockSpec((None, n_rows, d_half), lambda s: (s // tpg, 0, 0))

    args, specs = [h], [rows(d)]
    if mix is not None:
        ya, yb, w_out, wi = mix
        ffn = [W["norm_ffn"], W["ffn_gate"], W["ffn_up"], W["ffn_down"]]
        args += [ya, yb, w_out] + ffn
        specs += [rows(d_half), rows(d_half), _layer_spec(w_out, wi)]
        specs += [_layer_spec(w, layer - 1) for w in ffn]
    if nxt == "final":
        args.append(W["norm_final"])
        specs.append(_const_spec(W["norm_final"].shape))
    else:
        w_in = W["w_in_even"] if nxt == "even" else W["w_in_odd"]
        args += [W["norm_mix"], w_in]
        specs += [_layer_spec(W["norm_mix"], layer), _layer_spec(w_in, i)]
    scratch = []
    if nxt == "even":
        args.append(W["sgu_norm"])
        specs.append(_layer_spec(W["sgu_norm"], i))
        out_shapes = [jax.ShapeDtypeStruct((n_grp, r, d_half), F32)] * 3
        out_specs = [rows(d_half)] * 3
    elif nxt == "odd":
        pool_buf, conv_buf = odd_hist
        hp, hc = POOL_BUF * nb, (CONV_W - 1) * nb
        p0, c0 = _round_up(hp, SUBLANES), _round_up(hc, SUBLANES)
        pool_buf = jnp.pad(pool_buf, ((0, 0), (p0 - hp, 0), (0, 0)))
        conv_buf = jnp.pad(conv_buf, ((0, 0), (c0 - hc, 0), (0, 0)))
        odd_w = [W["pool_w"], W["pool_scale"], W["conv_w"], W["conv_b"]]
        args += [pool_buf, conv_buf] + odd_w
        specs += [hist_spec(p0), hist_spec(c0)] + [_layer_spec(w, i) for w in odd_w]
        scratch = [pltpu.VMEM((p0 + tile, d_half), F32), pltpu.VMEM((c0 + tile, d_half), F32),
                   pltpu.VMEM((tile, d_half), F32)]
        out_shapes = [jax.ShapeDtypeStruct((n_grp, r, d_half), BF16)] * 2 + [
            jax.ShapeDtypeStruct((n_grp, hp, d_half), F32), jax.ShapeDtypeStruct((n_grp, hc, d_half), F32)]
        out_specs = [rows(d_half)] * 2 + [hist_spec(hp), hist_spec(hc)]
    else:
        out_shapes, out_specs = [], []
    if mix is not None or nxt == "final":
        out_shapes = [jax.ShapeDtypeStruct((n_grp, r, d), F32)] + out_shapes
        out_specs = [rows(d)] + out_specs

    return pl.pallas_call(
        functools.partial(_trunk_kernel, has_mix=mix is not None, nxt=nxt, d_half=d_half, nb=nb,
                          prefix_valid=prefix_valid, tiles_per_group=tpg),
        grid=(n_tiles,),
        in_specs=specs,
        out_specs=out_specs,
        out_shape=out_shapes,
        scratch_shapes=scratch,
        compiler_params=_params("arbitrary"),
        name=f"trunk_{'mix' if mix is not None else 'in'}_{nxt}",
    )(*args)


def _s5_prep_kernel(lr_ref, li_ref, ldt_ref, bre_ref, bim_ref, are_ref, aim_ref, bbre_ref, bbim_ref):
    lr = lr_ref[...]
    li = li_ref[...]
    dt = jnp.exp(ldt_ref[...])
    mag = jnp.exp(lr * dt)
    ang = li * dt
    ab_re = mag * jnp.cos(ang)
    ab_im = mag * jnp.sin(ang)
    den = lr * lr + li * li
    f_re = ((ab_re - 1.0) * lr + ab_im * li) / den
    f_im = (ab_im * lr - (ab_re - 1.0) * li) / den
    are_ref[...] = ab_re
    aim_ref[...] = ab_im
    for g in range(lr.shape[0]):
        fr = f_re[g:g + 1, :]
        fi = f_im[g:g + 1, :]
        br = bre_ref[g]
        bi = bim_ref[g]
        bbre_ref[g] = fr * br - fi * bi
        bbim_ref[g] = fr * bi + fi * br


def _s5_prep(lam_re, lam_im, log_dt, b_re, b_im):
    g, n = lam_re.shape
    p = b_re.shape[-1]
    bt_re = jnp.swapaxes(b_re, 1, 2)
    bt_im = jnp.swapaxes(b_im, 1, 2)
    a_re, a_im, bb_re, bb_im = pl.pallas_call(
        _s5_prep_kernel,
        out_shape=[jax.ShapeDtypeStruct((g, n), F32)] * 2 + [jax.ShapeDtypeStruct((g, p, n), F32)] * 2,
        name="s5_discretize",
    )(lam_re, lam_im, log_dt.reshape(g, 1), bt_re, bt_im)
    gs = LANES // p
    eye = jnp.eye(gs, dtype=F32)

    def blockdiag_in(bb):
        return jnp.einsum("kgpn,gh->kgphn", bb.reshape(g // gs, gs, p, n), eye).reshape(
            g // gs, gs * p, gs * n).astype(BF16)

    def sublane_rows(a):
        return jnp.broadcast_to(a.reshape(1, g * n), (SUBLANES, g * n))

    return sublane_rows(a_re), sublane_rows(a_im), blockdiag_in(bb_re), blockdiag_in(bb_im)


def _blockdiag_out(c):
    g, p, n = c.shape
    gs = LANES // p
    eye = jnp.eye(gs, dtype=F32)
    return jnp.einsum("kgpn,gh->kgnhp", c.reshape(g // gs, gs, p, n), eye).reshape(
        g // gs, gs * n, gs * p).astype(BF16)


def _s5_kernel(u_ref, h0re_ref, h0im_ref, are_ref, aim_ref, bbre_ref, bbim_ref, cre_ref, cim_ref,
               d_ref, gw_ref, gb_ref, y_ref, hre_out, him_out, ut, yt, xre, xim, hre_s, him_s,
               *, n_grp, nb, tb):
    i = pl.program_id(0)
    n_seq = n_grp * nb

    @pl.when(i == 0)
    def _():
        hre_s[...] = h0re_ref[...]
        him_s[...] = h0im_ref[...]

    n_slab, slab_in, slab_st = bbre_ref.shape
    for k in range(n_slab):
        lanes = slice(k * slab_in, (k + 1) * slab_in)
        if n_grp == 1:
            ut[k] = u_ref[0, :, lanes]
        else:
            for b in range(n_grp):
                ut[k, pl.ds(b, tb, stride=n_grp), :] = u_ref[b, :, lanes]
    def project_in(k, x, bb_ref):
        x[:, k * slab_st:(k + 1) * slab_st] = _dot(ut[k].astype(BF16), bb_ref[k])

    def read_out(k, x, c_ref):
        return _dot(x[:, k * slab_st:(k + 1) * slab_st].astype(BF16), c_ref[k])

    def combine(k, y_re, y_im):
        return y_re - y_im + d_ref[:, k * slab_in:(k + 1) * slab_in] * ut[k]

    def scan_pieces(cols):
        items = [(r0, t) for r0 in range(0, n_seq, SUBLANES) for t in range(tb)]
        per = len(items) // SCAN_PIECES

        def run(chunk):
            hr = hi = cur = None
            for r0, t in chunk:
                if r0 != cur:
                    if cur is not None:
                        hre_s[cur:cur + SUBLANES, cols] = hr
                        him_s[cur:cur + SUBLANES, cols] = hi
                    hr, hi, cur = hre_s[r0:r0 + SUBLANES, cols], him_s[r0:r0 + SUBLANES, cols], r0
                ar = are_ref[:, cols]
                ai = aim_ref[:, cols]
                r = t * n_seq + r0
                nr = ar * hr - ai * hi + xre[r:r + SUBLANES, cols]
                ni = ar * hi + ai * hr + xim[r:r + SUBLANES, cols]
                xre[r:r + SUBLANES, cols] = nr
                xim[r:r + SUBLANES, cols] = ni
                hr, hi = nr, ni
            hre_s[cur:cur + SUBLANES, cols] = hr
            him_s[cur:cur + SUBLANES, cols] = hi

        return [functools.partial(run, items[p * per:(p + 1) * per]) for p in range(SCAN_PIECES)]

    project_in(0, xre, bbre_ref)
    project_in(0, xim, bbim_ref)
    y_first = []
    for k in range(n_slab):
        pieces = scan_pieces(slice(k * slab_st, (k + 1) * slab_st))
        for p, piece in enumerate(pieces):
            x, bb, c = ((xre, bbre_ref, cre_ref), (xim, bbim_ref, cim_ref))[p // (SCAN_PIECES // 2)]
            if p % (SCAN_PIECES // 2) == 0:
                if k + 1 < n_slab:
                    project_in(k + 1, x, bb)
                else:
                    y_first.append(read_out(0, x, c))
            piece()
    parts = [combine(0, *y_first)]
    parts += [combine(k, read_out(k, xre, cre_ref), read_out(k, xim, cim_ref)) for k in range(1, n_slab)]
    z = jax.nn.gelu(jnp.concatenate(parts, axis=-1))
    out = z * jax.nn.sigmoid(_dot(z.astype(BF16), gw_ref[...]) + gb_ref[...])
    for k in range(n_slab):
        lanes = slice(k * slab_in, (k + 1) * slab_in)
        if n_grp == 1:
            y_ref[0, :, lanes] = out[:, lanes].astype(y_ref.dtype)
        else:
            yt[k] = out[:, lanes]
            for b in range(n_grp):
                y_ref[b, :, lanes] = yt[k, pl.ds(b, tb, stride=n_grp), :].astype(y_ref.dtype)
    hre_out[...] = hre_s[...]
    him_out[...] = him_s[...]


def _s5_call(u, h0_re, h0_im, prm, *, nb):
    a_re, a_im, bb_re, bb_im, c_re, c_im, d_skip, glu_w, glu_b = prm
    n_grp, r, d_a = u.shape
    assert n_grp == 1 or nb == 1
    t = r // nb
    n_seq = n_grp * nb
    n_state = a_re.shape[1]
    tb = max(1, min(t, ROW_TILE // n_seq))
    rows = tb * n_seq
    consts = [a_re, a_im, bb_re, bb_im, c_re, c_im, d_skip, glu_w, glu_b]
    state_spec = pl.BlockSpec((n_seq, n_state), lambda i: (0, 0))
    blk = pl.BlockSpec((n_grp, tb * nb, d_a), lambda i: (0, i, 0))
    return pl.pallas_call(
        functools.partial(_s5_kernel, n_grp=n_grp, nb=nb, tb=tb),
        grid=(t // tb,),
        in_specs=[blk, state_spec, state_spec] + [_const_spec(c.shape) for c in consts],
        out_specs=[blk, state_spec, state_spec],
        out_shape=[jax.ShapeDtypeStruct(u.shape, BF16),
                   jax.ShapeDtypeStruct((n_seq, n_state), F32),
                   jax.ShapeDtypeStruct((n_seq, n_state), F32)],
        scratch_shapes=[pltpu.VMEM((d_a // LANES, rows, LANES), F32), pltpu.VMEM((d_a // LANES, rows, LANES), F32),
                        pltpu.VMEM((rows, n_state), F32), pltpu.VMEM((rows, n_state), F32),
                        pltpu.VMEM((n_seq, n_state), F32), pltpu.VMEM((n_seq, n_state), F32)],
        compiler_params=_params("arbitrary"),
        name="s5_mixer",
    )(u, h0_re, h0_im, *consts)


def _sgu_chunks_kernel(u_ref, v_ref, w_ref, bias_ref, y_ref):
    n_heads, cl, _ = w_ref.shape
    hd = v_ref.shape[1] // n_heads
    row = lax.broadcasted_iota(jnp.int32, (cl, cl), 0)
    col = lax.broadcasted_iota(jnp.int32, (cl, cl), 1)
    w = [jnp.where(col <= row, w_ref[h], 0.0).astype(BF16) for h in range(n_heads)]
    heads_per_slab = LANES // hd
    lane = lax.broadcasted_iota(jnp.int32, (cl, LANES), 1)
    for c in range(v_ref.shape[0] // cl):
        r = slice(c * cl, (c + 1) * cl)
        slabs = []
        for s in range(n_heads // heads_per_slab):
            v = v_ref[r, s * LANES:(s + 1) * LANES].astype(BF16)
            mixed = _dot(w[s * heads_per_slab], v)
            for j in range(1, heads_per_slab):
                mixed = jnp.where(lane >= j * hd, _dot(w[s * heads_per_slab + j], v), mixed)
            slabs.append(mixed)
        mixed = jnp.concatenate(slabs, axis=-1) + bias_ref[...]
        y_ref[r, :] = (u_ref[r, :] * mixed).astype(y_ref.dtype)


def _sgu_short_kernel(u_ref, v_ref, wl_ref, bias_ref, y_ref, *, nb, t):
    for i in range(t):
        mixed = bias_ref[i:i + 1, :]
        for j in range(i + 1):
            mixed = mixed + wl_ref[i, j:j + 1, :] * v_ref[0, j * nb:(j + 1) * nb, :]
        y_ref[0, i * nb:(i + 1) * nb, :] = (u_ref[0, i * nb:(i + 1) * nb, :] * mixed).astype(y_ref.dtype)


def _sgu_call(u, vn, w_s, b_s, *, nb):
    n_grp, r, d_b = u.shape
    t = r // nb
    n_heads = w_s.shape[0]
    hd = d_b // n_heads
    cl = min(t, SGU_CHUNK)
    bias = jnp.repeat(b_s[:, :cl].T, hd, axis=1)
    if nb == 1 and t % SGU_CHUNK == 0:
        tile = min(ROW_TILE, r)
        return pl.pallas_call(
            _sgu_chunks_kernel,
            grid=(n_grp, r // tile),
            in_specs=[_row_spec(tile, d_b), _row_spec(tile, d_b), _const_spec(w_s.shape), _const_spec(bias.shape)],
            out_specs=_row_spec(tile, d_b),
            out_shape=jax.ShapeDtypeStruct(u.shape, BF16),
            compiler_params=_params("arbitrary", "arbitrary"),
            name="sgu_chunks",
        )(u, vn, w_s, bias)
    assert n_grp == 1 and t < SGU_CHUNK
    wl = jnp.repeat(jnp.transpose(w_s[:, :cl, :cl], (1, 2, 0)), hd, axis=2)
    return pl.pallas_call(
        functools.partial(_sgu_short_kernel, nb=nb, t=t),
        out_shape=jax.ShapeDtypeStruct(u.shape, BF16),
        name="sgu_short",
    )(u, vn, wl, bias)


def _interleave(a):
    return jnp.swapaxes(a, 0, 1).reshape(1, a.shape[0] * a.shape[1], a.shape[2])


def _deinterleave(a, nb):
    return jnp.swapaxes(a.reshape(-1, nb, a.shape[2]), 0, 1)


def _run_trunk(x, s5_re, s5_im, pool_buf, conv_buf, prefix_valid, W, *, interleaved):
    batch, t, d = x.shape
    depth = W["norm_mix"].shape[0]
    nb = batch if interleaved else 1
    pack = _interleave if interleaved else (lambda a: a)
    unpack = (lambda a: _deinterleave(a, nb)) if interleaved else (lambda a: a)
    common = dict(nb=nb, prefix_valid=prefix_valid)

    def kind(layer):
        return "final" if layer == depth else ("even" if layer % 2 == 0 else "odd")

    def hist(layer):
        return (pack(pool_buf[layer // 2]), pack(conv_buf[layer // 2])) if kind(layer) == "odd" else None

    h = pack(x)
    mixer_in = _trunk_call(h, 0, W, mix=None, nxt=kind(0), odd_hist=hist(0), **common)
    new_re, new_im, new_v, new_pool, new_conv = [], [], [], [], []
    for layer in range(depth):
        i = layer // 2
        if layer % 2 == 0:
            u_a, u_b, vn = mixer_in
            y_a, hre, him = _s5_call(u_a, s5_re[i].reshape(batch, -1), s5_im[i].reshape(batch, -1), W["s5"][i],
                                     nb=nb)
            y_b = _sgu_call(u_b, vn, W["sgu_w"][i], W["sgu_b"][i], nb=nb)
            new_re.append(hre.reshape(s5_re[i].shape))
            new_im.append(him.reshape(s5_im[i].shape))
            new_v.append(unpack(vn))
            mix = (y_a, y_b, W["w_out_even"], i)
        else:
            y_c, y_d, pout, cout = mixer_in
            new_pool.append(unpack(pout))
            new_conv.append(unpack(cout))
            mix = (y_c, y_d, W["w_out_odd"], i)
        outs = _trunk_call(h, layer + 1, W, mix=mix, nxt=kind(layer + 1), odd_hist=hist(layer + 1), **common)
        h, mixer_in = outs[0], outs[1:]
    return (unpack(h), jnp.stack(new_re), jnp.stack(new_im), jnp.stack(new_v), jnp.stack(new_pool),
            jnp.stack(new_conv))


def kernel(x_prompt, x_sample, state_s5_re, state_s5_im, state_pool, state_conv, norm_mix, norm_ffn, norm_final, w_in_even, w_out_even, s5_lambda_re, s5_lambda_im, s5_log_dt, s5_b_re, s5_b_im, s5_c_re, s5_c_im, s5_d, s5_glu_w, s5_glu_b, sgu_norm, sgu_w, sgu_b, w_in_odd, w_out_odd, pool_w, pool_scale, conv_w, conv_b, ffn_w_gate, ffn_w_up, ffn_w_down):
    n_even, n_odd = w_in_even.shape[0], w_in_odd.shape[0]
    depth = norm_mix.shape[0]
    s5 = []
    for i in range(n_even):
        a_re, a_im, bb_re, bb_im = _s5_prep(s5_lambda_re[i], s5_lambda_im[i], s5_log_dt[i], s5_b_re[i], s5_b_im[i])
        s5.append((a_re, a_im, bb_re, bb_im, _blockdiag_out(s5_c_re[i]), _blockdiag_out(s5_c_im[i]),
                   s5_d[i].reshape(1, -1), s5_glu_w[i].astype(BF16), s5_glu_b[i].reshape(1, -1)))
    d = norm_mix.shape[1]
    W = dict(
        norm_mix=norm_mix.reshape(depth, 1, d), norm_ffn=norm_ffn.reshape(depth, 1, d),
        norm_final=norm_final.reshape(1, d),
        w_in_even=w_in_even.astype(BF16), w_out_even=w_out_even.astype(BF16),
        w_in_odd=w_in_odd.astype(BF16), w_out_odd=w_out_odd.astype(BF16),
        sgu_norm=sgu_norm.reshape(n_even, 1, -1), sgu_w=sgu_w, sgu_b=sgu_b, s5=s5,
        pool_w=pool_w.astype(BF16), pool_scale=pool_scale.reshape(n_odd, 1, -1), conv_w=conv_w,
        conv_b=conv_b.reshape(n_odd, 1, -1),
        ffn_gate=ffn_w_gate.astype(BF16), ffn_up=ffn_w_up.astype(BF16), ffn_down=ffn_w_down.astype(BF16),
    )
    bp = x_prompt.shape[0]
    z_s5 = jnp.zeros((n_even, bp) + state_s5_re.shape[2:], state_s5_re.dtype)
    z_pool = jnp.zeros((n_odd, bp) + state_pool.shape[2:], x_prompt.dtype)
    z_conv = jnp.zeros((n_odd, bp) + state_conv.shape[2:], x_prompt.dtype)
    y_p, p_re, p_im, _, p_pool, p_conv = _run_trunk(x_prompt, z_s5, z_s5, z_pool, z_conv, False, W,
                                                    interleaved=False)
    y_s, s_re, s_im, s_v, s_pool, s_conv = _run_trunk(
        x_sample, state_s5_re, state_s5_im, state_pool, state_conv, True, W, interleaved=True)
    return (y_p, y_s, p_re, p_im, p_pool, p_conv, s_re, s_im, s_v, s_pool, s_conv)
```

```python
import functools

import jax
import jax.numpy as jnp
from jax import lax
from jax.experimental import pallas as pl
from jax.experimental.pallas import tpu as pltpu

F32 = jnp.float32
BF16 = jnp.bfloat16

EPS = 1e-6
SGU_CHUNK = 128
POOL_WINDOWS = (2, 4, 8, 16)
POOL_BUF = max(POOL_WINDOWS) - 1
CONV_W = 3

LANES = 128
SUBLANES = 8
ROW_TILE = 512
FF_CHUNK = 512
SCAN_PIECES = 4
VMEM_LIMIT = 56 * 1024 * 1024


def _dot(a, b):
    return jnp.dot(a, b, preferred_element_type=F32)


def _rms(x, g):
    return x * lax.rsqrt(jnp.mean(x * x, axis=-1, keepdims=True) + EPS) * g


def _round_up(n, m):
    return -(-n // m) * m


def _const_spec(shape):
    zeros = (0,) * len(shape)
    return pl.BlockSpec(shape, lambda *_: zeros, pipeline_mode=pl.Buffered(1))


def _layer_spec(stack, layer):
    zeros = (0,) * (stack.ndim - 1)
    return pl.BlockSpec((None,) + stack.shape[1:], lambda *_: (layer,) + zeros, pipeline_mode=pl.Buffered(1))


def _params(*semantics):
    return pltpu.CompilerParams(dimension_semantics=semantics, vmem_limit_bytes=VMEM_LIMIT)


def _poolconv_pieces(xc, z, bg, i, pbuf_ref, cbuf_ref, pw_ref, ps_ref, cw_ref, cb_ref,
                     yc_ref, yd_ref, pout_ref, cout_ref, fullc, fullz, diff, *, nb, prefix_valid):
    hp, hc = POOL_BUF * nb, (CONV_W - 1) * nb
    p0, c0 = pbuf_ref.shape[0], cbuf_ref.shape[0]
    r = xc.shape[0]
    cg = xc.shape[1] // len(POOL_WINDOWS)
    first = i == 0

    def load_tile():
        fullc[0:p0, :] = jnp.where(first, pbuf_ref[...], fullc[0:p0, :])
        fullz[0:c0, :] = jnp.where(first, cbuf_ref[...], fullz[0:c0, :])
        fullc[p0:p0 + r, :] = xc[...]
        fullz[c0:c0 + r, :] = z[...]

    def conv():
        acc = cb_ref[...]
        for k in range(CONV_W):
            back = (CONV_W - 1 - k) * nb
            acc = acc + fullz[c0 - back:c0 - back + r, :] * cw_ref[k:k + 1, :]
        yd_ref[...] = (bg[...] * acc).astype(yd_ref.dtype)

    def window_sum(rows, lanes, w):
        s = fullc[p0:p0 + rows, lanes]
        for k in range(1, w):
            s = s + fullc[p0 - k * nb:p0 - k * nb + rows, lanes]
        return s

    def pool(gi, w):
        lanes = slice(gi * cg, (gi + 1) * cg)
        diff[:, lanes] = window_sum(r, lanes, w) * (1.0 / w) - fullc[p0:p0 + r, lanes]
        if not prefix_valid:
            head = min(p0, r)
            t = lax.broadcasted_iota(jnp.int32, (head, 1), 0) // nb
            n = jnp.minimum(t + 1, w).astype(F32)
            short = window_sum(head, lanes, w) / n - fullc[p0:p0 + head, lanes]
            diff[0:head, lanes] = jnp.where(first, short, diff[0:head, lanes])

    def project():
        for gi in range(len(POOL_WINDOWS)):
            lanes = slice(gi * cg, (gi + 1) * cg)
            yc = _dot(diff[:, lanes].astype(BF16), pw_ref[gi]) * ps_ref[:, lanes]
            yc_ref[:, lanes] = yc.astype(yc_ref.dtype)

    def shift_history():
        pout_ref[...] = fullc[p0 + r - hp:p0 + r, :]
        cout_ref[...] = fullz[c0 + r - hc:c0 + r, :]
        new_pool = fullc[r:r + p0, :]
        new_conv = fullz[r:r + c0, :]
        fullc[0:p0, :] = new_pool
        fullz[0:c0, :] = new_conv

    pools = [functools.partial(pool, gi, w) for gi, w in enumerate(POOL_WINDOWS)]
    return [load_tile, conv] + pools + [project, shift_history]


def _trunk_kernel(*refs, has_mix, nxt, d_half, nb, prefix_valid, tiles_per_group):
    refs = list(refs)
    h_ref = refs.pop(0)
    if has_mix:
        ya_ref, yb_ref, wo_ref, gf_ref, wg_ref, wu_ref, wd_ref = refs[:7]
        del refs[:7]
    gn_ref = refs.pop(0)
    if nxt != "final":
        win_ref = refs.pop(0)
    if nxt == "even":
        gs_ref, sw_ref, sb_ref = refs[:3]
        del refs[:3]
    if nxt == "odd":
        odd_in = refs[:6]
        del refs[:6]
    outs = refs
    s = pl.program_id(0)
    if nxt == "odd":
        odd_out = outs[-7:]

        @pl.when(s == 0)
        def _():
            for ref in odd_out[-3:]:
                ref[...] = jnp.zeros(ref.shape, ref.dtype)

    h = h_ref[...]
    if has_mix:
        h = h + _dot(ya_ref[...], wo_ref[:d_half, :]) + _dot(yb_ref[...], wo_ref[d_half:, :])
        hn = _rms(h, gf_ref[...]).astype(BF16)
        d_ff = wg_ref.shape[1]
        acc = None
        for c0 in range(0, d_ff, FF_CHUNK):
            c1 = min(c0 + FF_CHUNK, d_ff)
            g = _dot(hn, wg_ref[:, c0:c1])
            u = _dot(hn, wu_ref[:, c0:c1])
            a = (g * jax.nn.sigmoid(g) * u).astype(BF16)
            part = _dot(a, wd_ref[c0:c1, :])
            acc = part if acc is None else acc + part
        h = h + acc
        if nxt != "final":
            outs.pop(0)[...] = h
    hn = _rms(h, gn_ref[...])
    if nxt == "final":
        outs[0][...] = hn
        return
    proj = _dot(hn.astype(BF16), win_ref[...])
    if nxt == "even":
        ua_ref, yb_ref = outs[:2]
        ua_ref[...] = proj[:, :d_half]
        ub = proj[:, d_half:2 * d_half]
        vn = _rms(proj[:, 2 * d_half:], gs_ref[...])
        if nb == 1:
            _sgu_chunks(ub, vn, sw_ref, sb_ref, yb_ref)
        else:
            _sgu_short(ub, vn, sw_ref, sb_ref, yb_ref, nb=nb)
            outs[2][...] = vn
    else:
        xc = proj[:, :d_half]
        z = proj[:, 3 * d_half:] * proj[:, d_half:2 * d_half]
        bg = proj[:, 2 * d_half:3 * d_half]
        for piece in _poolconv_pieces(xc, z, bg, s % tiles_per_group, *odd_in, *odd_out, nb=nb,
                                      prefix_valid=prefix_valid):
            piece()


def _trunk_call(h, layer, W, *, mix, nxt, nb, prefix_valid, sgu_prm, odd_hist=None):
    n_grp, r, d = h.shape
    d_half = d // 2
    tile = min(ROW_TILE, r)
    i = layer // 2
    tpg = r // tile
    n_tiles = n_grp * tpg

    def rows(c):
        return pl.BlockSpec((None, tile, c), lambda s: (s // tpg, s % tpg, 0))

    def hist_spec(n_rows):
        return pl.BlockSpec((None, n_rows, d_half), lambda s: (s // tpg, 0, 0))

    args, specs = [h], [rows(d)]
    if mix is not None:
        ya, yb, w_out, wi = mix
        ffn = [W["norm_ffn"], W["ffn_gate"], W["ffn_up"], W["ffn_down"]]
        args += [ya, yb, w_out] + ffn
        specs += [rows(d_half), rows(d_half), _layer_spec(w_out, wi)]
        specs += [_layer_spec(w, layer - 1) for w in ffn]
    if nxt == "final":
        args.append(W["norm_final"])
        specs.append(_const_spec(W["norm_final"].shape))
    else:
        w_in = W["w_in_even"] if nxt == "even" else W["w_in_odd"]
        args += [W["norm_mix"], w_in]
        specs += [_layer_spec(W["norm_mix"], layer), _layer_spec(w_in, i)]
    scratch = []
    if nxt == "even":
        sgu = [W["sgu_norm"], *sgu_prm]
        args += sgu
        specs += [_layer_spec(w, i) for w in sgu]
        assert (nb == 1 and tile % SGU_CHUNK == 0) or (n_tiles == 1 and tile // nb < SGU_CHUNK)
        out_shapes = [jax.ShapeDtypeStruct((n_grp, r, d_half), F32), jax.ShapeDtypeStruct((n_grp, r, d_half), BF16)]
        if nb > 1:
            out_shapes.append(jax.ShapeDtypeStruct((n_grp, r, d_half), F32))
        out_specs = [rows(d_half)] * len(out_shapes)
    elif nxt == "odd":
        pool_buf, conv_buf = odd_hist
        hp, hc = POOL_BUF * nb, (CONV_W - 1) * nb
        p0, c0 = _round_up(hp, SUBLANES), _round_up(hc, SUBLANES)
        pool_buf = jnp.pad(pool_buf, ((0, 0), (p0 - hp, 0), (0, 0)))
        conv_buf = jnp.pad(conv_buf, ((0, 0), (c0 - hc, 0), (0, 0)))
        odd_w = [W["pool_w"], W["pool_scale"], W["conv_w"], W["conv_b"]]
        args += [pool_buf, conv_buf] + odd_w
        specs += [hist_spec(p0), hist_spec(c0)] + [_layer_spec(w, i) for w in odd_w]
        scratch = [pltpu.VMEM((p0 + tile, d_half), F32), pltpu.VMEM((c0 + tile, d_half), F32),
                   pltpu.VMEM((tile, d_half), F32)]
        out_shapes = [jax.ShapeDtypeStruct((n_grp, r, d_half), BF16)] * 2 + [
            jax.ShapeDtypeStruct((n_grp, hp, d_half), F32), jax.ShapeDtypeStruct((n_grp, hc, d_half), F32)]
        out_specs = [rows(d_half)] * 2 + [hist_spec(hp), hist_spec(hc)]
    else:
        out_shapes, out_specs = [], []
    if mix is not None or nxt == "final":
        out_shapes = [jax.ShapeDtypeStruct((n_grp, r, d), F32)] + out_shapes
        out_specs = [rows(d)] + out_specs

    return pl.pallas_call(
        functools.partial(_trunk_kernel, has_mix=mix is not None, nxt=nxt, d_half=d_half, nb=nb,
                          prefix_valid=prefix_valid, tiles_per_group=tpg),
        grid=(n_tiles,),
        in_specs=specs,
        out_specs=out_specs,
        out_shape=out_shapes,
        scratch_shapes=scratch,
        compiler_params=_params("arbitrary"),
        name=f"trunk_{'mix' if mix is not None else 'in'}_{nxt}",
    )(*args)


def _s5_prep_kernel(lr_ref, li_ref, ldt_ref, bre_ref, bim_ref, are_ref, aim_ref, bbre_ref, bbim_ref):
    lr = lr_ref[...]
    li = li_ref[...]
    dt = jnp.exp(ldt_ref[...])
    mag = jnp.exp(lr * dt)
    ang = li * dt
    ab_re = mag * jnp.cos(ang)
    ab_im = mag * jnp.sin(ang)
    den = lr * lr + li * li
    f_re = ((ab_re - 1.0) * lr + ab_im * li) / den
    f_im = (ab_im * lr - (ab_re - 1.0) * li) / den
    are_ref[...] = ab_re
    aim_ref[...] = ab_im
    for g in range(lr.shape[0]):
        fr = f_re[g:g + 1, :]
        fi = f_im[g:g + 1, :]
        br = bre_ref[g]
        bi = bim_ref[g]
        bbre_ref[g] = fr * br - fi * bi
        bbim_ref[g] = fr * bi + fi * br


def _s5_prep(lam_re, lam_im, log_dt, b_re, b_im):
    g, n = lam_re.shape
    p = b_re.shape[-1]
    bt_re = jnp.swapaxes(b_re, 1, 2)
    bt_im = jnp.swapaxes(b_im, 1, 2)
    a_re, a_im, bb_re, bb_im = pl.pallas_call(
        _s5_prep_kernel,
        out_shape=[jax.ShapeDtypeStruct((g, n), F32)] * 2 + [jax.ShapeDtypeStruct((g, p, n), F32)] * 2,
        name="s5_discretize",
    )(lam_re, lam_im, log_dt.reshape(g, 1), bt_re, bt_im)
    gs = LANES // p
    eye = jnp.eye(gs, dtype=F32)

    def blockdiag_in(bb):
        return jnp.einsum("kgpn,gh->kgphn", bb.reshape(g // gs, gs, p, n), eye).reshape(
            g // gs, gs * p, gs * n).astype(BF16)

    def sublane_rows(a):
        return jnp.broadcast_to(a.reshape(1, g * n), (SUBLANES, g * n))

    return sublane_rows(a_re), sublane_rows(a_im), blockdiag_in(bb_re), blockdiag_in(bb_im)


def _blockdiag_out(c):
    g, p, n = c.shape
    gs = LANES // p
    eye = jnp.eye(gs, dtype=F32)
    return jnp.einsum("kgpn,gh->kgnhp", c.reshape(g // gs, gs, p, n), eye).reshape(
        g // gs, gs * n, gs * p).astype(BF16)


def _s5_kernel(u_ref, h0re_ref, h0im_ref, are_ref, aim_ref, bbre_ref, bbim_ref, cre_ref, cim_ref,
               d_ref, gw_ref, gb_ref, y_ref, hre_out, him_out, ut, yt, xre, xim, hre_s, him_s,
               *, n_grp, nb, tb):
    i = pl.program_id(0)
    n_seq = n_grp * nb

    @pl.when(i == 0)
    def _():
        hre_s[...] = h0re_ref[...]
        him_s[...] = h0im_ref[...]

    n_slab, slab_in, slab_st = bbre_ref.shape
    for k in range(n_slab):
        lanes = slice(k * slab_in, (k + 1) * slab_in)
        if n_grp == 1:
            ut[k] = u_ref[0, :, lanes]
        else:
            for b in range(n_grp):
                ut[k, pl.ds(b, tb, stride=n_grp), :] = u_ref[b, :, lanes]
    def project_in(k, x, bb_ref):
        x[:, k * slab_st:(k + 1) * slab_st] = _dot(ut[k].astype(BF16), bb_ref[k])

    def read_out(k, x, c_ref):
        return _dot(x[:, k * slab_st:(k + 1) * slab_st].astype(BF16), c_ref[k])

    def combine(k, y_re, y_im):
        return y_re - y_im + d_ref[:, k * slab_in:(k + 1) * slab_in] * ut[k]

    def scan_pieces(cols):
        items = [(r0, t) for r0 in range(0, n_seq, SUBLANES) for t in range(tb)]
        per = len(items) // SCAN_PIECES

        def run(chunk):
            hr = hi = cur = None
            for r0, t in chunk:
                if r0 != cur:
                    if cur is not None:
                        hre_s[cur:cur + SUBLANES, cols] = hr
                        him_s[cur:cur + SUBLANES, cols] = hi
                    hr, hi, cur = hre_s[r0:r0 + SUBLANES, cols], him_s[r0:r0 + SUBLANES, cols], r0
                ar = are_ref[:, cols]
                ai = aim_ref[:, cols]
                r = t * n_seq + r0
                nr = ar * hr - ai * hi + xre[r:r + SUBLANES, cols]
                ni = ar * hi + ai * hr + xim[r:r + SUBLANES, cols]
                xre[r:r + SUBLANES, cols] = nr
                xim[r:r + SUBLANES, cols] = ni
                hr, hi = nr, ni
            hre_s[cur:cur + SUBLANES, cols] = hr
            him_s[cur:cur + SUBLANES, cols] = hi

        return [functools.partial(run, items[p * per:(p + 1) * per]) for p in range(SCAN_PIECES)]

    project_in(0, xre, bbre_ref)
    project_in(0, xim, bbim_ref)
    y_first = []
    for k in range(n_slab):
        pieces = scan_pieces(slice(k * slab_st, (k + 1) * slab_st))
        for p, piece in enumerate(pieces):
            x, bb, c = ((xre, bbre_ref, cre_ref), (xim, bbim_ref, cim_ref))[p // (SCAN_PIECES // 2)]
            if p % (SCAN_PIECES // 2) == 0:
                if k + 1 < n_slab:
                    project_in(k + 1, x, bb)
                else:
                    y_first.append(read_out(0, x, c))
            piece()
    parts = [combine(0, *y_first)]
    parts += [combine(k, read_out(k, xre, cre_ref), read_out(k, xim, cim_ref)) for k in range(1, n_slab)]
    z = jax.nn.gelu(jnp.concatenate(parts, axis=-1))
    out = z * jax.nn.sigmoid(_dot(z.astype(BF16), gw_ref[...]) + gb_ref[...])
    for k in range(n_slab):
        lanes = slice(k * slab_in, (k + 1) * slab_in)
        if n_grp == 1:
            y_ref[0, :, lanes] = out[:, lanes].astype(y_ref.dtype)
        else:
            yt[k] = out[:, lanes]
            for b in range(n_grp):
                y_ref[b, :, lanes] = yt[k, pl.ds(b, tb, stride=n_grp), :].astype(y_ref.dtype)
    hre_out[...] = hre_s[...]
    him_out[...] = him_s[...]


def _s5_call(u, h0_re, h0_im, prm, *, nb):
    a_re, a_im, bb_re, bb_im, c_re, c_im, d_skip, glu_w, glu_b = prm
    n_grp, r, d_a = u.shape
    assert n_grp == 1 or nb == 1
    t = r // nb
    n_seq = n_grp * nb
    n_state = a_re.shape[1]
    tb = max(1, min(t, ROW_TILE // n_seq))
    rows = tb * n_seq
    consts = [a_re, a_im, bb_re, bb_im, c_re, c_im, d_skip, glu_w, glu_b]
    state_spec = pl.BlockSpec((n_seq, n_state), lambda i: (0, 0))
    blk = pl.BlockSpec((n_grp, tb * nb, d_a), lambda i: (0, i, 0))
    return pl.pallas_call(
        functools.partial(_s5_kernel, n_grp=n_grp, nb=nb, tb=tb),
        grid=(t // tb,),
        in_specs=[blk, state_spec, state_spec] + [_const_spec(c.shape) for c in consts],
        out_specs=[blk, state_spec, state_spec],
        out_shape=[jax.ShapeDtypeStruct(u.shape, BF16),
                   jax.ShapeDtypeStruct((n_seq, n_state), F32),
                   jax.ShapeDtypeStruct((n_seq, n_state), F32)],
        scratch_shapes=[pltpu.VMEM((d_a // LANES, rows, LANES), F32), pltpu.VMEM((d_a // LANES, rows, LANES), F32),
                        pltpu.VMEM((rows, n_state), F32), pltpu.VMEM((rows, n_state), F32),
                        pltpu.VMEM((n_seq, n_state), F32), pltpu.VMEM((n_seq, n_state), F32)],
        compiler_params=_params("arbitrary"),
        name="s5_mixer",
    )(u, h0_re, h0_im, *consts)


def _sgu_chunks(u, vn, w_ref, bias_ref, y_ref):
    n_heads, cl, _ = w_ref.shape
    hd = vn.shape[1] // n_heads
    row = lax.broadcasted_iota(jnp.int32, (cl, cl), 0)
    col = lax.broadcasted_iota(jnp.int32, (cl, cl), 1)
    w = [jnp.where(col <= row, w_ref[h], 0.0).astype(BF16) for h in range(n_heads)]
    heads_per_slab = LANES // hd
    lane = lax.broadcasted_iota(jnp.int32, (cl, LANES), 1)
    for c in range(vn.shape[0] // cl):
        r = slice(c * cl, (c + 1) * cl)
        slabs = []
        for s in range(n_heads // heads_per_slab):
            v = vn[r, s * LANES:(s + 1) * LANES].astype(BF16)
            mixed = _dot(w[s * heads_per_slab], v)
            for j in range(1, heads_per_slab):
                mixed = jnp.where(lane >= j * hd, _dot(w[s * heads_per_slab + j], v), mixed)
            slabs.append(mixed)
        mixed = jnp.concatenate(slabs, axis=-1) + bias_ref[...]
        y_ref[r, :] = (u[r, :] * mixed).astype(y_ref.dtype)


def _sgu_short(u, vn, wl_ref, bias_ref, y_ref, *, nb):
    for i in range(wl_ref.shape[0]):
        mixed = bias_ref[i:i + 1, :]
        for j in range(i + 1):
            mixed = mixed + wl_ref[i, j:j + 1, :] * vn[j * nb:(j + 1) * nb, :]
        y_ref[i * nb:(i + 1) * nb, :] = (u[i * nb:(i + 1) * nb, :] * mixed).astype(y_ref.dtype)


def _sgu_params(w_s, b_s, d_b, t):
    hd = d_b // w_s.shape[1]
    cl = min(t, SGU_CHUNK)
    bias = jnp.repeat(jnp.swapaxes(b_s[:, :, :cl], 1, 2), hd, axis=2)
    if t % SGU_CHUNK == 0:
        return w_s, bias
    assert t < SGU_CHUNK
    return jnp.repeat(jnp.transpose(w_s[:, :, :cl, :cl], (0, 2, 3, 1)), hd, axis=3), bias


def _interleave(a):
    return jnp.swapaxes(a, 0, 1).reshape(1, a.shape[0] * a.shape[1], a.shape[2])


def _deinterleave(a, nb):
    return jnp.swapaxes(a.reshape(-1, nb, a.shape[2]), 0, 1)


def _run_trunk(x, s5_re, s5_im, pool_buf, conv_buf, prefix_valid, W, *, interleaved):
    batch, t, d = x.shape
    depth = W["norm_mix"].shape[0]
    nb = batch if interleaved else 1
    pack = _interleave if interleaved else (lambda a: a)
    unpack = (lambda a: _deinterleave(a, nb)) if interleaved else (lambda a: a)
    common = dict(nb=nb, prefix_valid=prefix_valid, sgu_prm=_sgu_params(W["sgu_w"], W["sgu_b"], d // 2, t))

    def kind(layer):
        return "final" if layer == depth else ("even" if layer % 2 == 0 else "odd")

    def hist(layer):
        return (pack(pool_buf[layer // 2]), pack(conv_buf[layer // 2])) if kind(layer) == "odd" else None

    h = pack(x)
    mixer_in = _trunk_call(h, 0, W, mix=None, nxt=kind(0), odd_hist=hist(0), **common)
    new_re, new_im, new_v, new_pool, new_conv = [], [], [], [], []
    for layer in range(depth):
        i = layer // 2
        if layer % 2 == 0:
            u_a, y_b = mixer_in[:2]
            y_a, hre, him = _s5_call(u_a, s5_re[i].reshape(batch, -1), s5_im[i].reshape(batch, -1), W["s5"][i],
                                     nb=nb)
            new_re.append(hre.reshape(s5_re[i].shape))
            new_im.append(him.reshape(s5_im[i].shape))
            if interleaved:
                new_v.append(unpack(mixer_in[2]))
            mix = (y_a, y_b, W["w_out_even"], i)
        else:
            y_c, y_d, pout, cout = mixer_in
            new_pool.append(unpack(pout))
            new_conv.append(unpack(cout))
            mix = (y_c, y_d, W["w_out_odd"], i)
        outs = _trunk_call(h, layer + 1, W, mix=mix, nxt=kind(layer + 1), odd_hist=hist(layer + 1), **common)
        h, mixer_in = outs[0], outs[1:]
    return (unpack(h), jnp.stack(new_re), jnp.stack(new_im), jnp.stack(new_v) if new_v else None,
            jnp.stack(new_pool), jnp.stack(new_conv))


def kernel(x_prompt, x_sample, state_s5_re, state_s5_im, state_pool, state_conv, norm_mix, norm_ffn, norm_final, w_in_even, w_out_even, s5_lambda_re, s5_lambda_im, s5_log_dt, s5_b_re, s5_b_im, s5_c_re, s5_c_im, s5_d, s5_glu_w, s5_glu_b, sgu_norm, sgu_w, sgu_b, w_in_odd, w_out_odd, pool_w, pool_scale, conv_w, conv_b, ffn_w_gate, ffn_w_up, ffn_w_down):
    n_even, n_odd = w_in_even.shape[0], w_in_odd.shape[0]
    depth = norm_mix.shape[0]
    s5 = []
    for i in range(n_even):
        a_re, a_im, bb_re, bb_im = _s5_prep(s5_lambda_re[i], s5_lambda_im[i], s5_log_dt[i], s5_b_re[i], s5_b_im[i])
        s5.append((a_re, a_im, bb_re, bb_im, _blockdiag_out(s5_c_re[i]), _blockdiag_out(s5_c_im[i]),
                   s5_d[i].reshape(1, -1), s5_glu_w[i].astype(BF16), s5_glu_b[i].reshape(1, -1)))
    d = norm_mix.shape[1]
    W = dict(
        norm_mix=norm_mix.reshape(depth, 1, d), norm_ffn=norm_ffn.reshape(depth, 1, d),
        norm_final=norm_final.reshape(1, d),
        w_in_even=w_in_even.astype(BF16), w_out_even=w_out_even.astype(BF16),
        w_in_odd=w_in_odd.astype(BF16), w_out_odd=w_out_odd.astype(BF16),
        sgu_norm=sgu_norm.reshape(n_even, 1, -1), sgu_w=sgu_w, sgu_b=sgu_b, s5=s5,
        pool_w=pool_w.astype(BF16), pool_scale=pool_scale.reshape(n_odd, 1, -1), conv_w=conv_w,
        conv_b=conv_b.reshape(n_odd, 1, -1),
        ffn_gate=ffn_w_gate.astype(BF16), ffn_up=ffn_w_up.astype(BF16), ffn_down=ffn_w_down.astype(BF16),
    )
    bp = x_prompt.shape[0]
    z_s5 = jnp.zeros((n_even, bp) + state_s5_re.shape[2:], state_s5_re.dtype)
    z_pool = jnp.zeros((n_odd, bp) + state_pool.shape[2:], x_prompt.dtype)
    z_conv = jnp.zeros((n_odd, bp) + state_conv.shape[2:], x_prompt.dtype)
    y_p, p_re, p_im, _, p_pool, p_conv = _run_trunk(x_prompt, z_s5, z_s5, z_pool, z_conv, False, W,
                                                    interleaved=False)
    y_s, s_re, s_im, s_v, s_pool, s_conv = _run_trunk(
        x_sample, state_s5_re, state_s5_im, state_pool, state_conv, True, W, interleaved=True)
    return (y_p, y_s, p_re, p_im, p_pool, p_conv, s_re, s_im, s_v, s_pool, s_conv)
```

```python
import functools

import jax
import jax.numpy as jnp
from jax import lax
from jax.experimental import pallas as pl
from jax.experimental.pallas import tpu as pltpu

F32 = jnp.float32
BF16 = jnp.bfloat16

EPS = 1e-6
SGU_CHUNK = 128
POOL_WINDOWS = (2, 4, 8, 16)
POOL_BUF = max(POOL_WINDOWS) - 1
CONV_W = 3

LANES = 128
SUBLANES = 8
ROW_TILE = 512
FF_CHUNK = 512
SCAN_PIECES = 4
VMEM_LIMIT = 56 * 1024 * 1024


def _dot(a, b):
    return jnp.dot(a, b, preferred_element_type=F32)


def _rms(x, g):
    return x * lax.rsqrt(jnp.mean(x * x, axis=-1, keepdims=True) + EPS) * g


def _round_up(n, m):
    return -(-n // m) * m


def _const_spec(shape):
    zeros = (0,) * len(shape)
    return pl.BlockSpec(shape, lambda *_: zeros, pipeline_mode=pl.Buffered(1))


def _layer_spec(stack, layer):
    zeros = (0,) * (stack.ndim - 1)
    return pl.BlockSpec((None,) + stack.shape[1:], lambda *_: (layer,) + zeros, pipeline_mode=pl.Buffered(1))


def _params(*semantics):
    return pltpu.CompilerParams(dimension_semantics=semantics, vmem_limit_bytes=VMEM_LIMIT)


def _poolconv_pieces(xc, z, bg, i, pbuf_ref, cbuf_ref, pw_ref, ps_ref, cw_ref, cb_ref,
                     yc_ref, yd_ref, pout_ref, cout_ref, fullc, fullz, diff, *, nb, prefix_valid):
    hp, hc = POOL_BUF * nb, (CONV_W - 1) * nb
    p0, c0 = pbuf_ref.shape[0], cbuf_ref.shape[0]
    r = xc.shape[0]
    cg = xc.shape[1] // len(POOL_WINDOWS)
    first = i == 0

    def load_tile():
        fullc[0:p0, :] = jnp.where(first, pbuf_ref[...], fullc[0:p0, :])
        fullz[0:c0, :] = jnp.where(first, cbuf_ref[...], fullz[0:c0, :])
        fullc[p0:p0 + r, :] = xc[...]
        fullz[c0:c0 + r, :] = z[...]

    def conv():
        acc = cb_ref[...]
        for k in range(CONV_W):
            back = (CONV_W - 1 - k) * nb
            acc = acc + fullz[c0 - back:c0 - back + r, :] * cw_ref[k:k + 1, :]
        yd_ref[...] = (bg[...] * acc).astype(yd_ref.dtype)

    def window_sum(rows, lanes, w):
        s = fullc[p0:p0 + rows, lanes]
        for k in range(1, w):
            s = s + fullc[p0 - k * nb:p0 - k * nb + rows, lanes]
        return s

    def pool(gi, w):
        lanes = slice(gi * cg, (gi + 1) * cg)
        diff[:, lanes] = window_sum(r, lanes, w) * (1.0 / w) - fullc[p0:p0 + r, lanes]
        if not prefix_valid:
            head = min(p0, r)
            t = lax.broadcasted_iota(jnp.int32, (head, 1), 0) // nb
            n = jnp.minimum(t + 1, w).astype(F32)
            short = window_sum(head, lanes, w) / n - fullc[p0:p0 + head, lanes]
            diff[0:head, lanes] = jnp.where(first, short, diff[0:head, lanes])

    def project():
        for gi in range(len(POOL_WINDOWS)):
            lanes = slice(gi * cg, (gi + 1) * cg)
            yc = _dot(diff[:, lanes].astype(BF16), pw_ref[gi]) * ps_ref[:, lanes]
            yc_ref[:, lanes] = yc.astype(yc_ref.dtype)

    def shift_history():
        pout_ref[...] = fullc[p0 + r - hp:p0 + r, :]
        cout_ref[...] = fullz[c0 + r - hc:c0 + r, :]
        new_pool = fullc[r:r + p0, :]
        new_conv = fullz[r:r + c0, :]
        fullc[0:p0, :] = new_pool
        fullz[0:c0, :] = new_conv

    pools = [functools.partial(pool, gi, w) for gi, w in enumerate(POOL_WINDOWS)]
    return [load_tile, conv] + pools + [project, shift_history]


def _trunk_kernel(*refs, has_mix, nxt, d_half, nb, prefix_valid, tiles_per_group, n_cast):
    refs = list(refs)
    h_ref = refs.pop(0)
    if has_mix:
        ya_ref, yb_ref, wo_ref, gf_ref, wg_ref, wu_ref, wd_ref = refs[:7]
        del refs[:7]
    gn_ref = refs.pop(0)
    if nxt != "final":
        win_ref = refs.pop(0)
    if nxt == "even":
        gs_ref, sw_ref, sb_ref = refs[:3]
        del refs[:3]
    if nxt == "odd":
        odd_in = refs[:6]
        del refs[:6]
    cast_in = refs[:n_cast]
    del refs[:n_cast]
    outs = refs
    for src, dst in zip(cast_in, outs[:n_cast]):
        dst[...] = src[...].astype(dst.dtype)
    del outs[:n_cast]
    s = pl.program_id(0)
    if nxt == "odd":
        odd_out = outs[-7:]

        @pl.when(s == 0)
        def _():
            for ref in odd_out[-3:]:
                ref[...] = jnp.zeros(ref.shape, ref.dtype)

    h = h_ref[...]
    if has_mix:
        h = h + _dot(ya_ref[...], wo_ref[:d_half, :]) + _dot(yb_ref[...], wo_ref[d_half:, :])
        hn = _rms(h, gf_ref[...]).astype(BF16)
        d_ff = wg_ref.shape[1]
        acc = None
        for c0 in range(0, d_ff, FF_CHUNK):
            c1 = min(c0 + FF_CHUNK, d_ff)
            g = _dot(hn, wg_ref[:, c0:c1])
            u = _dot(hn, wu_ref[:, c0:c1])
            a = (g * jax.nn.sigmoid(g) * u).astype(BF16)
            part = _dot(a, wd_ref[c0:c1, :])
            acc = part if acc is None else acc + part
        h = h + acc
        if nxt != "final":
            outs.pop(0)[...] = h
    hn = _rms(h, gn_ref[...])
    if nxt == "final":
        outs[0][...] = hn
        return
    proj = _dot(hn.astype(BF16), win_ref[...])
    if nxt == "even":
        ua_ref, yb_ref = outs[:2]
        ua_ref[...] = proj[:, :d_half]
        ub = proj[:, d_half:2 * d_half]
        vn = _rms(proj[:, 2 * d_half:], gs_ref[...])
        if nb == 1:
            _sgu_chunks(ub, vn, sw_ref, sb_ref, yb_ref)
        else:
            _sgu_short(ub, vn, sw_ref, sb_ref, yb_ref, nb=nb)
            outs[2][...] = vn
    else:
        xc = proj[:, :d_half]
        z = proj[:, 3 * d_half:] * proj[:, d_half:2 * d_half]
        bg = proj[:, 2 * d_half:3 * d_half]
        for piece in _poolconv_pieces(xc, z, bg, s % tiles_per_group, *odd_in, *odd_out, nb=nb,
                                      prefix_valid=prefix_valid):
            piece()


def _stream_rows(n_rows, n_steps):
    return next(rpb for rpb in range(2 * SUBLANES, n_rows + 1, 2 * SUBLANES)
                if n_rows % rpb == 0 and n_rows // rpb <= n_steps)


def _trunk_call(h, layer, W, *, mix, nxt, nb, prefix_valid, sgu_prm, ffn=None, cast=None, odd_hist=None):
    n_grp, r, d = h.shape
    d_half = d // 2
    tile = min(ROW_TILE, r)
    i = layer // 2
    tpg = r // tile
    n_tiles = n_grp * tpg

    def rows(c):
        return pl.BlockSpec((None, tile, c), lambda s: (s // tpg, s % tpg, 0))

    def hist_spec(n_rows):
        return pl.BlockSpec((None, n_rows, d_half), lambda s: (s // tpg, 0, 0))

    args, specs = [h], [rows(d)]
    if mix is not None:
        ya, yb, w_out, wi = mix
        args += [ya, yb, w_out, W["norm_ffn"], *ffn]
        specs += [rows(d_half), rows(d_half), _layer_spec(w_out, wi), _layer_spec(W["norm_ffn"], layer - 1)]
        specs += [_const_spec(w.shape) for w in ffn]
    if nxt == "final":
        args.append(W["norm_final"])
        specs.append(_const_spec(W["norm_final"].shape))
    else:
        w_in = W["w_in_even"] if nxt == "even" else W["w_in_odd"]
        args += [W["norm_mix"], w_in]
        specs += [_layer_spec(W["norm_mix"], layer), _layer_spec(w_in, i)]
    scratch = []
    if nxt == "even":
        sgu = [W["sgu_norm"], *sgu_prm]
        args += sgu
        specs += [_layer_spec(w, i) for w in sgu]
        assert (nb == 1 and tile % SGU_CHUNK == 0) or (n_tiles == 1 and tile // nb < SGU_CHUNK)
        out_shapes = [jax.ShapeDtypeStruct((n_grp, r, d_half), F32), jax.ShapeDtypeStruct((n_grp, r, d_half), BF16)]
        if nb > 1:
            out_shapes.append(jax.ShapeDtypeStruct((n_grp, r, d_half), F32))
        out_specs = [rows(d_half)] * len(out_shapes)
    elif nxt == "odd":
        pool_buf, conv_buf = odd_hist
        hp, hc = POOL_BUF * nb, (CONV_W - 1) * nb
        p0, c0 = _round_up(hp, SUBLANES), _round_up(hc, SUBLANES)
        pool_buf = jnp.pad(pool_buf, ((0, 0), (p0 - hp, 0), (0, 0)))
        conv_buf = jnp.pad(conv_buf, ((0, 0), (c0 - hc, 0), (0, 0)))
        odd_w = [W["pool_w"], W["pool_scale"], W["conv_w"], W["conv_b"]]
        args += [pool_buf, conv_buf] + odd_w
        specs += [hist_spec(p0), hist_spec(c0)] + [_layer_spec(w, i) for w in odd_w]
        scratch = [pltpu.VMEM((p0 + tile, d_half), F32), pltpu.VMEM((c0 + tile, d_half), F32),
                   pltpu.VMEM((tile, d_half), F32)]
        out_shapes = [jax.ShapeDtypeStruct((n_grp, r, d_half), BF16)] * 2 + [
            jax.ShapeDtypeStruct((n_grp, hp, d_half), F32), jax.ShapeDtypeStruct((n_grp, hc, d_half), F32)]
        out_specs = [rows(d_half)] * 2 + [hist_spec(hp), hist_spec(hc)]
    else:
        out_shapes, out_specs = [], []
    if mix is not None or nxt == "final":
        out_shapes = [jax.ShapeDtypeStruct((n_grp, r, d), F32)] + out_shapes
        out_specs = [rows(d)] + out_specs
    n_cast = 0
    if cast is not None:
        stacks, cast_layer = cast
        n_cast = len(stacks)
        cast_shapes, cast_specs = [], []
        for w in stacks:
            n_rows, n_cols = w.shape[1:]
            rpb = _stream_rows(n_rows, n_tiles)
            n_blk = n_rows // rpb
            args.append(w)
            specs.append(pl.BlockSpec((None, rpb, n_cols),
                                      lambda s, n_blk=n_blk: (cast_layer, s * n_blk // n_tiles, 0)))
            cast_shapes.append(jax.ShapeDtypeStruct((n_rows, n_cols), BF16))
            cast_specs.append(pl.BlockSpec((rpb, n_cols), lambda s, n_blk=n_blk: (s * n_blk // n_tiles, 0)))
        out_shapes, out_specs = cast_shapes + out_shapes, cast_specs + out_specs

    outs = pl.pallas_call(
        functools.partial(_trunk_kernel, has_mix=mix is not None, nxt=nxt, d_half=d_half, nb=nb,
                          prefix_valid=prefix_valid, tiles_per_group=tpg, n_cast=n_cast),
        grid=(n_tiles,),
        in_specs=specs,
        out_specs=out_specs,
        out_shape=out_shapes,
        scratch_shapes=scratch,
        compiler_params=_params("arbitrary"),
        name=f"trunk_{'mix' if mix is not None else 'in'}_{nxt}",
    )(*args)
    return outs[n_cast:], tuple(outs[:n_cast])


def _s5_prep_kernel(lr_ref, li_ref, ldt_ref, bre_ref, bim_ref, are_ref, aim_ref, bbre_ref, bbim_ref):
    lr = lr_ref[...]
    li = li_ref[...]
    dt = jnp.exp(ldt_ref[...])
    mag = jnp.exp(lr * dt)
    ang = li * dt
    ab_re = mag * jnp.cos(ang)
    ab_im = mag * jnp.sin(ang)
    den = lr * lr + li * li
    f_re = ((ab_re - 1.0) * lr + ab_im * li) / den
    f_im = (ab_im * lr - (ab_re - 1.0) * li) / den
    are_ref[...] = ab_re
    aim_ref[...] = ab_im
    for g in range(lr.shape[0]):
        fr = f_re[g:g + 1, :]
        fi = f_im[g:g + 1, :]
        br = bre_ref[g]
        bi = bim_ref[g]
        bbre_ref[g] = fr * br - fi * bi
        bbim_ref[g] = fr * bi + fi * br


def _s5_prep(lam_re, lam_im, log_dt, b_re, b_im):
    g, n = lam_re.shape
    p = b_re.shape[-1]
    bt_re = jnp.swapaxes(b_re, 1, 2)
    bt_im = jnp.swapaxes(b_im, 1, 2)
    a_re, a_im, bb_re, bb_im = pl.pallas_call(
        _s5_prep_kernel,
        out_shape=[jax.ShapeDtypeStruct((g, n), F32)] * 2 + [jax.ShapeDtypeStruct((g, p, n), F32)] * 2,
        name="s5_discretize",
    )(lam_re, lam_im, log_dt.reshape(g, 1), bt_re, bt_im)
    gs = LANES // p
    eye = jnp.eye(gs, dtype=F32)

    def blockdiag_in(bb):
        return jnp.einsum("kgpn,gh->kgphn", bb.reshape(g // gs, gs, p, n), eye).reshape(
            g // gs, gs * p, gs * n).astype(BF16)

    def sublane_rows(a):
        return jnp.broadcast_to(a.reshape(1, g * n), (SUBLANES, g * n))

    return sublane_rows(a_re), sublane_rows(a_im), blockdiag_in(bb_re), blockdiag_in(bb_im)


def _blockdiag_out(c):
    g, p, n = c.shape
    gs = LANES // p
    eye = jnp.eye(gs, dtype=F32)
    return jnp.einsum("kgpn,gh->kgnhp", c.reshape(g // gs, gs, p, n), eye).reshape(
        g // gs, gs * n, gs * p).astype(BF16)


def _s5_kernel(u_ref, h0re_ref, h0im_ref, are_ref, aim_ref, bbre_ref, bbim_ref, cre_ref, cim_ref,
               d_ref, gw_ref, gb_ref, y_ref, hre_out, him_out, ut, yt, xre, xim, hre_s, him_s,
               *, n_grp, nb, tb):
    i = pl.program_id(0)
    n_seq = n_grp * nb

    @pl.when(i == 0)
    def _():
        hre_s[...] = h0re_ref[...]
        him_s[...] = h0im_ref[...]

    n_slab, slab_in, slab_st = bbre_ref.shape
    for k in range(n_slab):
        lanes = slice(k * slab_in, (k + 1) * slab_in)
        if n_grp == 1:
            ut[k] = u_ref[0, :, lanes]
        else:
            for b in range(n_grp):
                ut[k, pl.ds(b, tb, stride=n_grp), :] = u_ref[b, :, lanes]
    def project_in(k, x, bb_ref):
        x[:, k * slab_st:(k + 1) * slab_st] = _dot(ut[k].astype(BF16), bb_ref[k])

    def read_out(k, x, c_ref):
        return _dot(x[:, k * slab_st:(k + 1) * slab_st].astype(BF16), c_ref[k])

    def combine(k, y_re, y_im):
        return y_re - y_im + d_ref[:, k * slab_in:(k + 1) * slab_in] * ut[k]

    def scan_pieces(cols):
        items = [(r0, t) for r0 in range(0, n_seq, SUBLANES) for t in range(tb)]
        per = len(items) // SCAN_PIECES

        def run(chunk):
            hr = hi = cur = None
            for r0, t in chunk:
                if r0 != cur:
                    if cur is not None:
                        hre_s[cur:cur + SUBLANES, cols] = hr
                        him_s[cur:cur + SUBLANES, cols] = hi
                    hr, hi, cur = hre_s[r0:r0 + SUBLANES, cols], him_s[r0:r0 + SUBLANES, cols], r0
                ar = are_ref[:, cols]
                ai = aim_ref[:, cols]
                r = t * n_seq + r0
                nr = ar * hr - ai * hi + xre[r:r + SUBLANES, cols]
                ni = ar * hi + ai * hr + xim[r:r + SUBLANES, cols]
                xre[r:r + SUBLANES, cols] = nr
                xim[r:r + SUBLANES, cols] = ni
                hr, hi = nr, ni
            hre_s[cur:cur + SUBLANES, cols] = hr
            him_s[cur:cur + SUBLANES, cols] = hi

        return [functools.partial(run, items[p * per:(p + 1) * per]) for p in range(SCAN_PIECES)]

    project_in(0, xre, bbre_ref)
    project_in(0, xim, bbim_ref)
    y_first = []
    for k in range(n_slab):
        pieces = scan_pieces(slice(k * slab_st, (k + 1) * slab_st))
        for p, piece in enumerate(pieces):
            x, bb, c = ((xre, bbre_ref, cre_ref), (xim, bbim_ref, cim_ref))[p // (SCAN_PIECES // 2)]
            if p % (SCAN_PIECES // 2) == 0:
                if k + 1 < n_slab:
                    project_in(k + 1, x, bb)
                else:
                    y_first.append(read_out(0, x, c))
            piece()
    parts = [combine(0, *y_first)]
    parts += [combine(k, read_out(k, xre, cre_ref), read_out(k, xim, cim_ref)) for k in range(1, n_slab)]
    z = jax.nn.gelu(jnp.concatenate(parts, axis=-1))
    out = z * jax.nn.sigmoid(_dot(z.astype(BF16), gw_ref[...]) + gb_ref[...])
    for k in range(n_slab):
        lanes = slice(k * slab_in, (k + 1) * slab_in)
        if n_grp == 1:
            y_ref[0, :, lanes] = out[:, lanes].astype(y_ref.dtype)
        else:
            yt[k] = out[:, lanes]
            for b in range(n_grp):
                y_ref[b, :, lanes] = yt[k, pl.ds(b, tb, stride=n_grp), :].astype(y_ref.dtype)
    hre_out[...] = hre_s[...]
    him_out[...] = him_s[...]


def _s5_call(u, h0_re, h0_im, prm, *, nb):
    a_re, a_im, bb_re, bb_im, c_re, c_im, d_skip, glu_w, glu_b = prm
    n_grp, r, d_a = u.shape
    assert n_grp == 1 or nb == 1
    t = r // nb
    n_seq = n_grp * nb
    n_state = a_re.shape[1]
    tb = max(1, min(t, ROW_TILE // n_seq))
    rows = tb * n_seq
    consts = [a_re, a_im, bb_re, bb_im, c_re, c_im, d_skip, glu_w, glu_b]
    state_spec = pl.BlockSpec((n_seq, n_state), lambda i: (0, 0))
    blk = pl.BlockSpec((n_grp, tb * nb, d_a), lambda i: (0, i, 0))
    return pl.pallas_call(
        functools.partial(_s5_kernel, n_grp=n_grp, nb=nb, tb=tb),
        grid=(t // tb,),
        in_specs=[blk, state_spec, state_spec] + [_const_spec(c.shape) for c in consts],
        out_specs=[blk, state_spec, state_spec],
        out_shape=[jax.ShapeDtypeStruct(u.shape, BF16),
                   jax.ShapeDtypeStruct((n_seq, n_state), F32),
                   jax.ShapeDtypeStruct((n_seq, n_state), F32)],
        scratch_shapes=[pltpu.VMEM((d_a // LANES, rows, LANES), F32), pltpu.VMEM((d_a // LANES, rows, LANES), F32),
                        pltpu.VMEM((rows, n_state), F32), pltpu.VMEM((rows, n_state), F32),
                        pltpu.VMEM((n_seq, n_state), F32), pltpu.VMEM((n_seq, n_state), F32)],
        compiler_params=_params("arbitrary"),
        name="s5_mixer",
    )(u, h0_re, h0_im, *consts)


def _sgu_chunks(u, vn, w_ref, bias_ref, y_ref):
    n_heads, cl, _ = w_ref.shape
    hd = vn.shape[1] // n_heads
    row = lax.broadcasted_iota(jnp.int32, (cl, cl), 0)
    col = lax.broadcasted_iota(jnp.int32, (cl, cl), 1)
    w = [jnp.where(col <= row, w_ref[h], 0.0).astype(BF16) for h in range(n_heads)]
    heads_per_slab = LANES // hd
    lane = lax.broadcasted_iota(jnp.int32, (cl, LANES), 1)
    for c in range(vn.shape[0] // cl):
        r = slice(c * cl, (c + 1) * cl)
        slabs = []
        for s in range(n_heads // heads_per_slab):
            v = vn[r, s * LANES:(s + 1) * LANES].astype(BF16)
            mixed = _dot(w[s * heads_per_slab], v)
            for j in range(1, heads_per_slab):
                mixed = jnp.where(lane >= j * hd, _dot(w[s * heads_per_slab + j], v), mixed)
            slabs.append(mixed)
        mixed = jnp.concatenate(slabs, axis=-1) + bias_ref[...]
        y_ref[r, :] = (u[r, :] * mixed).astype(y_ref.dtype)


def _sgu_short(u, vn, wl_ref, bias_ref, y_ref, *, nb):
    for i in range(wl_ref.shape[0]):
        mixed = bias_ref[i:i + 1, :]
        for j in range(i + 1):
            mixed = mixed + wl_ref[i, j:j + 1, :] * vn[j * nb:(j + 1) * nb, :]
        y_ref[i * nb:(i + 1) * nb, :] = (u[i * nb:(i + 1) * nb, :] * mixed).astype(y_ref.dtype)


def _sgu_params(w_s, b_s, d_b, t):
    hd = d_b // w_s.shape[1]
    cl = min(t, SGU_CHUNK)
    bias = jnp.repeat(jnp.swapaxes(b_s[:, :, :cl], 1, 2), hd, axis=2)
    if t % SGU_CHUNK == 0:
        return w_s, bias
    assert t < SGU_CHUNK
    return jnp.repeat(jnp.transpose(w_s[:, :, :cl, :cl], (0, 2, 3, 1)), hd, axis=3), bias


def _interleave(a):
    return jnp.swapaxes(a, 0, 1).reshape(1, a.shape[0] * a.shape[1], a.shape[2])


def _deinterleave(a, nb):
    return jnp.swapaxes(a.reshape(-1, nb, a.shape[2]), 0, 1)


def _run_trunk(x, s5_re, s5_im, pool_buf, conv_buf, prefix_valid, W, *, interleaved, ffn_bf16=None):
    batch, t, d = x.shape
    depth = W["norm_mix"].shape[0]
    nb = batch if interleaved else 1
    pack = _interleave if interleaved else (lambda a: a)
    unpack = (lambda a: _deinterleave(a, nb)) if interleaved else (lambda a: a)
    common = dict(nb=nb, prefix_valid=prefix_valid, sgu_prm=_sgu_params(W["sgu_w"], W["sgu_b"], d // 2, t))

    def kind(layer):
        return "final" if layer == depth else ("even" if layer % 2 == 0 else "odd")

    def hist(layer):
        return (pack(pool_buf[layer // 2]), pack(conv_buf[layer // 2])) if kind(layer) == "odd" else None

    convert = ffn_bf16 is None
    ffn_bf16 = [] if convert else ffn_bf16

    def cast(layer):
        return ((W["ffn_gate"], W["ffn_up"], W["ffn_down"]), layer) if convert and layer < depth else None

    h = pack(x)
    mixer_in, rounded = _trunk_call(h, 0, W, mix=None, nxt=kind(0), odd_hist=hist(0), cast=cast(0), **common)
    ffn_bf16 += [rounded] if convert else []
    new_re, new_im, new_v, new_pool, new_conv = [], [], [], [], []
    for layer in range(depth):
        i = layer // 2
        if layer % 2 == 0:
            u_a, y_b = mixer_in[:2]
            y_a, hre, him = _s5_call(u_a, s5_re[i].reshape(batch, -1), s5_im[i].reshape(batch, -1), W["s5"][i],
                                     nb=nb)
            new_re.append(hre.reshape(s5_re[i].shape))
            new_im.append(him.reshape(s5_im[i].shape))
            if interleaved:
                new_v.append(unpack(mixer_in[2]))
            mix = (y_a, y_b, W["w_out_even"], i)
        else:
            y_c, y_d, pout, cout = mixer_in
            new_pool.append(unpack(pout))
            new_conv.append(unpack(cout))
            mix = (y_c, y_d, W["w_out_odd"], i)
        outs, rounded = _trunk_call(h, layer + 1, W, mix=mix, ffn=ffn_bf16[layer], nxt=kind(layer + 1),
                                    odd_hist=hist(layer + 1), cast=cast(layer + 1), **common)
        ffn_bf16 += [rounded] if rounded else []
        h, mixer_in = outs[0], outs[1:]
    return (unpack(h), jnp.stack(new_re), jnp.stack(new_im), jnp.stack(new_v) if new_v else None,
            jnp.stack(new_pool), jnp.stack(new_conv), ffn_bf16)


def kernel(x_prompt, x_sample, state_s5_re, state_s5_im, state_pool, state_conv, norm_mix, norm_ffn, norm_final, w_in_even, w_out_even, s5_lambda_re, s5_lambda_im, s5_log_dt, s5_b_re, s5_b_im, s5_c_re, s5_c_im, s5_d, s5_glu_w, s5_glu_b, sgu_norm, sgu_w, sgu_b, w_in_odd, w_out_odd, pool_w, pool_scale, conv_w, conv_b, ffn_w_gate, ffn_w_up, ffn_w_down):
    n_even, n_odd = w_in_even.shape[0], w_in_odd.shape[0]
    depth = norm_mix.shape[0]
    s5 = []
    for i in range(n_even):
        a_re, a_im, bb_re, bb_im = _s5_prep(s5_lambda_re[i], s5_lambda_im[i], s5_log_dt[i], s5_b_re[i], s5_b_im[i])
        s5.append((a_re, a_im, bb_re, bb_im, _blockdiag_out(s5_c_re[i]), _blockdiag_out(s5_c_im[i]),
                   s5_d[i].reshape(1, -1), s5_glu_w[i].astype(BF16), s5_glu_b[i].reshape(1, -1)))
    d = norm_mix.shape[1]
    W = dict(
        norm_mix=norm_mix.reshape(depth, 1, d), norm_ffn=norm_ffn.reshape(depth, 1, d),
        norm_final=norm_final.reshape(1, d),
        w_in_even=w_in_even.astype(BF16), w_out_even=w_out_even.astype(BF16),
        w_in_odd=w_in_odd.astype(BF16), w_out_odd=w_out_odd.astype(BF16),
        sgu_norm=sgu_norm.reshape(n_even, 1, -1), sgu_w=sgu_w, sgu_b=sgu_b, s5=s5,
        pool_w=pool_w.astype(BF16), pool_scale=pool_scale.reshape(n_odd, 1, -1), conv_w=conv_w,
        conv_b=conv_b.reshape(n_odd, 1, -1),
        ffn_gate=ffn_w_gate, ffn_up=ffn_w_up, ffn_down=ffn_w_down,
    )
    bp = x_prompt.shape[0]
    z_s5 = jnp.zeros((n_even, bp) + state_s5_re.shape[2:], state_s5_re.dtype)
    z_pool = jnp.zeros((n_odd, bp) + state_pool.shape[2:], x_prompt.dtype)
    z_conv = jnp.zeros((n_odd, bp) + state_conv.shape[2:], x_prompt.dtype)
    y_p, p_re, p_im, _, p_pool, p_conv, ffn_bf16 = _run_trunk(x_prompt, z_s5, z_s5, z_pool, z_conv, False, W,
                                                              interleaved=False)
    y_s, s_re, s_im, s_v, s_pool, s_conv, _ = _run_trunk(
        x_sample, state_s5_re, state_s5_im, state_pool, state_conv, True, W, interleaved=True, ffn_bf16=ffn_bf16)
    return (y_p, y_s, p_re, p_im, p_pool, p_conv, s_re, s_im, s_v, s_pool, s_conv)
```

```python
import functools

import jax
import jax.numpy as jnp
from jax import lax
from jax.experimental import pallas as pl
from jax.experimental.pallas import tpu as pltpu

F32 = jnp.float32
BF16 = jnp.bfloat16

EPS = 1e-6
SGU_CHUNK = 128
POOL_WINDOWS = (2, 4, 8, 16)
POOL_BUF = max(POOL_WINDOWS) - 1
CONV_W = 3

LANES = 128
SUBLANES = 8
ROW_TILE = 512
FF_CHUNK = 512
SCAN_PIECES = 4
VMEM_LIMIT = 56 * 1024 * 1024


def _dot(a, b):
    return jnp.dot(a, b, preferred_element_type=F32)


def _rms(x, g):
    return x * lax.rsqrt(jnp.mean(x * x, axis=-1, keepdims=True) + EPS) * g


def _round_up(n, m):
    return -(-n // m) * m


def _const_spec(shape):
    zeros = (0,) * len(shape)
    return pl.BlockSpec(shape, lambda *_: zeros, pipeline_mode=pl.Buffered(1))


def _layer_spec(stack, layer):
    zeros = (0,) * (stack.ndim - 1)
    return pl.BlockSpec((None,) + stack.shape[1:], lambda *_: (layer,) + zeros, pipeline_mode=pl.Buffered(1))


def _params(*semantics):
    return pltpu.CompilerParams(dimension_semantics=semantics, vmem_limit_bytes=VMEM_LIMIT)


def _poolconv_pieces(xc, z, bg, i, pbuf_ref, cbuf_ref, pw_ref, ps_ref, cw_ref, cb_ref,
                     yc_ref, yd_ref, pout_ref, cout_ref, fullc, fullz, diff, *, nb, prefix_valid):
    hp, hc = POOL_BUF * nb, (CONV_W - 1) * nb
    p0, c0 = pbuf_ref.shape[0], cbuf_ref.shape[0]
    r = xc.shape[0]
    cg = xc.shape[1] // len(POOL_WINDOWS)
    first = i == 0

    def load_tile():
        fullc[0:p0, :] = jnp.where(first, pbuf_ref[...], fullc[0:p0, :])
        fullz[0:c0, :] = jnp.where(first, cbuf_ref[...], fullz[0:c0, :])
        fullc[p0:p0 + r, :] = xc[...]
        fullz[c0:c0 + r, :] = z[...]

    def conv():
        acc = cb_ref[...]
        for k in range(CONV_W):
            back = (CONV_W - 1 - k) * nb
            acc = acc + fullz[c0 - back:c0 - back + r, :] * cw_ref[k:k + 1, :]
        yd_ref[...] = (bg[...] * acc).astype(yd_ref.dtype)

    def window_sum(rows, lanes, w):
        s = fullc[p0:p0 + rows, lanes]
        for k in range(1, w):
            s = s + fullc[p0 - k * nb:p0 - k * nb + rows, lanes]
        return s

    def pool(gi, w):
        lanes = slice(gi * cg, (gi + 1) * cg)
        diff[:, lanes] = window_sum(r, lanes, w) * (1.0 / w) - fullc[p0:p0 + r, lanes]
        if not prefix_valid:
            head = min(p0, r)
            t = lax.broadcasted_iota(jnp.int32, (head, 1), 0) // nb
            n = jnp.minimum(t + 1, w).astype(F32)
            short = window_sum(head, lanes, w) / n - fullc[p0:p0 + head, lanes]
            diff[0:head, lanes] = jnp.where(first, short, diff[0:head, lanes])

    def project():
        for gi in range(len(POOL_WINDOWS)):
            lanes = slice(gi * cg, (gi + 1) * cg)
            yc = _dot(diff[:, lanes].astype(BF16), pw_ref[gi]) * ps_ref[:, lanes]
            yc_ref[:, lanes] = yc.astype(yc_ref.dtype)

    def shift_history():
        pout_ref[...] = fullc[p0 + r - hp:p0 + r, :]
        cout_ref[...] = fullz[c0 + r - hc:c0 + r, :]
        new_pool = fullc[r:r + p0, :]
        new_conv = fullz[r:r + c0, :]
        fullc[0:p0, :] = new_pool
        fullz[0:c0, :] = new_conv

    pools = [functools.partial(pool, gi, w) for gi, w in enumerate(POOL_WINDOWS)]
    return [load_tile, conv] + pools + [project, shift_history]


def _trunk_kernel(*refs, has_mix, nxt, d_half, nb, prefix_valid, tiles_per_group, n_cast):
    refs = list(refs)
    h_ref = refs.pop(0)
    if has_mix:
        ya_ref, yb_ref, wo_ref, gf_ref, wg_ref, wu_ref, wd_ref = refs[:7]
        del refs[:7]
    gn_ref = refs.pop(0)
    if nxt != "final":
        win_ref = refs.pop(0)
    if nxt == "even":
        gs_ref, sw_ref, sb_ref = refs[:3]
        del refs[:3]
    if nxt == "odd":
        odd_in = refs[:6]
        del refs[:6]
    cast_in = refs[:n_cast]
    del refs[:n_cast]
    outs = refs
    for src, dst in zip(cast_in, outs[:n_cast]):
        dst[...] = src[...].astype(dst.dtype)
    del outs[:n_cast]
    s = pl.program_id(0)
    if nxt == "odd":
        odd_out = outs[-7:]

        @pl.when(s == 0)
        def _():
            for ref in odd_out[-3:]:
                ref[...] = jnp.zeros(ref.shape, ref.dtype)

    h = h_ref[...]
    if has_mix:
        h = h + _dot(ya_ref[...], wo_ref[:d_half, :]) + _dot(yb_ref[...], wo_ref[d_half:, :])
        hn = _rms(h, gf_ref[...]).astype(BF16)
        d_ff = wg_ref.shape[1]
        acc = None
        for c0 in range(0, d_ff, FF_CHUNK):
            c1 = min(c0 + FF_CHUNK, d_ff)
            g = _dot(hn, wg_ref[:, c0:c1])
            u = _dot(hn, wu_ref[:, c0:c1])
            a = (g * jax.nn.sigmoid(g) * u).astype(BF16)
            part = _dot(a, wd_ref[c0:c1, :])
            acc = part if acc is None else acc + part
        h = h + acc
        if nxt != "final":
            outs.pop(0)[...] = h
    hn = _rms(h, gn_ref[...])
    if nxt == "final":
        outs[0][...] = hn
        return
    proj = _dot(hn.astype(BF16), win_ref[...])
    if nxt == "even":
        ua_ref, yb_ref = outs[:2]
        ua_ref[...] = proj[:, :d_half]
        ub = proj[:, d_half:2 * d_half]
        vn = _rms(proj[:, 2 * d_half:], gs_ref[...])
        if nb == 1:
            _sgu_chunks(ub, vn, sw_ref, sb_ref, yb_ref)
        else:
            _sgu_short(ub, vn, sw_ref, sb_ref, yb_ref, nb=nb)
            outs[2][...] = vn
    else:
        xc = proj[:, :d_half]
        z = proj[:, 3 * d_half:] * proj[:, d_half:2 * d_half]
        bg = proj[:, 2 * d_half:3 * d_half]
        for piece in _poolconv_pieces(xc, z, bg, s % tiles_per_group, *odd_in, *odd_out, nb=nb,
                                      prefix_valid=prefix_valid):
            piece()


def _stream_rows(n_rows, n_steps):
    return next(rpb for rpb in range(2 * SUBLANES, n_rows + 1, 2 * SUBLANES)
                if n_rows % rpb == 0 and n_rows // rpb <= n_steps)


def _trunk_call(h, layer, W, *, mix, wts, nxt, nb, prefix_valid, sgu_prm, cast=(), odd_hist=None):
    n_grp, r, d = h.shape
    d_half = d // 2
    tile = min(ROW_TILE, r)
    i = layer // 2
    tpg = r // tile
    n_tiles = n_grp * tpg

    def rows(c):
        return pl.BlockSpec((None, tile, c), lambda s: (s // tpg, s % tpg, 0))

    def hist_spec(n_rows):
        return pl.BlockSpec((None, n_rows, d_half), lambda s: (s // tpg, 0, 0))

    args, specs = [h], [rows(d)]
    if mix is not None:
        ya, yb = mix
        w_out, gate, up, down = wts[:4]
        args += [ya, yb, w_out, W["norm_ffn"], gate, up, down]
        specs += [rows(d_half), rows(d_half), _const_spec(w_out.shape), _layer_spec(W["norm_ffn"], layer - 1)]
        specs += [_const_spec(w.shape) for w in (gate, up, down)]
    if nxt == "final":
        args.append(W["norm_final"])
        specs.append(_const_spec(W["norm_final"].shape))
    else:
        args += [W["norm_mix"], wts[-1]]
        specs += [_layer_spec(W["norm_mix"], layer), _const_spec(wts[-1].shape)]
    scratch = []
    if nxt == "even":
        sgu = [W["sgu_norm"], *sgu_prm]
        args += sgu
        specs += [_layer_spec(w, i) for w in sgu]
        assert (nb == 1 and tile % SGU_CHUNK == 0) or (n_tiles == 1 and tile // nb < SGU_CHUNK)
        out_shapes = [jax.ShapeDtypeStruct((n_grp, r, d_half), F32), jax.ShapeDtypeStruct((n_grp, r, d_half), BF16)]
        if nb > 1:
            out_shapes.append(jax.ShapeDtypeStruct((n_grp, r, d_half), F32))
        out_specs = [rows(d_half)] * len(out_shapes)
    elif nxt == "odd":
        hp, hc = POOL_BUF * nb, (CONV_W - 1) * nb
        p0, c0 = _round_up(hp, SUBLANES), _round_up(hc, SUBLANES)
        if odd_hist is None:
            pool_buf, conv_buf = jnp.zeros((n_grp, p0, d_half), F32), jnp.zeros((n_grp, c0, d_half), F32)
        else:
            pool_buf, conv_buf = odd_hist
            assert (p0, c0) == (hp, hc)
        odd_w = [W["pool_w"], W["pool_scale"], W["conv_w"], W["conv_b"]]
        args += [pool_buf, conv_buf] + odd_w
        specs += [hist_spec(p0), hist_spec(c0)] + [_layer_spec(w, i) for w in odd_w]
        scratch = [pltpu.VMEM((p0 + tile, d_half), F32), pltpu.VMEM((c0 + tile, d_half), F32),
                   pltpu.VMEM((tile, d_half), F32)]
        out_shapes = [jax.ShapeDtypeStruct((n_grp, r, d_half), BF16)] * 2 + [
            jax.ShapeDtypeStruct((n_grp, hp, d_half), F32), jax.ShapeDtypeStruct((n_grp, hc, d_half), F32)]
        out_specs = [rows(d_half)] * 2 + [hist_spec(hp), hist_spec(hc)]
    else:
        out_shapes, out_specs = [], []
    if mix is not None or nxt == "final":
        out_shapes = [jax.ShapeDtypeStruct((n_grp, r, d), F32)] + out_shapes
        out_specs = [rows(d)] + out_specs
    n_cast = len(cast)
    cast_shapes, cast_specs = [], []
    for w, idx in cast:
        n_rows, n_cols = w.shape[1:]
        rpb = _stream_rows(n_rows, n_tiles)
        n_blk = n_rows // rpb
        args.append(w)
        specs.append(pl.BlockSpec((None, rpb, n_cols),
                                  lambda s, n_blk=n_blk, idx=idx: (idx, s * n_blk // n_tiles, 0)))
        cast_shapes.append(jax.ShapeDtypeStruct((n_rows, n_cols), BF16))
        cast_specs.append(pl.BlockSpec((rpb, n_cols), lambda s, n_blk=n_blk: (s * n_blk // n_tiles, 0)))
    out_shapes, out_specs = cast_shapes + out_shapes, cast_specs + out_specs

    outs = pl.pallas_call(
        functools.partial(_trunk_kernel, has_mix=mix is not None, nxt=nxt, d_half=d_half, nb=nb,
                          prefix_valid=prefix_valid, tiles_per_group=tpg, n_cast=n_cast),
        grid=(n_tiles,),
        in_specs=specs,
        out_specs=out_specs,
        out_shape=out_shapes,
        scratch_shapes=scratch,
        compiler_params=_params("arbitrary"),
        name=f"trunk_{'mix' if mix is not None else 'in'}_{nxt}",
    )(*args)
    return outs[n_cast:], tuple(outs[:n_cast])


def _s5_prep_kernel(lr_ref, li_ref, ldt_ref, bre_ref, bim_ref, are_ref, aim_ref, bbre_ref, bbim_ref):
    lr = lr_ref[...]
    li = li_ref[...]
    dt = jnp.exp(ldt_ref[...])
    mag = jnp.exp(lr * dt)
    ang = li * dt
    ab_re = mag * jnp.cos(ang)
    ab_im = mag * jnp.sin(ang)
    den = lr * lr + li * li
    f_re = ((ab_re - 1.0) * lr + ab_im * li) / den
    f_im = (ab_im * lr - (ab_re - 1.0) * li) / den
    are_ref[...] = ab_re
    aim_ref[...] = ab_im
    for g in range(lr.shape[0]):
        fr = f_re[g:g + 1, :]
        fi = f_im[g:g + 1, :]
        br = bre_ref[g]
        bi = bim_ref[g]
        bbre_ref[g] = fr * br - fi * bi
        bbim_ref[g] = fr * bi + fi * br


def _s5_prep(lam_re, lam_im, log_dt, b_re, b_im):
    l, g, n = lam_re.shape
    p = b_re.shape[-1]
    bt_re = jnp.swapaxes(b_re, 2, 3).reshape(l * g, p, n)
    bt_im = jnp.swapaxes(b_im, 2, 3).reshape(l * g, p, n)
    a_re, a_im, bb_re, bb_im = pl.pallas_call(
        _s5_prep_kernel,
        out_shape=[jax.ShapeDtypeStruct((l * g, n), F32)] * 2 + [jax.ShapeDtypeStruct((l * g, p, n), F32)] * 2,
        name="s5_discretize",
    )(lam_re.reshape(l * g, n), lam_im.reshape(l * g, n), log_dt.reshape(l * g, 1), bt_re, bt_im)
    gs = LANES // p
    eye = jnp.eye(gs, dtype=F32)

    def blockdiag_in(bb):
        return jnp.einsum("lkgpn,gh->lkgphn", bb.reshape(l, g // gs, gs, p, n), eye).reshape(
            l, g // gs, gs * p, gs * n).astype(BF16)

    def sublane_rows(a):
        return jnp.broadcast_to(a.reshape(l, 1, g * n), (l, SUBLANES, g * n))

    return sublane_rows(a_re), sublane_rows(a_im), blockdiag_in(bb_re), blockdiag_in(bb_im)


def _blockdiag_out(c):
    l, g, p, n = c.shape
    gs = LANES // p
    eye = jnp.eye(gs, dtype=F32)
    return jnp.einsum("lkgpn,gh->lkgnhp", c.reshape(l, g // gs, gs, p, n), eye).reshape(
        l, g // gs, gs * n, gs * p).astype(BF16)


def _s5_kernel(u_ref, h0re_ref, h0im_ref, are_ref, aim_ref, bbre_ref, bbim_ref, cre_ref, cim_ref,
               d_ref, gw_ref, gb_ref, y_ref, hre_out, him_out, ut, yt, xre, xim, hre_s, him_s,
               *, n_grp, nb, tb):
    i = pl.program_id(0)
    n_seq = n_grp * nb

    @pl.when(i == 0)
    def _():
        hre_s[...] = h0re_ref[...]
        him_s[...] = h0im_ref[...]

    n_slab, slab_in, slab_st = bbre_ref.shape
    for k in range(n_slab):
        lanes = slice(k * slab_in, (k + 1) * slab_in)
        if n_grp == 1:
            ut[k] = u_ref[0, :, lanes]
        else:
            for b in range(n_grp):
                ut[k, pl.ds(b, tb, stride=n_grp), :] = u_ref[b, :, lanes]
    def project_in(k, x, bb_ref):
        x[:, k * slab_st:(k + 1) * slab_st] = _dot(ut[k].astype(BF16), bb_ref[k])

    def read_out(k, x, c_ref):
        return _dot(x[:, k * slab_st:(k + 1) * slab_st].astype(BF16), c_ref[k])

    def combine(k, y_re, y_im):
        return y_re - y_im + d_ref[:, k * slab_in:(k + 1) * slab_in] * ut[k]

    def scan_pieces(cols):
        items = [(r0, t) for r0 in range(0, n_seq, SUBLANES) for t in range(tb)]
        per = len(items) // SCAN_PIECES

        def run(chunk):
            hr = hi = cur = None
            for r0, t in chunk:
                if r0 != cur:
                    if cur is not None:
                        hre_s[cur:cur + SUBLANES, cols] = hr
                        him_s[cur:cur + SUBLANES, cols] = hi
                    hr, hi, cur = hre_s[r0:r0 + SUBLANES, cols], him_s[r0:r0 + SUBLANES, cols], r0
                ar = are_ref[:, cols]
                ai = aim_ref[:, cols]
                r = t * n_seq + r0
                nr = ar * hr - ai * hi + xre[r:r + SUBLANES, cols]
                ni = ar * hi + ai * hr + xim[r:r + SUBLANES, cols]
                xre[r:r + SUBLANES, cols] = nr
                xim[r:r + SUBLANES, cols] = ni
                hr, hi = nr, ni
            hre_s[cur:cur + SUBLANES, cols] = hr
            him_s[cur:cur + SUBLANES, cols] = hi

        return [functools.partial(run, items[p * per:(p + 1) * per]) for p in range(SCAN_PIECES)]

    project_in(0, xre, bbre_ref)
    project_in(0, xim, bbim_ref)
    y_first = []
    for k in range(n_slab):
        pieces = scan_pieces(slice(k * slab_st, (k + 1) * slab_st))
        for p, piece in enumerate(pieces):
            x, bb, c = ((xre, bbre_ref, cre_ref), (xim, bbim_ref, cim_ref))[p // (SCAN_PIECES // 2)]
            if p % (SCAN_PIECES // 2) == 0:
                if k + 1 < n_slab:
                    project_in(k + 1, x, bb)
                else:
                    y_first.append(read_out(0, x, c))
            piece()
    parts = [combine(0, *y_first)]
    parts += [combine(k, read_out(k, xre, cre_ref), read_out(k, xim, cim_ref)) for k in range(1, n_slab)]
    z = jax.nn.gelu(jnp.concatenate(parts, axis=-1))
    out = z * jax.nn.sigmoid(_dot(z.astype(BF16), gw_ref[...]) + gb_ref[...])
    for k in range(n_slab):
        lanes = slice(k * slab_in, (k + 1) * slab_in)
        if n_grp == 1:
            y_ref[0, :, lanes] = out[:, lanes].astype(y_ref.dtype)
        else:
            yt[k] = out[:, lanes]
            for b in range(n_grp):
                y_ref[b, :, lanes] = yt[k, pl.ds(b, tb, stride=n_grp), :].astype(y_ref.dtype)
    hre_out[...] = hre_s[...]
    him_out[...] = him_s[...]


def _s5_call(u, h0_re, h0_im, consts, layer, *, nb):
    n_grp, r, d_a = u.shape
    assert n_grp == 1 or nb == 1
    t = r // nb
    n_seq = n_grp * nb
    n_state = consts[0].shape[2]
    tb = max(1, min(t, ROW_TILE // n_seq))
    rows = tb * n_seq
    state_spec = pl.BlockSpec((n_seq, n_state), lambda i: (0, 0))
    blk = pl.BlockSpec((n_grp, tb * nb, d_a), lambda i: (0, i, 0))
    return pl.pallas_call(
        functools.partial(_s5_kernel, n_grp=n_grp, nb=nb, tb=tb),
        grid=(t // tb,),
        in_specs=[blk, state_spec, state_spec] + [_layer_spec(c, layer) for c in consts],
        out_specs=[blk, state_spec, state_spec],
        out_shape=[jax.ShapeDtypeStruct(u.shape, BF16),
                   jax.ShapeDtypeStruct((n_seq, n_state), F32),
                   jax.ShapeDtypeStruct((n_seq, n_state), F32)],
        scratch_shapes=[pltpu.VMEM((d_a // LANES, rows, LANES), F32), pltpu.VMEM((d_a // LANES, rows, LANES), F32),
                        pltpu.VMEM((rows, n_state), F32), pltpu.VMEM((rows, n_state), F32),
                        pltpu.VMEM((n_seq, n_state), F32), pltpu.VMEM((n_seq, n_state), F32)],
        compiler_params=_params("arbitrary"),
        name="s5_mixer",
    )(u, h0_re, h0_im, *consts)


def _sgu_chunks(u, vn, w_ref, bias_ref, y_ref):
    n_heads, cl, _ = w_ref.shape
    hd = vn.shape[1] // n_heads
    row = lax.broadcasted_iota(jnp.int32, (cl, cl), 0)
    col = lax.broadcasted_iota(jnp.int32, (cl, cl), 1)
    w = [jnp.where(col <= row, w_ref[h], 0.0).astype(BF16) for h in range(n_heads)]
    heads_per_slab = LANES // hd
    lane = lax.broadcasted_iota(jnp.int32, (cl, LANES), 1)
    for c in range(vn.shape[0] // cl):
        r = slice(c * cl, (c + 1) * cl)
        slabs = []
        for s in range(n_heads // heads_per_slab):
            v = vn[r, s * LANES:(s + 1) * LANES].astype(BF16)
            mixed = _dot(w[s * heads_per_slab], v)
            for j in range(1, heads_per_slab):
                mixed = jnp.where(lane >= j * hd, _dot(w[s * heads_per_slab + j], v), mixed)
            slabs.append(mixed)
        mixed = jnp.concatenate(slabs, axis=-1) + bias_ref[...]
        y_ref[r, :] = (u[r, :] * mixed).astype(y_ref.dtype)


def _sgu_short(u, vn, wl_ref, bias_ref, y_ref, *, nb):
    for i in range(wl_ref.shape[0]):
        mixed = bias_ref[i:i + 1, :]
        for j in range(i + 1):
            mixed = mixed + wl_ref[i, j:j + 1, :] * vn[j * nb:(j + 1) * nb, :]
        y_ref[i * nb:(i + 1) * nb, :] = (u[i * nb:(i + 1) * nb, :] * mixed).astype(y_ref.dtype)


def _sgu_params(w_s, b_s, d_b, t):
    hd = d_b // w_s.shape[1]
    cl = min(t, SGU_CHUNK)
    bias = jnp.repeat(jnp.swapaxes(b_s[:, :, :cl], 1, 2), hd, axis=2)
    if t % SGU_CHUNK == 0:
        return w_s, bias
    assert t < SGU_CHUNK
    return jnp.repeat(jnp.transpose(w_s[:, :, :cl, :cl], (0, 2, 3, 1)), hd, axis=3), bias


def _interleave(a):
    return jnp.swapaxes(a, 0, 1).reshape(1, a.shape[0] * a.shape[1], a.shape[2])


def _deinterleave(a, nb):
    return jnp.swapaxes(a.reshape(-1, nb, a.shape[2]), 0, 1)


def _run_trunk(x, s5_re, s5_im, hist_bufs, prefix_valid, W, *, interleaved, rounded=None):
    batch, t, d = x.shape
    depth = W["norm_mix"].shape[0]
    nb = batch if interleaved else 1
    pack = _interleave if interleaved else (lambda a: a)
    unpack = (lambda a: _deinterleave(a, nb)) if interleaved else (lambda a: a)
    common = dict(nb=nb, prefix_valid=prefix_valid, sgu_prm=_sgu_params(W["sgu_w"], W["sgu_b"], d // 2, t))

    def kind(layer):
        return "final" if layer == depth else ("even" if layer % 2 == 0 else "odd")

    def hist(layer):
        if kind(layer) != "odd" or hist_bufs is None:
            return None
        return tuple(pack(buf[layer // 2]) for buf in hist_bufs)

    def matrices(call):
        prev, need = call - 1, []
        if call > 0:
            need += [(W["w_out_even" if prev % 2 == 0 else "w_out_odd"], prev // 2),
                     (W["ffn_gate"], prev), (W["ffn_up"], prev), (W["ffn_down"], prev)]
        if call < depth:
            need.append((W["w_in_even" if call % 2 == 0 else "w_in_odd"], call // 2))
        return need

    convert = rounded is None
    if convert:
        rounded = [[w[idx].astype(BF16) for w, idx in matrices(0)]]

    def cast(call):
        return matrices(call + 1) if convert and call < depth else ()

    h = pack(x)
    mixer_in, nxt_w = _trunk_call(h, 0, W, mix=None, wts=rounded[0], nxt=kind(0), odd_hist=hist(0), cast=cast(0),
                                  **common)
    rounded += [nxt_w] if convert else []
    new_re, new_im, new_v, new_pool, new_conv = [], [], [], [], []
    for layer in range(depth):
        i = layer // 2
        if layer % 2 == 0:
            u_a, y_b = mixer_in[:2]
            y_a, hre, him = _s5_call(u_a, s5_re[i].reshape(batch, -1), s5_im[i].reshape(batch, -1), W["s5"], i,
                                     nb=nb)
            new_re.append(hre.reshape(s5_re[i].shape))
            new_im.append(him.reshape(s5_im[i].shape))
            if interleaved:
                new_v.append(unpack(mixer_in[2]))
            mix = (y_a, y_b)
        else:
            y_c, y_d, pout, cout = mixer_in
            new_pool.append(unpack(pout))
            new_conv.append(unpack(cout))
            mix = (y_c, y_d)
        outs, nxt_w = _trunk_call(h, layer + 1, W, mix=mix, wts=rounded[layer + 1], nxt=kind(layer + 1),
                                  odd_hist=hist(layer + 1), cast=cast(layer + 1), **common)
        rounded += [nxt_w] if nxt_w else []
        h, mixer_in = outs[0], outs[1:]
    return (unpack(h), jnp.stack(new_re), jnp.stack(new_im), jnp.stack(new_v) if new_v else None,
            jnp.stack(new_pool), jnp.stack(new_conv), rounded)


def kernel(x_prompt, x_sample, state_s5_re, state_s5_im, state_pool, state_conv, norm_mix, norm_ffn, norm_final, w_in_even, w_out_even, s5_lambda_re, s5_lambda_im, s5_log_dt, s5_b_re, s5_b_im, s5_c_re, s5_c_im, s5_d, s5_glu_w, s5_glu_b, sgu_norm, sgu_w, sgu_b, w_in_odd, w_out_odd, pool_w, pool_scale, conv_w, conv_b, ffn_w_gate, ffn_w_up, ffn_w_down):
    n_even, n_odd = w_in_even.shape[0], w_in_odd.shape[0]
    depth = norm_mix.shape[0]
    s5 = [*_s5_prep(s5_lambda_re, s5_lambda_im, s5_log_dt, s5_b_re, s5_b_im),
          _blockdiag_out(s5_c_re), _blockdiag_out(s5_c_im), s5_d.reshape(n_even, 1, -1),
          s5_glu_w.astype(BF16), s5_glu_b.reshape(n_even, 1, -1)]
    d = norm_mix.shape[1]
    W = dict(
        norm_mix=norm_mix.reshape(depth, 1, d), norm_ffn=norm_ffn.reshape(depth, 1, d),
        norm_final=norm_final.reshape(1, d),
        w_in_even=w_in_even, w_out_even=w_out_even, w_in_odd=w_in_odd, w_out_odd=w_out_odd,
        sgu_norm=sgu_norm.reshape(n_even, 1, -1), sgu_w=sgu_w, sgu_b=sgu_b, s5=s5,
        pool_w=pool_w.astype(BF16), pool_scale=pool_scale.reshape(n_odd, 1, -1), conv_w=conv_w,
        conv_b=conv_b.reshape(n_odd, 1, -1),
        ffn_gate=ffn_w_gate, ffn_up=ffn_w_up, ffn_down=ffn_w_down,
    )
    bp = x_prompt.shape[0]
    z_s5 = jnp.zeros((n_even, bp) + state_s5_re.shape[2:], state_s5_re.dtype)
    y_p, p_re, p_im, _, p_pool, p_conv, rounded = _run_trunk(x_prompt, z_s5, z_s5, None, False, W,
                                                             interleaved=False)
    y_s, s_re, s_im, s_v, s_pool, s_conv, _ = _run_trunk(
        x_sample, state_s5_re, state_s5_im, (state_pool, state_conv), True, W, interleaved=True, rounded=rounded)
    return (y_p, y_s, p_re, p_im, p_pool, p_conv, s_re, s_im, s_v, s_pool, s_conv)
```

```python
import functools

import jax
import jax.numpy as jnp
from jax import lax
from jax.experimental import pallas as pl
from jax.experimental.pallas import tpu as pltpu

F32 = jnp.float32
BF16 = jnp.bfloat16

EPS = 1e-6
SGU_CHUNK = 128
POOL_WINDOWS = (2, 4, 8, 16)
POOL_BUF = max(POOL_WINDOWS) - 1
CONV_W = 3

LANES = 128
SUBLANES = 8
ROW_TILE = 512
FF_CHUNK = 512
SCAN_PIECES = 4
VMEM_LIMIT = 56 * 1024 * 1024


def _dot(a, b):
    return jnp.dot(a, b, preferred_element_type=F32)


def _rms(x, g):
    return x * lax.rsqrt(jnp.mean(x * x, axis=-1, keepdims=True) + EPS) * g


def _round_up(n, m):
    return -(-n // m) * m


def _const_spec(shape):
    zeros = (0,) * len(shape)
    return pl.BlockSpec(shape, lambda *_: zeros, pipeline_mode=pl.Buffered(1))


def _layer_spec(stack, layer):
    zeros = (0,) * (stack.ndim - 1)
    return pl.BlockSpec((None,) + stack.shape[1:], lambda *_: (layer,) + zeros, pipeline_mode=pl.Buffered(1))


def _params(*semantics):
    return pltpu.CompilerParams(dimension_semantics=semantics, vmem_limit_bytes=VMEM_LIMIT)


def _poolconv_pieces(xc, z, bg, i, pbuf_ref, cbuf_ref, pw_ref, ps_ref, cw_ref, cb_ref,
                     yc_ref, yd_ref, pout_ref, cout_ref, fullc, fullz, diff, *, nb, prefix_valid):
    hp, hc = POOL_BUF * nb, (CONV_W - 1) * nb
    p0, c0 = pbuf_ref.shape[0], cbuf_ref.shape[0]
    r = xc.shape[0]
    cg = xc.shape[1] // len(POOL_WINDOWS)
    first = i == 0

    def load_tile():
        fullc[0:p0, :] = jnp.where(first, pbuf_ref[...], fullc[0:p0, :])
        fullz[0:c0, :] = jnp.where(first, cbuf_ref[...], fullz[0:c0, :])
        fullc[p0:p0 + r, :] = xc[...]
        fullz[c0:c0 + r, :] = z[...]

    def conv():
        acc = cb_ref[...]
        for k in range(CONV_W):
            back = (CONV_W - 1 - k) * nb
            acc = acc + fullz[c0 - back:c0 - back + r, :] * cw_ref[k:k + 1, :]
        yd_ref[...] = (bg[...] * acc).astype(yd_ref.dtype)

    def window_sum(rows, lanes, w):
        s = fullc[p0:p0 + rows, lanes]
        for k in range(1, w):
            s = s + fullc[p0 - k * nb:p0 - k * nb + rows, lanes]
        return s

    def pool(gi, w):
        lanes = slice(gi * cg, (gi + 1) * cg)
        diff[:, lanes] = window_sum(r, lanes, w) * (1.0 / w) - fullc[p0:p0 + r, lanes]
        if not prefix_valid:
            head = min(p0, r)
            t = lax.broadcasted_iota(jnp.int32, (head, 1), 0) // nb
            n = jnp.minimum(t + 1, w).astype(F32)
            short = window_sum(head, lanes, w) / n - fullc[p0:p0 + head, lanes]
            diff[0:head, lanes] = jnp.where(first, short, diff[0:head, lanes])

    def project():
        for gi in range(len(POOL_WINDOWS)):
            lanes = slice(gi * cg, (gi + 1) * cg)
            yc = _dot(diff[:, lanes].astype(BF16), pw_ref[gi]) * ps_ref[:, lanes]
            yc_ref[:, lanes] = yc.astype(yc_ref.dtype)

    def shift_history():
        pout_ref[...] = fullc[p0 + r - hp:p0 + r, :]
        cout_ref[...] = fullz[c0 + r - hc:c0 + r, :]
        new_pool = fullc[r:r + p0, :]
        new_conv = fullz[r:r + c0, :]
        fullc[0:p0, :] = new_pool
        fullz[0:c0, :] = new_conv

    pools = [functools.partial(pool, gi, w) for gi, w in enumerate(POOL_WINDOWS)]
    return [load_tile, conv] + pools + [project, shift_history]


def _trunk_kernel(*refs, has_mix, nxt, d_half, nb, prefix_valid, tiles_per_group, n_cast):
    refs = list(refs)
    h_ref = refs.pop(0)
    if has_mix:
        ya_ref, yb_ref, wo_ref, gf_ref, wg_ref, wu_ref, wd_ref = refs[:7]
        del refs[:7]
    gn_ref = refs.pop(0)
    if nxt != "final":
        win_ref = refs.pop(0)
    if nxt == "even":
        gs_ref, sw_ref, sb_ref = refs[:3]
        del refs[:3]
    if nxt == "odd":
        odd_in = refs[:6]
        del refs[:6]
    cast_in = refs[:n_cast]
    del refs[:n_cast]
    outs = refs
    _cast_blocks(cast_in, outs[:n_cast])
    del outs[:n_cast]
    s = pl.program_id(0)
    if nxt == "odd":
        odd_out = outs[-7:]

        @pl.when(s == 0)
        def _():
            for ref in odd_out[-3:]:
                ref[...] = jnp.zeros(ref.shape, ref.dtype)

    h = h_ref[...]
    if has_mix:
        h = h + _dot(ya_ref[...], wo_ref[:d_half, :]) + _dot(yb_ref[...], wo_ref[d_half:, :])
        hn = _rms(h, gf_ref[...]).astype(BF16)
        d_ff = wg_ref.shape[1]
        acc = None
        for c0 in range(0, d_ff, FF_CHUNK):
            c1 = min(c0 + FF_CHUNK, d_ff)
            g = _dot(hn, wg_ref[:, c0:c1])
            u = _dot(hn, wu_ref[:, c0:c1])
            a = (g * jax.nn.sigmoid(g) * u).astype(BF16)
            part = _dot(a, wd_ref[c0:c1, :])
            acc = part if acc is None else acc + part
        h = h + acc
        if nxt != "final":
            outs.pop(0)[...] = h
    hn = _rms(h, gn_ref[...])
    if nxt == "final":
        outs[0][...] = hn
        return
    proj = _dot(hn.astype(BF16), win_ref[...])
    if nxt == "even":
        ua_ref, yb_ref = outs[:2]
        ua_ref[...] = proj[:, :d_half]
        ub = proj[:, d_half:2 * d_half]
        vn = _rms(proj[:, 2 * d_half:], gs_ref[...])
        if nb == 1:
            _sgu_chunks(ub, vn, sw_ref, sb_ref, yb_ref)
        else:
            _sgu_short(ub, vn, sw_ref, sb_ref, yb_ref, nb=nb)
            outs[2][...] = vn
    else:
        xc = proj[:, :d_half]
        z = proj[:, 3 * d_half:] * proj[:, d_half:2 * d_half]
        bg = proj[:, 2 * d_half:3 * d_half]
        for piece in _poolconv_pieces(xc, z, bg, s % tiles_per_group, *odd_in, *odd_out, nb=nb,
                                      prefix_valid=prefix_valid):
            piece()


def _cast_stream(cast, n_steps):
    args, in_specs, out_shapes, out_specs = [], [], [], []
    for w, idx in cast:
        n_rows, n_cols = w.shape[1:]
        rpb = next(k for k in range(2 * SUBLANES, n_rows + 1, 2 * SUBLANES)
                   if n_rows % k == 0 and n_rows // k <= n_steps)
        n_blk = n_rows // rpb
        args.append(w)
        in_specs.append(pl.BlockSpec((None, rpb, n_cols),
                                     lambda s, n_blk=n_blk, idx=idx: (idx, s * n_blk // n_steps, 0)))
        out_shapes.append(jax.ShapeDtypeStruct((n_rows, n_cols), BF16))
        out_specs.append(pl.BlockSpec((rpb, n_cols), lambda s, n_blk=n_blk: (s * n_blk // n_steps, 0)))
    return args, in_specs, out_shapes, out_specs


def _cast_blocks(srcs, dsts):
    for src, dst in zip(srcs, dsts):
        dst[...] = src[...].astype(dst.dtype)


def _trunk_call(h, layer, W, *, mix, wts, nxt, nb, prefix_valid, sgu_prm, cast=(), odd_hist=None):
    n_grp, r, d = h.shape
    d_half = d // 2
    tile = min(ROW_TILE, r)
    i = layer // 2
    tpg = r // tile
    n_tiles = n_grp * tpg

    def rows(c):
        return pl.BlockSpec((None, tile, c), lambda s: (s // tpg, s % tpg, 0))

    def hist_spec(n_rows):
        return pl.BlockSpec((None, n_rows, d_half), lambda s: (s // tpg, 0, 0))

    args, specs = [h], [rows(d)]
    if mix is not None:
        ya, yb = mix
        w_out, gate, up, down = wts[:4]
        args += [ya, yb, w_out, W["norm_ffn"], gate, up, down]
        specs += [rows(d_half), rows(d_half), _const_spec(w_out.shape), _layer_spec(W["norm_ffn"], layer - 1)]
        specs += [_const_spec(w.shape) for w in (gate, up, down)]
    if nxt == "final":
        args.append(W["norm_final"])
        specs.append(_const_spec(W["norm_final"].shape))
    else:
        args += [W["norm_mix"], wts[-1]]
        specs += [_layer_spec(W["norm_mix"], layer), _const_spec(wts[-1].shape)]
    scratch = []
    if nxt == "even":
        sgu = [W["sgu_norm"], *sgu_prm]
        args += sgu
        specs += [_layer_spec(w, i) for w in sgu]
        assert (nb == 1 and tile % SGU_CHUNK == 0) or (n_tiles == 1 and tile // nb < SGU_CHUNK)
        out_shapes = [jax.ShapeDtypeStruct((n_grp, r, d_half), F32), jax.ShapeDtypeStruct((n_grp, r, d_half), BF16)]
        if nb > 1:
            out_shapes.append(jax.ShapeDtypeStruct((n_grp, r, d_half), F32))
        out_specs = [rows(d_half)] * len(out_shapes)
    elif nxt == "odd":
        hp, hc = POOL_BUF * nb, (CONV_W - 1) * nb
        p0, c0 = _round_up(hp, SUBLANES), _round_up(hc, SUBLANES)
        if odd_hist is None:
            pool_buf, conv_buf = jnp.zeros((n_grp, p0, d_half), F32), jnp.zeros((n_grp, c0, d_half), F32)
        else:
            pool_buf, conv_buf = odd_hist
            assert (p0, c0) == (hp, hc)
        odd_w = [W["pool_w"], W["pool_scale"], W["conv_w"], W["conv_b"]]
        args += [pool_buf, conv_buf] + odd_w
        specs += [hist_spec(p0), hist_spec(c0)] + [_layer_spec(w, i) for w in odd_w]
        scratch = [pltpu.VMEM((p0 + tile, d_half), F32), pltpu.VMEM((c0 + tile, d_half), F32),
                   pltpu.VMEM((tile, d_half), F32)]
        out_shapes = [jax.ShapeDtypeStruct((n_grp, r, d_half), BF16)] * 2 + [
            jax.ShapeDtypeStruct((n_grp, hp, d_half), F32), jax.ShapeDtypeStruct((n_grp, hc, d_half), F32)]
        out_specs = [rows(d_half)] * 2 + [hist_spec(hp), hist_spec(hc)]
    else:
        out_shapes, out_specs = [], []
    if mix is not None or nxt == "final":
        out_shapes = [jax.ShapeDtypeStruct((n_grp, r, d), F32)] + out_shapes
        out_specs = [rows(d)] + out_specs
    n_cast = len(cast)
    cast_args, cast_in_specs, cast_shapes, cast_specs = _cast_stream(cast, n_tiles)
    args, specs = args + cast_args, specs + cast_in_specs
    out_shapes, out_specs = cast_shapes + out_shapes, cast_specs + out_specs

    outs = pl.pallas_call(
        functools.partial(_trunk_kernel, has_mix=mix is not None, nxt=nxt, d_half=d_half, nb=nb,
                          prefix_valid=prefix_valid, tiles_per_group=tpg, n_cast=n_cast),
        grid=(n_tiles,),
        in_specs=specs,
        out_specs=out_specs,
        out_shape=out_shapes,
        scratch_shapes=scratch,
        compiler_params=_params("arbitrary"),
        name=f"trunk_{'mix' if mix is not None else 'in'}_{nxt}",
    )(*args)
    return outs[n_cast:], tuple(outs[:n_cast])


def _s5_prep_kernel(lr_ref, li_ref, ldt_ref, bre_ref, bim_ref, are_ref, aim_ref, bbre_ref, bbim_ref):
    lr = lr_ref[...]
    li = li_ref[...]
    dt = jnp.exp(ldt_ref[...])
    mag = jnp.exp(lr * dt)
    ang = li * dt
    ab_re = mag * jnp.cos(ang)
    ab_im = mag * jnp.sin(ang)
    den = lr * lr + li * li
    f_re = ((ab_re - 1.0) * lr + ab_im * li) / den
    f_im = (ab_im * lr - (ab_re - 1.0) * li) / den
    are_ref[...] = ab_re
    aim_ref[...] = ab_im
    for g in range(lr.shape[0]):
        fr = f_re[g:g + 1, :]
        fi = f_im[g:g + 1, :]
        br = bre_ref[g]
        bi = bim_ref[g]
        bbre_ref[g] = fr * br - fi * bi
        bbim_ref[g] = fr * bi + fi * br


def _s5_prep(lam_re, lam_im, log_dt, b_re, b_im):
    l, g, n = lam_re.shape
    p = b_re.shape[-1]
    bt_re = jnp.swapaxes(b_re, 2, 3).reshape(l * g, p, n)
    bt_im = jnp.swapaxes(b_im, 2, 3).reshape(l * g, p, n)
    a_re, a_im, bb_re, bb_im = pl.pallas_call(
        _s5_prep_kernel,
        out_shape=[jax.ShapeDtypeStruct((l * g, n), F32)] * 2 + [jax.ShapeDtypeStruct((l * g, p, n), F32)] * 2,
        name="s5_discretize",
    )(lam_re.reshape(l * g, n), lam_im.reshape(l * g, n), log_dt.reshape(l * g, 1), bt_re, bt_im)
    gs = LANES // p
    eye = jnp.eye(gs, dtype=F32)

    def blockdiag_in(bb):
        return jnp.einsum("lkgpn,gh->lkgphn", bb.reshape(l, g // gs, gs, p, n), eye).reshape(
            l, g // gs, gs * p, gs * n).astype(BF16)

    def sublane_rows(a):
        return jnp.broadcast_to(a.reshape(l, 1, g * n), (l, SUBLANES, g * n))

    return sublane_rows(a_re), sublane_rows(a_im), blockdiag_in(bb_re), blockdiag_in(bb_im)


def _blockdiag_out(c):
    l, g, p, n = c.shape
    gs = LANES // p
    eye = jnp.eye(gs, dtype=F32)
    return jnp.einsum("lkgpn,gh->lkgnhp", c.reshape(l, g // gs, gs, p, n), eye).reshape(
        l, g // gs, gs * n, gs * p).astype(BF16)


def _s5_kernel(*refs, n_grp, nb, tb, n_cast):
    n_in = 12
    (u_ref, h0re_ref, h0im_ref, are_ref, aim_ref, bbre_ref, bbim_ref, cre_ref, cim_ref,
     d_ref, gw_ref, gb_ref) = refs[:n_in]
    y_ref, hre_out, him_out = refs[n_in + n_cast:n_in + n_cast + 3]
    ut, yt, xre, xim, hre_s, him_s = refs[n_in + 2 * n_cast + 3:]
    _cast_blocks(refs[n_in:n_in + n_cast], refs[n_in + n_cast + 3:n_in + 2 * n_cast + 3])
    i = pl.program_id(0)
    n_seq = n_grp * nb

    @pl.when(i == 0)
    def _():
        hre_s[...] = h0re_ref[...]
        him_s[...] = h0im_ref[...]

    n_slab, slab_in, slab_st = bbre_ref.shape
    for k in range(n_slab):
        lanes = slice(k * slab_in, (k + 1) * slab_in)
        if n_grp == 1:
            ut[k] = u_ref[0, :, lanes]
        else:
            for b in range(n_grp):
                ut[k, pl.ds(b, tb, stride=n_grp), :] = u_ref[b, :, lanes]
    def project_in(k, x, bb_ref):
        x[:, k * slab_st:(k + 1) * slab_st] = _dot(ut[k].astype(BF16), bb_ref[k])

    def read_out(k, x, c_ref):
        return _dot(x[:, k * slab_st:(k + 1) * slab_st].astype(BF16), c_ref[k])

    def combine(k, y_re, y_im):
        return y_re - y_im + d_ref[:, k * slab_in:(k + 1) * slab_in] * ut[k]

    def scan_pieces(cols):
        items = [(r0, t) for r0 in range(0, n_seq, SUBLANES) for t in range(tb)]
        per = len(items) // SCAN_PIECES

        def run(chunk):
            hr = hi = cur = None
            for r0, t in chunk:
                if r0 != cur:
                    if cur is not None:
                        hre_s[cur:cur + SUBLANES, cols] = hr
                        him_s[cur:cur + SUBLANES, cols] = hi
                    hr, hi, cur = hre_s[r0:r0 + SUBLANES, cols], him_s[r0:r0 + SUBLANES, cols], r0
                ar = are_ref[:, cols]
                ai = aim_ref[:, cols]
                r = t * n_seq + r0
                nr = ar * hr - ai * hi + xre[r:r + SUBLANES, cols]
                ni = ar * hi + ai * hr + xim[r:r + SUBLANES, cols]
                xre[r:r + SUBLANES, cols] = nr
                xim[r:r + SUBLANES, cols] = ni
                hr, hi = nr, ni
            hre_s[cur:cur + SUBLANES, cols] = hr
            him_s[cur:cur + SUBLANES, cols] = hi

        return [functools.partial(run, items[p * per:(p + 1) * per]) for p in range(SCAN_PIECES)]

    project_in(0, xre, bbre_ref)
    project_in(0, xim, bbim_ref)
    y_first = []
    for k in range(n_slab):
        pieces = scan_pieces(slice(k * slab_st, (k + 1) * slab_st))
        for p, piece in enumerate(pieces):
            x, bb, c = ((xre, bbre_ref, cre_ref), (xim, bbim_ref, cim_ref))[p // (SCAN_PIECES // 2)]
            if p % (SCAN_PIECES // 2) == 0:
                if k + 1 < n_slab:
                    project_in(k + 1, x, bb)
                else:
                    y_first.append(read_out(0, x, c))
            piece()
    parts = [combine(0, *y_first)]
    parts += [combine(k, read_out(k, xre, cre_ref), read_out(k, xim, cim_ref)) for k in range(1, n_slab)]
    z = jax.nn.gelu(jnp.concatenate(parts, axis=-1))
    out = z * jax.nn.sigmoid(_dot(z.astype(BF16), gw_ref[...]) + gb_ref[...])
    for k in range(n_slab):
        lanes = slice(k * slab_in, (k + 1) * slab_in)
        if n_grp == 1:
            y_ref[0, :, lanes] = out[:, lanes].astype(y_ref.dtype)
        else:
            yt[k] = out[:, lanes]
            for b in range(n_grp):
                y_ref[b, :, lanes] = yt[k, pl.ds(b, tb, stride=n_grp), :].astype(y_ref.dtype)
    hre_out[...] = hre_s[...]
    him_out[...] = him_s[...]


def _s5_call(u, h0_re, h0_im, consts, layer, *, nb, cast=()):
    n_grp, r, d_a = u.shape
    assert n_grp == 1 or nb == 1
    t = r // nb
    n_seq = n_grp * nb
    n_state = consts[0].shape[2]
    tb = max(1, min(t, ROW_TILE // n_seq))
    rows = tb * n_seq
    state_spec = pl.BlockSpec((n_seq, n_state), lambda i: (0, 0))
    blk = pl.BlockSpec((n_grp, tb * nb, d_a), lambda i: (0, i, 0))
    cast_args, cast_in_specs, cast_shapes, cast_specs = _cast_stream(cast, t // tb)
    y, hre, him, *rounded = pl.pallas_call(
        functools.partial(_s5_kernel, n_grp=n_grp, nb=nb, tb=tb, n_cast=len(cast)),
        grid=(t // tb,),
        in_specs=[blk, state_spec, state_spec] + [_layer_spec(c, layer) for c in consts] + cast_in_specs,
        out_specs=[blk, state_spec, state_spec] + cast_specs,
        out_shape=[jax.ShapeDtypeStruct(u.shape, BF16),
                   jax.ShapeDtypeStruct((n_seq, n_state), F32),
                   jax.ShapeDtypeStruct((n_seq, n_state), F32)] + cast_shapes,
        scratch_shapes=[pltpu.VMEM((d_a // LANES, rows, LANES), F32), pltpu.VMEM((d_a // LANES, rows, LANES), F32),
                        pltpu.VMEM((rows, n_state), F32), pltpu.VMEM((rows, n_state), F32),
                        pltpu.VMEM((n_seq, n_state), F32), pltpu.VMEM((n_seq, n_state), F32)],
        compiler_params=_params("arbitrary"),
        name="s5_mixer",
    )(u, h0_re, h0_im, *consts, *cast_args)
    return y, hre, him, tuple(rounded)


def _sgu_chunks(u, vn, w_ref, bias_ref, y_ref):
    n_heads, cl, _ = w_ref.shape
    hd = vn.shape[1] // n_heads
    row = lax.broadcasted_iota(jnp.int32, (cl, cl), 0)
    col = lax.broadcasted_iota(jnp.int32, (cl, cl), 1)
    w = [jnp.where(col <= row, w_ref[h], 0.0).astype(BF16) for h in range(n_heads)]
    heads_per_slab = LANES // hd
    lane = lax.broadcasted_iota(jnp.int32, (cl, LANES), 1)
    for c in range(vn.shape[0] // cl):
        r = slice(c * cl, (c + 1) * cl)
        slabs = []
        for s in range(n_heads // heads_per_slab):
            v = vn[r, s * LANES:(s + 1) * LANES].astype(BF16)
            mixed = _dot(w[s * heads_per_slab], v)
            for j in range(1, heads_per_slab):
                mixed = jnp.where(lane >= j * hd, _dot(w[s * heads_per_slab + j], v), mixed)
            slabs.append(mixed)
        mixed = jnp.concatenate(slabs, axis=-1) + bias_ref[...]
        y_ref[r, :] = (u[r, :] * mixed).astype(y_ref.dtype)


def _sgu_short(u, vn, wl_ref, bias_ref, y_ref, *, nb):
    for i in range(wl_ref.shape[0]):
        mixed = bias_ref[i:i + 1, :]
        for j in range(i + 1):
            mixed = mixed + wl_ref[i, j:j + 1, :] * vn[j * nb:(j + 1) * nb, :]
        y_ref[i * nb:(i + 1) * nb, :] = (u[i * nb:(i + 1) * nb, :] * mixed).astype(y_ref.dtype)


def _sgu_params(w_s, b_s, d_b, t):
    hd = d_b // w_s.shape[1]
    cl = min(t, SGU_CHUNK)
    bias = jnp.repeat(jnp.swapaxes(b_s[:, :, :cl], 1, 2), hd, axis=2)
    if t % SGU_CHUNK == 0:
        return w_s, bias
    assert t < SGU_CHUNK
    return jnp.repeat(jnp.transpose(w_s[:, :, :cl, :cl], (0, 2, 3, 1)), hd, axis=3), bias


def _interleave(a):
    return jnp.swapaxes(a, 0, 1).reshape(1, a.shape[0] * a.shape[1], a.shape[2])


def _deinterleave(a, nb):
    return jnp.swapaxes(a.reshape(-1, nb, a.shape[2]), 0, 1)


def _run_trunk(x, s5_re, s5_im, hist_bufs, prefix_valid, W, *, interleaved, rounded=None):
    batch, t, d = x.shape
    depth = W["norm_mix"].shape[0]
    nb = batch if interleaved else 1
    pack = _interleave if interleaved else (lambda a: a)
    unpack = (lambda a: _deinterleave(a, nb)) if interleaved else (lambda a: a)
    common = dict(nb=nb, prefix_valid=prefix_valid, sgu_prm=_sgu_params(W["sgu_w"], W["sgu_b"], d // 2, t))

    def kind(layer):
        return "final" if layer == depth else ("even" if layer % 2 == 0 else "odd")

    def hist(layer):
        if kind(layer) != "odd" or hist_bufs is None:
            return None
        return tuple(pack(buf[layer // 2]) for buf in hist_bufs)

    def matrices(call):
        prev, need = call - 1, []
        if call > 0:
            need += [(W["w_out_even" if prev % 2 == 0 else "w_out_odd"], prev // 2),
                     (W["ffn_gate"], prev), (W["ffn_up"], prev), (W["ffn_down"], prev)]
        if call < depth:
            need.append((W["w_in_even" if call % 2 == 0 else "w_in_odd"], call // 2))
        return need

    convert = rounded is None
    if convert:
        rounded = [[w[idx].astype(BF16) for w, idx in matrices(0)]]

    s5_rounds_first = convert and kind(0) == "even"

    def cast(call):
        return matrices(call + 1) if convert and call < depth and not (call == 0 and s5_rounds_first) else ()

    h = pack(x)
    mixer_in, nxt_w = _trunk_call(h, 0, W, mix=None, wts=rounded[0], nxt=kind(0), odd_hist=hist(0), cast=cast(0),
                                  **common)
    rounded += [nxt_w] if nxt_w else []
    new_re, new_im, new_v, new_pool, new_conv = [], [], [], [], []
    for layer in range(depth):
        i = layer // 2
        if layer % 2 == 0:
            u_a, y_b = mixer_in[:2]
            y_a, hre, him, nxt_w = _s5_call(u_a, s5_re[i].reshape(batch, -1), s5_im[i].reshape(batch, -1), W["s5"],
                                            i, nb=nb, cast=matrices(1) if layer == 0 and s5_rounds_first else ())
            rounded += [nxt_w] if nxt_w else []
            new_re.append(hre.reshape(s5_re[i].shape))
            new_im.append(him.reshape(s5_im[i].shape))
            if interleaved:
                new_v.append(unpack(mixer_in[2]))
            mix = (y_a, y_b)
        else:
            y_c, y_d, pout, cout = mixer_in
            new_pool.append(unpack(pout))
            new_conv.append(unpack(cout))
            mix = (y_c, y_d)
        outs, nxt_w = _trunk_call(h, layer + 1, W, mix=mix, wts=rounded[layer + 1], nxt=kind(layer + 1),
                                  odd_hist=hist(layer + 1), cast=cast(layer + 1), **common)
        rounded += [nxt_w] if nxt_w else []
        h, mixer_in = outs[0], outs[1:]
    return (unpack(h), jnp.stack(new_re), jnp.stack(new_im), jnp.stack(new_v) if new_v else None,
            jnp.stack(new_pool), jnp.stack(new_conv), rounded)


def kernel(x_prompt, x_sample, state_s5_re, state_s5_im, state_pool, state_conv, norm_mix, norm_ffn, norm_final, w_in_even, w_out_even, s5_lambda_re, s5_lambda_im, s5_log_dt, s5_b_re, s5_b_im, s5_c_re, s5_c_im, s5_d, s5_glu_w, s5_glu_b, sgu_norm, sgu_w, sgu_b, w_in_odd, w_out_odd, pool_w, pool_scale, conv_w, conv_b, ffn_w_gate, ffn_w_up, ffn_w_down):
    n_even, n_odd = w_in_even.shape[0], w_in_odd.shape[0]
    depth = norm_mix.shape[0]
    s5 = [*_s5_prep(s5_lambda_re, s5_lambda_im, s5_log_dt, s5_b_re, s5_b_im),
          _blockdiag_out(s5_c_re), _blockdiag_out(s5_c_im), s5_d.reshape(n_even, 1, -1),
          s5_glu_w.astype(BF16), s5_glu_b.reshape(n_even, 1, -1)]
    d = norm_mix.shape[1]
    W = dict(
        norm_mix=norm_mix.reshape(depth, 1, d), norm_ffn=norm_ffn.reshape(depth, 1, d),
        norm_final=norm_final.reshape(1, d),
        w_in_even=w_in_even, w_out_even=w_out_even, w_in_odd=w_in_odd, w_out_odd=w_out_odd,
        sgu_norm=sgu_norm.reshape(n_even, 1, -1), sgu_w=sgu_w, sgu_b=sgu_b, s5=s5,
        pool_w=pool_w.astype(BF16), pool_scale=pool_scale.reshape(n_odd, 1, -1), conv_w=conv_w,
        conv_b=conv_b.reshape(n_odd, 1, -1),
        ffn_gate=ffn_w_gate, ffn_up=ffn_w_up, ffn_down=ffn_w_down,
    )
    bp = x_prompt.shape[0]
    z_s5 = jnp.zeros((n_even, bp) + state_s5_re.shape[2:], state_s5_re.dtype)
    y_p, p_re, p_im, _, p_pool, p_conv, rounded = _run_trunk(x_prompt, z_s5, z_s5, None, False, W,
                                                             interleaved=False)
    y_s, s_re, s_im, s_v, s_pool, s_conv, _ = _run_trunk(
        x_sample, state_s5_re, state_s5_im, (state_pool, state_conv), True, W, interleaved=True, rounded=rounded)
    return (y_p, y_s, p_re, p_im, p_pool, p_conv, s_re, s_im, s_v, s_pool, s_conv)
```

```python
import functools

import jax
import jax.numpy as jnp
from jax import lax
from jax.experimental import pallas as pl
from jax.experimental.pallas import tpu as pltpu

F32 = jnp.float32
BF16 = jnp.bfloat16

EPS = 1e-6
SGU_CHUNK = 128
POOL_WINDOWS = (2, 4, 8, 16)
POOL_BUF = max(POOL_WINDOWS) - 1
CONV_W = 3

LANES = 128
SUBLANES = 8
BF16_ROWS = 2 * SUBLANES
V7X_VMEM_BYTES = 64 * 1024 * 1024
ROW_TILE = 512
FF_CHUNK = 512
SCAN_PIECES = 4
VMEM_LIMIT = V7X_VMEM_BYTES * 7 // 8


def _dot(a, b):
    return jnp.dot(a, b, preferred_element_type=F32)


def _rms(x, g):
    return x * lax.rsqrt(jnp.mean(x * x, axis=-1, keepdims=True) + EPS) * g


def _round_up(n, m):
    return -(-n // m) * m


def _const_spec(shape):
    zeros = (0,) * len(shape)
    return pl.BlockSpec(shape, lambda *_: zeros, pipeline_mode=pl.Buffered(1))


def _layer_spec(stack, layer):
    zeros = (0,) * (stack.ndim - 1)
    return pl.BlockSpec((None,) + stack.shape[1:], lambda *_: (layer,) + zeros, pipeline_mode=pl.Buffered(1))


def _params(*semantics):
    return pltpu.CompilerParams(dimension_semantics=semantics, vmem_limit_bytes=VMEM_LIMIT)


def _poolconv_pieces(xc, z, bg, i, pbuf_ref, cbuf_ref, pw_ref, ps_ref, cw_ref, cb_ref,
                     yc_ref, yd_ref, pout_ref, cout_ref, fullc, fullz, diff, *, nb, prefix_valid):
    hp, hc = POOL_BUF * nb, (CONV_W - 1) * nb
    p0, c0 = pbuf_ref.shape[0], cbuf_ref.shape[0]
    r = xc.shape[0]
    cg = xc.shape[1] // len(POOL_WINDOWS)
    first = i == 0

    def load_tile():
        fullc[0:p0, :] = jnp.where(first, pbuf_ref[...], fullc[0:p0, :])
        fullz[0:c0, :] = jnp.where(first, cbuf_ref[...], fullz[0:c0, :])
        fullc[p0:p0 + r, :] = xc[...]
        fullz[c0:c0 + r, :] = z[...]

    def conv():
        acc = cb_ref[...]
        for k in range(CONV_W):
            back = (CONV_W - 1 - k) * nb
            acc = acc + fullz[c0 - back:c0 - back + r, :] * cw_ref[k:k + 1, :]
        yd_ref[...] = (bg[...] * acc).astype(yd_ref.dtype)

    def window_sum(rows, lanes, w):
        s = fullc[p0:p0 + rows, lanes]
        for k in range(1, w):
            s = s + fullc[p0 - k * nb:p0 - k * nb + rows, lanes]
        return s

    def pool(gi, w):
        lanes = slice(gi * cg, (gi + 1) * cg)
        diff[:, lanes] = window_sum(r, lanes, w) * (1.0 / w) - fullc[p0:p0 + r, lanes]
        if not prefix_valid:
            head = min(p0, r)
            t = lax.broadcasted_iota(jnp.int32, (head, 1), 0) // nb
            n = jnp.minimum(t + 1, w).astype(F32)
            short = window_sum(head, lanes, w) / n - fullc[p0:p0 + head, lanes]
            diff[0:head, lanes] = jnp.where(first, short, diff[0:head, lanes])

    def project():
        for gi in range(len(POOL_WINDOWS)):
            lanes = slice(gi * cg, (gi + 1) * cg)
            yc = _dot(diff[:, lanes].astype(BF16), pw_ref[gi]) * ps_ref[:, lanes]
            yc_ref[:, lanes] = yc.astype(yc_ref.dtype)

    def shift_history():
        pout_ref[...] = fullc[p0 + r - hp:p0 + r, :]
        cout_ref[...] = fullz[c0 + r - hc:c0 + r, :]
        new_pool = fullc[r:r + p0, :]
        new_conv = fullz[r:r + c0, :]
        fullc[0:p0, :] = new_pool
        fullz[0:c0, :] = new_conv

    pools = [functools.partial(pool, gi, w) for gi, w in enumerate(POOL_WINDOWS)]
    return [load_tile, conv] + pools + [project, shift_history]


def _trunk_kernel(*refs, has_mix, nxt, d_half, nb, prefix_valid, tiles_per_group, n_cast):
    refs = list(refs)
    h_ref = refs.pop(0)
    if has_mix:
        ya_ref, yb_ref, wo_ref, gf_ref, wg_ref, wu_ref, wd_ref = refs[:7]
        del refs[:7]
    gn_ref = refs.pop(0)
    if nxt != "final":
        win_ref = refs.pop(0)
    if nxt == "even":
        gs_ref, sw_ref, sb_ref = refs[:3]
        del refs[:3]
    if nxt == "odd":
        odd_in = refs[:6]
        del refs[:6]
    cast_in = refs[:n_cast]
    del refs[:n_cast]
    outs = refs
    _cast_blocks(cast_in, outs[:n_cast])
    del outs[:n_cast]
    s = pl.program_id(0)
    if nxt == "odd":
        odd_out = outs[-7:]

        @pl.when(s == 0)
        def _():
            for ref in odd_out[-3:]:
                ref[...] = jnp.zeros(ref.shape, ref.dtype)

    h = h_ref[...]
    if has_mix:
        h = h + _dot(ya_ref[...], wo_ref[:d_half, :]) + _dot(yb_ref[...], wo_ref[d_half:, :])
        hn = _rms(h, gf_ref[...]).astype(BF16)
        d_ff = wg_ref.shape[1]
        acc = None
        for c0 in range(0, d_ff, FF_CHUNK):
            c1 = min(c0 + FF_CHUNK, d_ff)
            g = _dot(hn, wg_ref[:, c0:c1])
            u = _dot(hn, wu_ref[:, c0:c1])
            a = (g * jax.nn.sigmoid(g) * u).astype(BF16)
            part = _dot(a, wd_ref[c0:c1, :])
            acc = part if acc is None else acc + part
        h = h + acc
        if nxt != "final":
            outs.pop(0)[...] = h
    hn = _rms(h, gn_ref[...])
    if nxt == "final":
        outs[0][...] = hn
        return
    proj = _dot(hn.astype(BF16), win_ref[...])
    if nxt == "even":
        ua_ref, yb_ref = outs[:2]
        ua_ref[...] = proj[:, :d_half]
        ub = proj[:, d_half:2 * d_half]
        vn = _rms(proj[:, 2 * d_half:], gs_ref[...])
        if nb == 1:
            _sgu_chunks(ub, vn, sw_ref, sb_ref, yb_ref)
        else:
            _sgu_short(ub, vn, sw_ref, sb_ref, yb_ref, nb=nb)
            outs[2][...] = vn
    else:
        xc = proj[:, :d_half]
        z = proj[:, 3 * d_half:] * proj[:, d_half:2 * d_half]
        bg = proj[:, 2 * d_half:3 * d_half]
        for piece in _poolconv_pieces(xc, z, bg, s % tiles_per_group, *odd_in, *odd_out, nb=nb,
                                      prefix_valid=prefix_valid):
            piece()


def _cast_stream(cast, n_steps):
    args, in_specs, out_shapes, out_specs = [], [], [], []
    for w, idx in cast:
        n_rows, n_cols = w.shape[1:]
        rpb = next(k for k in range(BF16_ROWS, n_rows + 1, BF16_ROWS)
                   if n_rows % k == 0 and n_rows // k <= n_steps)
        n_blk = n_rows // rpb
        args.append(w)
        in_specs.append(pl.BlockSpec((None, rpb, n_cols),
                                     lambda s, n_blk=n_blk, idx=idx: (idx, s * n_blk // n_steps, 0)))
        out_shapes.append(jax.ShapeDtypeStruct((n_rows, n_cols), BF16))
        out_specs.append(pl.BlockSpec((rpb, n_cols), lambda s, n_blk=n_blk: (s * n_blk // n_steps, 0)))
    return args, in_specs, out_shapes, out_specs


def _cast_blocks(srcs, dsts):
    for src, dst in zip(srcs, dsts):
        dst[...] = src[...].astype(dst.dtype)


def _trunk_call(h, layer, W, *, mix, wts, nxt, nb, prefix_valid, sgu_prm, cast=(), odd_hist=None):
    n_grp, r, d = h.shape
    d_half = d // 2
    tile = min(ROW_TILE if mix is not None else 2 * ROW_TILE, r)
    i = layer // 2
    tpg = r // tile
    n_tiles = n_grp * tpg

    def rows(c):
        return pl.BlockSpec((None, tile, c), lambda s: (s // tpg, s % tpg, 0))

    def hist_spec(n_rows):
        return pl.BlockSpec((None, n_rows, d_half), lambda s: (s // tpg, 0, 0))

    args, specs = [h], [rows(d)]
    if mix is not None:
        ya, yb = mix
        w_out, gate, up, down = wts[:4]
        args += [ya, yb, w_out, W["norm_ffn"], gate, up, down]
        specs += [rows(d_half), rows(d_half), _const_spec(w_out.shape), _layer_spec(W["norm_ffn"], layer - 1)]
        specs += [_const_spec(w.shape) for w in (gate, up, down)]
    if nxt == "final":
        args.append(W["norm_final"])
        specs.append(_const_spec(W["norm_final"].shape))
    else:
        args += [W["norm_mix"], wts[-1]]
        specs += [_layer_spec(W["norm_mix"], layer), _const_spec(wts[-1].shape)]
    scratch = []
    if nxt == "even":
        sgu = [W["sgu_norm"], *sgu_prm]
        args += sgu
        specs += [_layer_spec(w, i) for w in sgu]
        assert (nb == 1 and tile % SGU_CHUNK == 0) or (n_tiles == 1 and tile // nb < SGU_CHUNK)
        out_shapes = [jax.ShapeDtypeStruct((n_grp, r, d_half), F32), jax.ShapeDtypeStruct((n_grp, r, d_half), BF16)]
        if nb > 1:
            out_shapes.append(jax.ShapeDtypeStruct((n_grp, r, d_half), F32))
        out_specs = [rows(d_half)] * len(out_shapes)
    elif nxt == "odd":
        hp, hc = POOL_BUF * nb, (CONV_W - 1) * nb
        p0, c0 = _round_up(hp, SUBLANES), _round_up(hc, SUBLANES)
        if odd_hist is None:
            pool_buf, conv_buf = jnp.zeros((n_grp, p0, d_half), F32), jnp.zeros((n_grp, c0, d_half), F32)
        else:
            pool_buf, conv_buf = odd_hist
            assert (p0, c0) == (hp, hc)
        odd_w = [W["pool_w"], W["pool_scale"], W["conv_w"], W["conv_b"]]
        args += [pool_buf, conv_buf] + odd_w
        specs += [hist_spec(p0), hist_spec(c0)] + [_layer_spec(w, i) for w in odd_w]
        scratch = [pltpu.VMEM((p0 + tile, d_half), F32), pltpu.VMEM((c0 + tile, d_half), F32),
                   pltpu.VMEM((tile, d_half), F32)]
        out_shapes = [jax.ShapeDtypeStruct((n_grp, r, d_half), BF16)] * 2 + [
            jax.ShapeDtypeStruct((n_grp, hp, d_half), F32), jax.ShapeDtypeStruct((n_grp, hc, d_half), F32)]
        out_specs = [rows(d_half)] * 2 + [hist_spec(hp), hist_spec(hc)]
    else:
        out_shapes, out_specs = [], []
    if mix is not None or nxt == "final":
        out_shapes = [jax.ShapeDtypeStruct((n_grp, r, d), F32)] + out_shapes
        out_specs = [rows(d)] + out_specs
    n_cast = len(cast)
    cast_args, cast_in_specs, cast_shapes, cast_specs = _cast_stream(cast, n_tiles)
    args, specs = args + cast_args, specs + cast_in_specs
    out_shapes, out_specs = cast_shapes + out_shapes, cast_specs + out_specs

    outs = pl.pallas_call(
        functools.partial(_trunk_kernel, has_mix=mix is not None, nxt=nxt, d_half=d_half, nb=nb,
                          prefix_valid=prefix_valid, tiles_per_group=tpg, n_cast=n_cast),
        grid=(n_tiles,),
        in_specs=specs,
        out_specs=out_specs,
        out_shape=out_shapes,
        scratch_shapes=scratch,
        compiler_params=_params("arbitrary"),
        name=f"trunk_{'mix' if mix is not None else 'in'}_{nxt}",
    )(*args)
    return outs[n_cast:], tuple(outs[:n_cast])


def _s5_prep_kernel(lr_ref, li_ref, ldt_ref, bre_ref, bim_ref, are_ref, aim_ref, bbre_ref, bbim_ref):
    lr = lr_ref[...]
    li = li_ref[...]
    dt = jnp.exp(ldt_ref[...])
    mag = jnp.exp(lr * dt)
    ang = li * dt
    ab_re = mag * jnp.cos(ang)
    ab_im = mag * jnp.sin(ang)
    den = lr * lr + li * li
    f_re = ((ab_re - 1.0) * lr + ab_im * li) / den
    f_im = (ab_im * lr - (ab_re - 1.0) * li) / den
    are_ref[...] = ab_re
    aim_ref[...] = ab_im
    for g in range(lr.shape[0]):
        fr = f_re[g:g + 1, :]
        fi = f_im[g:g + 1, :]
        br = bre_ref[g]
        bi = bim_ref[g]
        bbre_ref[g] = fr * br - fi * bi
        bbim_ref[g] = fr * bi + fi * br


def _s5_prep(lam_re, lam_im, log_dt, b_re, b_im):
    l, g, n = lam_re.shape
    p = b_re.shape[-1]
    bt_re = jnp.swapaxes(b_re, 2, 3).reshape(l * g, p, n)
    bt_im = jnp.swapaxes(b_im, 2, 3).reshape(l * g, p, n)
    a_re, a_im, bb_re, bb_im = pl.pallas_call(
        _s5_prep_kernel,
        out_shape=[jax.ShapeDtypeStruct((l * g, n), F32)] * 2 + [jax.ShapeDtypeStruct((l * g, p, n), F32)] * 2,
        name="s5_discretize",
    )(lam_re.reshape(l * g, n), lam_im.reshape(l * g, n), log_dt.reshape(l * g, 1), bt_re, bt_im)
    gs = LANES // p
    eye = jnp.eye(gs, dtype=F32)

    def blockdiag_in(bb):
        return jnp.einsum("lkgpn,gh->lkgphn", bb.reshape(l, g // gs, gs, p, n), eye).reshape(
            l, g // gs, gs * p, gs * n).astype(BF16)

    def sublane_rows(a):
        return jnp.broadcast_to(a.reshape(l, 1, g * n), (l, SUBLANES, g * n))

    return sublane_rows(a_re), sublane_rows(a_im), blockdiag_in(bb_re), blockdiag_in(bb_im)


def _blockdiag_out(c):
    l, g, p, n = c.shape
    gs = LANES // p
    eye = jnp.eye(gs, dtype=F32)
    return jnp.einsum("lkgpn,gh->lkgnhp", c.reshape(l, g // gs, gs, p, n), eye).reshape(
        l, g // gs, gs * n, gs * p).astype(BF16)


def _s5_kernel(*refs, n_grp, nb, tb, n_cast):
    n_in = 12
    (u_ref, h0re_ref, h0im_ref, are_ref, aim_ref, bbre_ref, bbim_ref, cre_ref, cim_ref,
     d_ref, gw_ref, gb_ref) = refs[:n_in]
    y_ref, hre_out, him_out = refs[n_in + n_cast:n_in + n_cast + 3]
    ut, yt, xre, xim, hre_s, him_s = refs[n_in + 2 * n_cast + 3:]
    _cast_blocks(refs[n_in:n_in + n_cast], refs[n_in + n_cast + 3:n_in + 2 * n_cast + 3])
    i = pl.program_id(0)
    n_seq = n_grp * nb

    @pl.when(i == 0)
    def _():
        hre_s[...] = h0re_ref[...]
        him_s[...] = h0im_ref[...]

    n_slab, slab_in, slab_st = bbre_ref.shape
    for k in range(n_slab):
        lanes = slice(k * slab_in, (k + 1) * slab_in)
        if n_grp == 1:
            ut[k] = u_ref[0, :, lanes]
        else:
            for b in range(n_grp):
                ut[k, pl.ds(b, tb, stride=n_grp), :] = u_ref[b, :, lanes]
    def project_in(k, x, bb_ref):
        x[:, k * slab_st:(k + 1) * slab_st] = _dot(ut[k].astype(BF16), bb_ref[k])

    def read_out(k, x, c_ref, rows):
        return _dot(x[rows, k * slab_st:(k + 1) * slab_st].astype(BF16), c_ref[k])

    def combine(k, y_re, y_im, rows):
        return y_re - y_im + d_ref[:, k * slab_in:(k + 1) * slab_in] * ut[k, rows, :]

    def scan_pieces(cols, t0, t1):
        items = [(r0, t) for r0 in range(0, n_seq, SUBLANES) for t in range(t0, t1)]
        per = len(items) // SCAN_PIECES

        def run(chunk):
            hr = hi = cur = None
            for r0, t in chunk:
                if r0 != cur:
                    if cur is not None:
                        hre_s[cur:cur + SUBLANES, cols] = hr
                        him_s[cur:cur + SUBLANES, cols] = hi
                    hr, hi, cur = hre_s[r0:r0 + SUBLANES, cols], him_s[r0:r0 + SUBLANES, cols], r0
                ar = are_ref[:, cols]
                ai = aim_ref[:, cols]
                r = t * n_seq + r0
                nr = ar * hr - ai * hi + xre[r:r + SUBLANES, cols]
                ni = ar * hi + ai * hr + xim[r:r + SUBLANES, cols]
                xre[r:r + SUBLANES, cols] = nr
                xim[r:r + SUBLANES, cols] = ni
                hr, hi = nr, ni
            hre_s[cur:cur + SUBLANES, cols] = hr
            him_s[cur:cur + SUBLANES, cols] = hi

        return [functools.partial(run, items[p * per:(p + 1) * per]) for p in range(SCAN_PIECES)]

    rows = slice(0, tb * n_seq)
    project_in(0, xre, bbre_ref)
    project_in(0, xim, bbim_ref)
    y_first = []
    for k in range(n_slab):
        pieces = scan_pieces(slice(k * slab_st, (k + 1) * slab_st), 0, tb)
        for p, piece in enumerate(pieces):
            x, bb, c = ((xre, bbre_ref, cre_ref), (xim, bbim_ref, cim_ref))[p // (SCAN_PIECES // 2)]
            if p % (SCAN_PIECES // 2) == 0:
                if k + 1 < n_slab:
                    project_in(k + 1, x, bb)
                else:
                    y_first.append(read_out(0, x, c, rows))
            piece()
    parts = [combine(0, *y_first, rows)]
    parts += [combine(k, read_out(k, xre, cre_ref, rows), read_out(k, xim, cim_ref, rows), rows)
              for k in range(1, n_slab)]
    z = jax.nn.gelu(jnp.concatenate(parts, axis=-1))
    out = z * jax.nn.sigmoid(_dot(z.astype(BF16), gw_ref[...]) + gb_ref[...])
    for k in range(n_slab):
        lanes = slice(k * slab_in, (k + 1) * slab_in)
        if n_grp == 1:
            y_ref[0, :, lanes] = out[:, lanes].astype(y_ref.dtype)
        else:
            yt[k] = out[:, lanes]
            for b in range(n_grp):
                y_ref[b, :, lanes] = yt[k, pl.ds(b, tb, stride=n_grp), :].astype(y_ref.dtype)
    hre_out[...] = hre_s[...]
    him_out[...] = him_s[...]


def _s5_call(u, h0_re, h0_im, consts, layer, *, nb, cast=()):
    n_grp, r, d_a = u.shape
    assert n_grp == 1 or nb == 1
    t = r // nb
    n_seq = n_grp * nb
    n_state = consts[0].shape[2]
    tb = max(1, min(t, ROW_TILE // n_seq))
    rows = tb * n_seq
    state_spec = pl.BlockSpec((n_seq, n_state), lambda i: (0, 0))
    blk = pl.BlockSpec((n_grp, tb * nb, d_a), lambda i: (0, i, 0))
    cast_args, cast_in_specs, cast_shapes, cast_specs = _cast_stream(cast, t // tb)
    y, hre, him, *rounded = pl.pallas_call(
        functools.partial(_s5_kernel, n_grp=n_grp, nb=nb, tb=tb, n_cast=len(cast)),
        grid=(t // tb,),
        in_specs=[blk, state_spec, state_spec] + [_layer_spec(c, layer) for c in consts] + cast_in_specs,
        out_specs=[blk, state_spec, state_spec] + cast_specs,
        out_shape=[jax.ShapeDtypeStruct(u.shape, BF16),
                   jax.ShapeDtypeStruct((n_seq, n_state), F32),
                   jax.ShapeDtypeStruct((n_seq, n_state), F32)] + cast_shapes,
        scratch_shapes=[pltpu.VMEM((d_a // LANES, rows, LANES), F32), pltpu.VMEM((d_a // LANES, rows, LANES), F32),
                        pltpu.VMEM((rows, n_state), F32), pltpu.VMEM((rows, n_state), F32),
                        pltpu.VMEM((n_seq, n_state), F32), pltpu.VMEM((n_seq, n_state), F32)],
        compiler_params=_params("arbitrary"),
        name="s5_mixer",
    )(u, h0_re, h0_im, *consts, *cast_args)
    return y, hre, him, tuple(rounded)


def _sgu_chunks(u, vn, w_ref, bias_ref, y_ref):
    n_heads, cl, _ = w_ref.shape
    hd = vn.shape[1] // n_heads
    row = lax.broadcasted_iota(jnp.int32, (cl, cl), 0)
    col = lax.broadcasted_iota(jnp.int32, (cl, cl), 1)
    w = [jnp.where(col <= row, w_ref[h], 0.0).astype(BF16) for h in range(n_heads)]
    heads_per_slab = LANES // hd
    lane = lax.broadcasted_iota(jnp.int32, (cl, LANES), 1)
    for c in range(vn.shape[0] // cl):
        r = slice(c * cl, (c + 1) * cl)
        slabs = []
        for s in range(n_heads // heads_per_slab):
            v = vn[r, s * LANES:(s + 1) * LANES].astype(BF16)
            mixed = _dot(w[s * heads_per_slab], v)
            for j in range(1, heads_per_slab):
                mixed = jnp.where(lane >= j * hd, _dot(w[s * heads_per_slab + j], v), mixed)
            slabs.append(mixed)
        mixed = jnp.concatenate(slabs, axis=-1) + bias_ref[...]
        y_ref[r, :] = (u[r, :] * mixed).astype(y_ref.dtype)


def _sgu_short(u, vn, wl_ref, bias_ref, y_ref, *, nb):
    for i in range(wl_ref.shape[0]):
        mixed = bias_ref[i:i + 1, :]
        for j in range(i + 1):
            mixed = mixed + wl_ref[i, j:j + 1, :] * vn[j * nb:(j + 1) * nb, :]
        y_ref[i * nb:(i + 1) * nb, :] = (u[i * nb:(i + 1) * nb, :] * mixed).astype(y_ref.dtype)


def _sgu_params(w_s, b_s, d_b, t):
    hd = d_b // w_s.shape[1]
    cl = min(t, SGU_CHUNK)
    bias = jnp.repeat(jnp.swapaxes(b_s[:, :, :cl], 1, 2), hd, axis=2)
    if t % SGU_CHUNK == 0:
        return w_s, bias
    assert t < SGU_CHUNK
    return jnp.repeat(jnp.transpose(w_s[:, :, :cl, :cl], (0, 2, 3, 1)), hd, axis=3), bias


def _interleave(a):
    return jnp.swapaxes(a, 0, 1).reshape(1, a.shape[0] * a.shape[1], a.shape[2])


def _deinterleave(a, nb):
    return jnp.swapaxes(a.reshape(-1, nb, a.shape[2]), 0, 1)


def _run_trunk(x, s5_re, s5_im, hist_bufs, prefix_valid, W, *, interleaved, rounded=None):
    batch, t, d = x.shape
    depth = W["norm_mix"].shape[0]
    nb = batch if interleaved else 1
    pack = _interleave if interleaved else (lambda a: a)
    unpack = (lambda a: _deinterleave(a, nb)) if interleaved else (lambda a: a)
    common = dict(nb=nb, prefix_valid=prefix_valid, sgu_prm=_sgu_params(W["sgu_w"], W["sgu_b"], d // 2, t))

    def kind(layer):
        return "final" if layer == depth else ("even" if layer % 2 == 0 else "odd")

    def hist(layer):
        if kind(layer) != "odd" or hist_bufs is None:
            return None
        return tuple(pack(buf[layer // 2]) for buf in hist_bufs)

    def matrices(call):
        prev, need = call - 1, []
        if call > 0:
            need += [(W["w_out_even" if prev % 2 == 0 else "w_out_odd"], prev // 2),
                     (W["ffn_gate"], prev), (W["ffn_up"], prev), (W["ffn_down"], prev)]
        if call < depth:
            need.append((W["w_in_even" if call % 2 == 0 else "w_in_odd"], call // 2))
        return need

    convert = rounded is None
    if convert:
        rounded = [[w[idx].astype(BF16) for w, idx in matrices(0)]]

    s5_rounds_first = convert and kind(0) == "even"

    def cast(call):
        return matrices(call + 1) if convert and call < depth and not (call == 0 and s5_rounds_first) else ()

    h = pack(x)
    mixer_in, nxt_w = _trunk_call(h, 0, W, mix=None, wts=rounded[0], nxt=kind(0), odd_hist=hist(0), cast=cast(0),
                                  **common)
    rounded += [nxt_w] if nxt_w else []
    new_re, new_im, new_v, new_pool, new_conv = [], [], [], [], []
    for layer in range(depth):
        i = layer // 2
        if layer % 2 == 0:
            u_a, y_b = mixer_in[:2]
            y_a, hre, him, nxt_w = _s5_call(u_a, s5_re[i].reshape(batch, -1), s5_im[i].reshape(batch, -1), W["s5"],
                                            i, nb=nb, cast=matrices(1) if layer == 0 and s5_rounds_first else ())
            rounded += [nxt_w] if nxt_w else []
            new_re.append(hre.reshape(s5_re[i].shape))
            new_im.append(him.reshape(s5_im[i].shape))
            if interleaved:
                new_v.append(unpack(mixer_in[2]))
            mix = (y_a, y_b)
        else:
            y_c, y_d, pout, cout = mixer_in
            new_pool.append(unpack(pout))
            new_conv.append(unpack(cout))
            mix = (y_c, y_d)
        outs, nxt_w = _trunk_call(h, layer + 1, W, mix=mix, wts=rounded[layer + 1], nxt=kind(layer + 1),
                                  odd_hist=hist(layer + 1), cast=cast(layer + 1), **common)
        rounded += [nxt_w] if nxt_w else []
        h, mixer_in = outs[0], outs[1:]
    return (unpack(h), jnp.stack(new_re), jnp.stack(new_im), jnp.stack(new_v) if new_v else None,
            jnp.stack(new_pool), jnp.stack(new_conv), rounded)


def kernel(x_prompt, x_sample, state_s5_re, state_s5_im, state_pool, state_conv, norm_mix, norm_ffn, norm_final, w_in_even, w_out_even, s5_lambda_re, s5_lambda_im, s5_log_dt, s5_b_re, s5_b_im, s5_c_re, s5_c_im, s5_d, s5_glu_w, s5_glu_b, sgu_norm, sgu_w, sgu_b, w_in_odd, w_out_odd, pool_w, pool_scale, conv_w, conv_b, ffn_w_gate, ffn_w_up, ffn_w_down):
    n_even, n_odd = w_in_even.shape[0], w_in_odd.shape[0]
    depth = norm_mix.shape[0]
    s5 = [*_s5_prep(s5_lambda_re, s5_lambda_im, s5_log_dt, s5_b_re, s5_b_im),
          _blockdiag_out(s5_c_re), _blockdiag_out(s5_c_im), s5_d.reshape(n_even, 1, -1),
          s5_glu_w.astype(BF16), s5_glu_b.reshape(n_even, 1, -1)]
    d = norm_mix.shape[1]
    W = dict(
        norm_mix=norm_mix.reshape(depth, 1, d), norm_ffn=norm_ffn.reshape(depth, 1, d),
        norm_final=norm_final.reshape(1, d),
        w_in_even=w_in_even, w_out_even=w_out_even, w_in_odd=w_in_odd, w_out_odd=w_out_odd,
        sgu_norm=sgu_norm.reshape(n_even, 1, -1), sgu_w=sgu_w, sgu_b=sgu_b, s5=s5,
        pool_w=pool_w.astype(BF16), pool_scale=pool_scale.reshape(n_odd, 1, -1), conv_w=conv_w,
        conv_b=conv_b.reshape(n_odd, 1, -1),
        ffn_gate=ffn_w_gate, ffn_up=ffn_w_up, ffn_down=ffn_w_down,
    )
    bp = x_prompt.shape[0]
    z_s5 = jnp.zeros((n_even, bp) + state_s5_re.shape[2:], state_s5_re.dtype)
    y_p, p_re, p_im, _, p_pool, p_conv, rounded = _run_trunk(x_prompt, z_s5, z_s5, None, False, W,
                                                             interleaved=False)
    y_s, s_re, s_im, s_v, s_pool, s_conv, _ = _run_trunk(
        x_sample, state_s5_re, state_s5_im, (state_pool, state_conv), True, W, interleaved=True, rounded=rounded)
    return (y_p, y_s, p_re, p_im, p_pool, p_conv, s_re, s_im, s_v, s_pool, s_conv)
```

```python
import functools

import jax
import jax.numpy as jnp
from jax import lax
from jax.experimental import pallas as pl
from jax.experimental.pallas import tpu as pltpu

F32 = jnp.float32
BF16 = jnp.bfloat16

EPS = 1e-6
SGU_CHUNK = 128
POOL_WINDOWS = (2, 4, 8, 16)
POOL_BUF = max(POOL_WINDOWS) - 1
CONV_W = 3

LANES = 128
SUBLANES = 8
BF16_ROWS = 2 * SUBLANES
V7X_VMEM_BYTES = 64 * 1024 * 1024
ROW_TILE = 512
FF_CHUNK = 512
SCAN_PIECES = 4
VMEM_LIMIT = V7X_VMEM_BYTES * 7 // 8


def _dot(a, b):
    return jnp.dot(a, b, preferred_element_type=F32)


def _rms(x, g):
    return x * lax.rsqrt(jnp.mean(x * x, axis=-1, keepdims=True) + EPS) * g


def _round_up(n, m):
    return -(-n // m) * m


def _const_spec(shape):
    zeros = (0,) * len(shape)
    return pl.BlockSpec(shape, lambda *_: zeros, pipeline_mode=pl.Buffered(1))


def _layer_spec(stack, layer):
    zeros = (0,) * (stack.ndim - 1)
    return pl.BlockSpec((None,) + stack.shape[1:], lambda *_: (layer,) + zeros, pipeline_mode=pl.Buffered(1))


def _params(*semantics):
    return pltpu.CompilerParams(dimension_semantics=semantics, vmem_limit_bytes=VMEM_LIMIT)


def _poolconv_pieces(xc, z, bg, i, pbuf_ref, cbuf_ref, pw_ref, ps_ref, cw_ref, cb_ref,
                     yc_ref, yd_ref, pout_ref, cout_ref, fullc, fullz, diff, *, nb, prefix_valid):
    hp, hc = POOL_BUF * nb, (CONV_W - 1) * nb
    p0, c0 = pbuf_ref.shape[0], cbuf_ref.shape[0]
    r = xc.shape[0]
    cg = xc.shape[1] // len(POOL_WINDOWS)
    first = i == 0

    def load_tile():
        fullc[0:p0, :] = jnp.where(first, pbuf_ref[...], fullc[0:p0, :])
        fullz[0:c0, :] = jnp.where(first, cbuf_ref[...], fullz[0:c0, :])
        fullc[p0:p0 + r, :] = xc[...]
        fullz[c0:c0 + r, :] = z[...]

    def conv():
        acc = cb_ref[...]
        for k in range(CONV_W):
            back = (CONV_W - 1 - k) * nb
            acc = acc + fullz[c0 - back:c0 - back + r, :] * cw_ref[k:k + 1, :]
        yd_ref[...] = (bg[...] * acc).astype(yd_ref.dtype)

    def window_sum(rows, lanes, w):
        s = fullc[p0:p0 + rows, lanes]
        for k in range(1, w):
            s = s + fullc[p0 - k * nb:p0 - k * nb + rows, lanes]
        return s

    def pool(gi, w):
        lanes = slice(gi * cg, (gi + 1) * cg)
        diff[:, lanes] = window_sum(r, lanes, w) * (1.0 / w) - fullc[p0:p0 + r, lanes]
        if not prefix_valid:
            head = min(p0, r)
            t = lax.broadcasted_iota(jnp.int32, (head, 1), 0) // nb
            n = jnp.minimum(t + 1, w).astype(F32)
            short = window_sum(head, lanes, w) / n - fullc[p0:p0 + head, lanes]
            diff[0:head, lanes] = jnp.where(first, short, diff[0:head, lanes])

    def project():
        for gi in range(len(POOL_WINDOWS)):
            lanes = slice(gi * cg, (gi + 1) * cg)
            yc = _dot(diff[:, lanes].astype(BF16), pw_ref[gi]) * ps_ref[:, lanes]
            yc_ref[:, lanes] = yc.astype(yc_ref.dtype)

    def shift_history():
        pout_ref[...] = fullc[p0 + r - hp:p0 + r, :]
        cout_ref[...] = fullz[c0 + r - hc:c0 + r, :]
        new_pool = fullc[r:r + p0, :]
        new_conv = fullz[r:r + c0, :]
        fullc[0:p0, :] = new_pool
        fullz[0:c0, :] = new_conv

    pools = [functools.partial(pool, gi, w) for gi, w in enumerate(POOL_WINDOWS)]
    return [load_tile, conv] + pools + [project, shift_history]


def _trunk_kernel(*refs, has_mix, nxt, d_half, nb, prefix_valid, tiles_per_group, n_cast):
    refs = list(refs)
    h_ref = refs.pop(0)
    if has_mix:
        ya_ref, yb_ref, wo_ref, gf_ref, wg_ref, wu_ref, wd_ref = refs[:7]
        del refs[:7]
    gn_ref = refs.pop(0)
    if nxt != "final":
        win_ref = refs.pop(0)
    if nxt == "even":
        gs_ref, sw_ref, sb_ref = refs[:3]
        del refs[:3]
    if nxt == "odd":
        odd_in = refs[:6]
        del refs[:6]
    cast_in = refs[:n_cast]
    del refs[:n_cast]
    outs = refs
    _cast_blocks(cast_in, outs[:n_cast])
    del outs[:n_cast]
    s = pl.program_id(0)
    if nxt == "odd":
        odd_out = outs[-7:]

        @pl.when(s == 0)
        def _():
            for ref in odd_out[-3:]:
                ref[...] = jnp.zeros(ref.shape, ref.dtype)

    h = h_ref[...]
    if has_mix:
        h = h + _dot(ya_ref[...], wo_ref[:d_half, :]) + _dot(yb_ref[...], wo_ref[d_half:, :])
        hn = _rms(h, gf_ref[...]).astype(BF16)
        d_ff = wg_ref.shape[1]
        acc = None
        for c0 in range(0, d_ff, FF_CHUNK):
            c1 = min(c0 + FF_CHUNK, d_ff)
            g = _dot(hn, wg_ref[:, c0:c1])
            u = _dot(hn, wu_ref[:, c0:c1])
            a = (g * jax.nn.sigmoid(g) * u).astype(BF16)
            part = _dot(a, wd_ref[c0:c1, :])
            acc = part if acc is None else acc + part
        h = h + acc
        if nxt != "final":
            outs.pop(0)[...] = h
    hn = _rms(h, gn_ref[...])
    if nxt == "final":
        outs[0][...] = hn
        return
    proj = _dot(hn.astype(BF16), win_ref[...])
    if nxt == "even":
        ua_ref, yb_ref = outs[:2]
        ua_ref[...] = proj[:, :d_half]
        ub = proj[:, d_half:2 * d_half]
        vn = _rms(proj[:, 2 * d_half:], gs_ref[...])
        if nb == 1:
            _sgu_chunks(ub, vn, sw_ref, sb_ref, yb_ref)
        else:
            _sgu_short(ub, vn, sw_ref, sb_ref, yb_ref, nb=nb)
            outs[2][...] = vn
    else:
        xc = proj[:, :d_half]
        z = proj[:, 3 * d_half:] * proj[:, d_half:2 * d_half]
        bg = proj[:, 2 * d_half:3 * d_half]
        for piece in _poolconv_pieces(xc, z, bg, s % tiles_per_group, *odd_in, *odd_out, nb=nb,
                                      prefix_valid=prefix_valid):
            piece()


def _cast_stream(cast, n_steps):
    args, in_specs, out_shapes, out_specs = [], [], [], []
    for w, idx in cast:
        n_rows, n_cols = w.shape[1:]
        rpb = next(k for k in range(BF16_ROWS, n_rows + 1, BF16_ROWS)
                   if n_rows % k == 0 and n_rows // k <= n_steps)
        n_blk = n_rows // rpb
        args.append(w)
        in_specs.append(pl.BlockSpec((None, rpb, n_cols),
                                     lambda s, n_blk=n_blk, idx=idx: (idx, s * n_blk // n_steps, 0)))
        out_shapes.append(jax.ShapeDtypeStruct((n_rows, n_cols), BF16))
        out_specs.append(pl.BlockSpec((rpb, n_cols), lambda s, n_blk=n_blk: (s * n_blk // n_steps, 0)))
    return args, in_specs, out_shapes, out_specs


def _cast_blocks(srcs, dsts):
    for src, dst in zip(srcs, dsts):
        dst[...] = src[...].astype(dst.dtype)


def _trunk_call(h, layer, W, *, mix, wts, nxt, nb, prefix_valid, sgu_prm, cast=(), odd_hist=None):
    n_grp, r, d = h.shape
    d_half = d // 2
    tile = min(ROW_TILE if mix is not None else 2 * ROW_TILE, r)
    i = layer // 2
    tpg = r // tile
    n_tiles = n_grp * tpg

    def rows(c):
        return pl.BlockSpec((None, tile, c), lambda s: (s // tpg, s % tpg, 0))

    def hist_spec(n_rows):
        return pl.BlockSpec((None, n_rows, d_half), lambda s: (s // tpg, 0, 0))

    args, specs = [h], [rows(d)]
    if mix is not None:
        ya, yb = mix
        w_out, gate, up, down = wts[:4]
        args += [ya, yb, w_out, W["norm_ffn"], gate, up, down]
        specs += [rows(d_half), rows(d_half), _const_spec(w_out.shape), _layer_spec(W["norm_ffn"], layer - 1)]
        specs += [_const_spec(w.shape) for w in (gate, up, down)]
    if nxt == "final":
        args.append(W["norm_final"])
        specs.append(_const_spec(W["norm_final"].shape))
    else:
        args += [W["norm_mix"], wts[-1]]
        specs += [_layer_spec(W["norm_mix"], layer), _const_spec(wts[-1].shape)]
    scratch = []
    if nxt == "even":
        sgu = [W["sgu_norm"], *sgu_prm]
        args += sgu
        specs += [_layer_spec(w, i) for w in sgu]
        assert (nb == 1 and tile % SGU_CHUNK == 0) or (n_tiles == 1 and tile // nb < SGU_CHUNK)
        out_shapes = [jax.ShapeDtypeStruct((n_grp, r, d_half), F32), jax.ShapeDtypeStruct((n_grp, r, d_half), BF16)]
        if nb > 1:
            out_shapes.append(jax.ShapeDtypeStruct((n_grp, r, d_half), F32))
        out_specs = [rows(d_half)] * len(out_shapes)
    elif nxt == "odd":
        hp, hc = POOL_BUF * nb, (CONV_W - 1) * nb
        p0, c0 = _round_up(hp, SUBLANES), _round_up(hc, SUBLANES)
        if odd_hist is None:
            pool_buf, conv_buf = jnp.zeros((n_grp, p0, d_half), F32), jnp.zeros((n_grp, c0, d_half), F32)
        else:
            pool_buf, conv_buf = odd_hist
            assert (p0, c0) == (hp, hc)
        odd_w = [W["pool_w"], W["pool_scale"], W["conv_w"], W["conv_b"]]
        args += [pool_buf, conv_buf] + odd_w
        specs += [hist_spec(p0), hist_spec(c0)] + [_layer_spec(w, i) for w in odd_w]
        scratch = [pltpu.VMEM((p0 + tile, d_half), F32), pltpu.VMEM((c0 + tile, d_half), F32),
                   pltpu.VMEM((tile, d_half), F32)]
        out_shapes = [jax.ShapeDtypeStruct((n_grp, r, d_half), BF16)] * 2 + [
            jax.ShapeDtypeStruct((n_grp, hp, d_half), F32), jax.ShapeDtypeStruct((n_grp, hc, d_half), F32)]
        out_specs = [rows(d_half)] * 2 + [hist_spec(hp), hist_spec(hc)]
    else:
        out_shapes, out_specs = [], []
    if mix is not None or nxt == "final":
        out_shapes = [jax.ShapeDtypeStruct((n_grp, r, d), F32)] + out_shapes
        out_specs = [rows(d)] + out_specs
    n_cast = len(cast)
    cast_args, cast_in_specs, cast_shapes, cast_specs = _cast_stream(cast, n_tiles)
    args, specs = args + cast_args, specs + cast_in_specs
    out_shapes, out_specs = cast_shapes + out_shapes, cast_specs + out_specs

    outs = pl.pallas_call(
        functools.partial(_trunk_kernel, has_mix=mix is not None, nxt=nxt, d_half=d_half, nb=nb,
                          prefix_valid=prefix_valid, tiles_per_group=tpg, n_cast=n_cast),
        grid=(n_tiles,),
        in_specs=specs,
        out_specs=out_specs,
        out_shape=out_shapes,
        scratch_shapes=scratch,
        compiler_params=_params("arbitrary"),
        name=f"trunk_{'mix' if mix is not None else 'in'}_{nxt}",
    )(*args)
    return outs[n_cast:], tuple(outs[:n_cast])


def _s5_prep_kernel(lr_ref, li_ref, ldt_ref, bre_ref, bim_ref, are_ref, aim_ref, bbre_ref, bbim_ref):
    lr = lr_ref[...]
    li = li_ref[...]
    dt = jnp.exp(ldt_ref[...])
    mag = jnp.exp(lr * dt)
    ang = li * dt
    ab_re = mag * jnp.cos(ang)
    ab_im = mag * jnp.sin(ang)
    den = lr * lr + li * li
    f_re = ((ab_re - 1.0) * lr + ab_im * li) / den
    f_im = (ab_im * lr - (ab_re - 1.0) * li) / den
    are_ref[...] = ab_re
    aim_ref[...] = ab_im
    for g in range(lr.shape[0]):
        fr = f_re[g:g + 1, :]
        fi = f_im[g:g + 1, :]
        br = bre_ref[g]
        bi = bim_ref[g]
        bbre_ref[g] = fr * br - fi * bi
        bbim_ref[g] = fr * bi + fi * br


def _s5_prep(lam_re, lam_im, log_dt, b_re, b_im):
    l, g, n = lam_re.shape
    p = b_re.shape[-1]
    bt_re = jnp.swapaxes(b_re, 2, 3).reshape(l * g, p, n)
    bt_im = jnp.swapaxes(b_im, 2, 3).reshape(l * g, p, n)
    a_re, a_im, bb_re, bb_im = pl.pallas_call(
        _s5_prep_kernel,
        out_shape=[jax.ShapeDtypeStruct((l * g, n), F32)] * 2 + [jax.ShapeDtypeStruct((l * g, p, n), F32)] * 2,
        name="s5_discretize",
    )(lam_re.reshape(l * g, n), lam_im.reshape(l * g, n), log_dt.reshape(l * g, 1), bt_re, bt_im)
    gs = LANES // p
    eye = jnp.eye(gs, dtype=F32)

    def blockdiag_in(bb):
        return jnp.einsum("lkgpn,gh->lkgphn", bb.reshape(l, g // gs, gs, p, n), eye).reshape(
            l, g // gs, gs * p, gs * n).astype(BF16)

    def sublane_rows(a):
        return jnp.broadcast_to(a.reshape(l, 1, g * n), (l, SUBLANES, g * n))

    return sublane_rows(a_re), sublane_rows(a_im), blockdiag_in(bb_re), blockdiag_in(bb_im)


def _blockdiag_out(c):
    l, g, p, n = c.shape
    gs = LANES // p
    eye = jnp.eye(gs, dtype=F32)
    return jnp.einsum("lkgpn,gh->lkgnhp", c.reshape(l, g // gs, gs, p, n), eye).reshape(
        l, g // gs, gs * n, gs * p).astype(BF16)


def _s5_pieces(u_ref, prm, scratch, store_y, *, n_grp, nb, tb):
    are_ref, aim_ref, bbre_ref, bbim_ref, cre_ref, cim_ref, d_ref, gw_ref, gb_ref = prm
    ut, xre, xim, hre_s, him_s = scratch
    yt = ut

    def ring(k):
        return slice(k % 2 * slab_st, (k % 2 + 1) * slab_st)
    n_seq = n_grp * nb
    n_slab, slab_in, slab_st = bbre_ref.shape

    def load_u():
        for k in range(n_slab):
            lanes = slice(k * slab_in, (k + 1) * slab_in)
            if n_grp == 1:
                ut[k] = u_ref[0, :, lanes]
            else:
                for b in range(n_grp):
                    ut[k, pl.ds(b, tb, stride=n_grp), :] = u_ref[b, :, lanes]

    def project_in(k, x, bb_ref):
        x[:, ring(k)] = _dot(ut[k].astype(BF16), bb_ref[k])

    def read_out(k, x, c_ref, rows):
        return _dot(x[rows, ring(k)].astype(BF16), c_ref[k])

    def combine(k, y_re, y_im, rows):
        return y_re - y_im + d_ref[:, k * slab_in:(k + 1) * slab_in] * ut[k, rows, :]

    def scan_pieces(k):
        cols, xcols = slice(k * slab_st, (k + 1) * slab_st), ring(k)
        items = [(r0, t) for r0 in range(0, n_seq, SUBLANES) for t in range(tb)]
        per = len(items) // SCAN_PIECES

        def run(chunk):
            hr = hi = cur = None
            for r0, t in chunk:
                if r0 != cur:
                    if cur is not None:
                        hre_s[cur:cur + SUBLANES, cols] = hr
                        him_s[cur:cur + SUBLANES, cols] = hi
                    hr, hi, cur = hre_s[r0:r0 + SUBLANES, cols], him_s[r0:r0 + SUBLANES, cols], r0
                ar = are_ref[:, cols]
                ai = aim_ref[:, cols]
                r = t * n_seq + r0
                nr = ar * hr - ai * hi + xre[r:r + SUBLANES, xcols]
                ni = ar * hi + ai * hr + xim[r:r + SUBLANES, xcols]
                xre[r:r + SUBLANES, xcols] = nr
                xim[r:r + SUBLANES, xcols] = ni
                hr, hi = nr, ni
            hre_s[cur:cur + SUBLANES, cols] = hr
            him_s[cur:cur + SUBLANES, cols] = hi

        return [functools.partial(run, items[p * per:(p + 1) * per]) for p in range(SCAN_PIECES)]

    rows = slice(0, tb * n_seq)
    re_im = ((xre, bbre_ref, cre_ref), (xim, bbim_ref, cim_ref))
    y_parts, box = {}, {}

    def read_out_part(k, x, c):
        y_parts.setdefault(k, []).append(read_out(k, x, c, rows))

    def gate():
        y_slabs = [combine(k, *y_parts[k], rows) for k in range(n_slab)]
        z = jax.nn.gelu(jnp.concatenate(y_slabs, axis=-1))
        box["out"] = z * jax.nn.sigmoid(_dot(z.astype(BF16), gw_ref[...]) + gb_ref[...])

    def write(k):
        lanes = slice(k * slab_in, (k + 1) * slab_in)
        if n_grp == 1:
            store_y(0, lanes, box["out"][:, lanes].astype(BF16))
        else:
            yt[k] = box["out"][:, lanes]
            for b in range(n_grp):
                store_y(b, lanes, yt[k, pl.ds(b, tb, stride=n_grp), :].astype(BF16))

    pieces = [load_u] + [functools.partial(project_in, 0, x, bb) for x, bb, _ in re_im]
    for k in range(n_slab):
        matmuls = [functools.partial(read_out_part, k - 1, x, c) for x, _, c in re_im] if k > 0 else []
        matmuls += [functools.partial(project_in, k + 1, x, bb) for x, bb, _ in re_im] if k + 1 < n_slab else []
        scans = scan_pieces(k)
        for p, piece in enumerate(scans):
            pieces += matmuls[p * len(matmuls) // len(scans):(p + 1) * len(matmuls) // len(scans)] + [piece]
    pieces += [functools.partial(read_out_part, n_slab - 1, x, c) for x, _, c in re_im] + [gate]
    pieces += [functools.partial(write, k) for k in range(n_slab)]
    return pieces


def _s5_kernel(*refs, n_grp, nb, tb, n_cast):
    n_in = 12
    u_ref, h0re_ref, h0im_ref = refs[:3]
    y_ref, hre_out, him_out = refs[n_in + n_cast:n_in + n_cast + 3]
    scratch = refs[n_in + 2 * n_cast + 3:]
    hre_s, him_s = scratch[-2:]
    _cast_blocks(refs[n_in:n_in + n_cast], refs[n_in + n_cast + 3:n_in + 2 * n_cast + 3])

    @pl.when(pl.program_id(0) == 0)
    def _():
        hre_s[...] = h0re_ref[...]
        him_s[...] = h0im_ref[...]

    def store_y(b, lanes, y):
        y_ref[b, :, lanes] = y

    for piece in _s5_pieces(u_ref, refs[3:n_in], scratch, store_y, n_grp=n_grp, nb=nb, tb=tb):
        piece()
    hre_out[...] = hre_s[...]
    him_out[...] = him_s[...]


def _s5_call(u, h0_re, h0_im, consts, layer, *, nb, cast=()):
    n_grp, r, d_a = u.shape
    assert n_grp == 1 or nb == 1
    t = r // nb
    n_seq = n_grp * nb
    n_state, slab_st = consts[0].shape[2], consts[2].shape[3]
    tb = max(1, min(t, ROW_TILE // n_seq))
    rows = tb * n_seq
    state_spec = pl.BlockSpec((n_seq, n_state), lambda i: (0, 0))
    blk = pl.BlockSpec((n_grp, tb * nb, d_a), lambda i: (0, i, 0))
    cast_args, cast_in_specs, cast_shapes, cast_specs = _cast_stream(cast, t // tb)
    y, hre, him, *rounded = pl.pallas_call(
        functools.partial(_s5_kernel, n_grp=n_grp, nb=nb, tb=tb, n_cast=len(cast)),
        grid=(t // tb,),
        in_specs=[blk, state_spec, state_spec] + [_layer_spec(c, layer) for c in consts] + cast_in_specs,
        out_specs=[blk, state_spec, state_spec] + cast_specs,
        out_shape=[jax.ShapeDtypeStruct(u.shape, BF16),
                   jax.ShapeDtypeStruct((n_seq, n_state), F32),
                   jax.ShapeDtypeStruct((n_seq, n_state), F32)] + cast_shapes,
        scratch_shapes=[pltpu.VMEM((d_a // LANES, rows, LANES), F32),
                        pltpu.VMEM((rows, 2 * slab_st), F32), pltpu.VMEM((rows, 2 * slab_st), F32),
                        pltpu.VMEM((n_seq, n_state), F32), pltpu.VMEM((n_seq, n_state), F32)],
        compiler_params=_params("arbitrary"),
        name="s5_mixer",
    )(u, h0_re, h0_im, *consts, *cast_args)
    return y, hre, him, tuple(rounded)


def _sgu_chunks(u, vn, w_ref, bias_ref, y_ref):
    n_heads, cl, _ = w_ref.shape
    hd = vn.shape[1] // n_heads
    row = lax.broadcasted_iota(jnp.int32, (cl, cl), 0)
    col = lax.broadcasted_iota(jnp.int32, (cl, cl), 1)
    w = [jnp.where(col <= row, w_ref[h], 0.0).astype(BF16) for h in range(n_heads)]
    heads_per_slab = LANES // hd
    lane = lax.broadcasted_iota(jnp.int32, (cl, LANES), 1)
    for c in range(vn.shape[0] // cl):
        r = slice(c * cl, (c + 1) * cl)
        slabs = []
        for s in range(n_heads // heads_per_slab):
            v = vn[r, s * LANES:(s + 1) * LANES].astype(BF16)
            mixed = _dot(w[s * heads_per_slab], v)
            for j in range(1, heads_per_slab):
                mixed = jnp.where(lane >= j * hd, _dot(w[s * heads_per_slab + j], v), mixed)
            slabs.append(mixed)
        mixed = jnp.concatenate(slabs, axis=-1) + bias_ref[...]
        y_ref[r, :] = (u[r, :] * mixed).astype(y_ref.dtype)


def _sgu_short(u, vn, wl_ref, bias_ref, y_ref, *, nb):
    for i in range(wl_ref.shape[0]):
        mixed = bias_ref[i:i + 1, :]
        for j in range(i + 1):
            mixed = mixed + wl_ref[i, j:j + 1, :] * vn[j * nb:(j + 1) * nb, :]
        y_ref[i * nb:(i + 1) * nb, :] = (u[i * nb:(i + 1) * nb, :] * mixed).astype(y_ref.dtype)


def _sgu_params(w_s, b_s, d_b, t):
    hd = d_b // w_s.shape[1]
    cl = min(t, SGU_CHUNK)
    bias = jnp.repeat(jnp.swapaxes(b_s[:, :, :cl], 1, 2), hd, axis=2)
    if t % SGU_CHUNK == 0:
        return w_s, bias
    assert t < SGU_CHUNK
    return jnp.repeat(jnp.transpose(w_s[:, :, :cl, :cl], (0, 2, 3, 1)), hd, axis=3), bias


def _interleave(a):
    return jnp.swapaxes(a, 0, 1).reshape(1, a.shape[0] * a.shape[1], a.shape[2])


def _deinterleave(a, nb):
    return jnp.swapaxes(a.reshape(-1, nb, a.shape[2]), 0, 1)


def _run_trunk(x, s5_re, s5_im, hist_bufs, prefix_valid, W, *, interleaved, rounded=None):
    batch, t, d = x.shape
    depth = W["norm_mix"].shape[0]
    nb = batch if interleaved else 1
    pack = _interleave if interleaved else (lambda a: a)
    unpack = (lambda a: _deinterleave(a, nb)) if interleaved else (lambda a: a)
    common = dict(nb=nb, prefix_valid=prefix_valid, sgu_prm=_sgu_params(W["sgu_w"], W["sgu_b"], d // 2, t))

    def kind(layer):
        return "final" if layer == depth else ("even" if layer % 2 == 0 else "odd")

    def hist(layer):
        if kind(layer) != "odd" or hist_bufs is None:
            return None
        return tuple(pack(buf[layer // 2]) for buf in hist_bufs)

    def matrices(call):
        prev, need = call - 1, []
        if call > 0:
            need += [(W["w_out_even" if prev % 2 == 0 else "w_out_odd"], prev // 2),
                     (W["ffn_gate"], prev), (W["ffn_up"], prev), (W["ffn_down"], prev)]
        if call < depth:
            need.append((W["w_in_even" if call % 2 == 0 else "w_in_odd"], call // 2))
        return need

    convert = rounded is None
    if convert:
        rounded = [[w[idx].astype(BF16) for w, idx in matrices(0)]]

    s5_rounds_first = convert and kind(0) == "even"

    def cast(call):
        return matrices(call + 1) if convert and call < depth and not (call == 0 and s5_rounds_first) else ()

    h = pack(x)
    mixer_in, nxt_w = _trunk_call(h, 0, W, mix=None, wts=rounded[0], nxt=kind(0), odd_hist=hist(0), cast=cast(0),
                                  **common)
    rounded += [nxt_w] if nxt_w else []
    new_re, new_im, new_v, new_pool, new_conv = [], [], [], [], []
    for layer in range(depth):
        i = layer // 2
        if layer % 2 == 0:
            u_a, y_b = mixer_in[:2]
            y_a, hre, him, nxt_w = _s5_call(u_a, s5_re[i].reshape(batch, -1), s5_im[i].reshape(batch, -1), W["s5"],
                                            i, nb=nb, cast=matrices(1) if layer == 0 and s5_rounds_first else ())
            rounded += [nxt_w] if nxt_w else []
            new_re.append(hre.reshape(s5_re[i].shape))
            new_im.append(him.reshape(s5_im[i].shape))
            if interleaved:
                new_v.append(unpack(mixer_in[2]))
            mix = (y_a, y_b)
        else:
            y_c, y_d, pout, cout = mixer_in
            new_pool.append(unpack(pout))
            new_conv.append(unpack(cout))
            mix = (y_c, y_d)
        outs, nxt_w = _trunk_call(h, layer + 1, W, mix=mix, wts=rounded[layer + 1], nxt=kind(layer + 1),
                                  odd_hist=hist(layer + 1), cast=cast(layer + 1), **common)
        rounded += [nxt_w] if nxt_w else []
        h, mixer_in = outs[0], outs[1:]
    return (unpack(h), jnp.stack(new_re), jnp.stack(new_im), jnp.stack(new_v) if new_v else None,
            jnp.stack(new_pool), jnp.stack(new_conv), rounded)


def kernel(x_prompt, x_sample, state_s5_re, state_s5_im, state_pool, state_conv, norm_mix, norm_ffn, norm_final, w_in_even, w_out_even, s5_lambda_re, s5_lambda_im, s5_log_dt, s5_b_re, s5_b_im, s5_c_re, s5_c_im, s5_d, s5_glu_w, s5_glu_b, sgu_norm, sgu_w, sgu_b, w_in_odd, w_out_odd, pool_w, pool_scale, conv_w, conv_b, ffn_w_gate, ffn_w_up, ffn_w_down):
    n_even, n_odd = w_in_even.shape[0], w_in_odd.shape[0]
    depth = norm_mix.shape[0]
    s5 = [*_s5_prep(s5_lambda_re, s5_lambda_im, s5_log_dt, s5_b_re, s5_b_im),
          _blockdiag_out(s5_c_re), _blockdiag_out(s5_c_im), s5_d.reshape(n_even, 1, -1),
          s5_glu_w.astype(BF16), s5_glu_b.reshape(n_even, 1, -1)]
    d = norm_mix.shape[1]
    W = dict(
        norm_mix=norm_mix.reshape(depth, 1, d), norm_ffn=norm_ffn.reshape(depth, 1, d),
        norm_final=norm_final.reshape(1, d),
        w_in_even=w_in_even, w_out_even=w_out_even, w_in_odd=w_in_odd, w_out_odd=w_out_odd,
        sgu_norm=sgu_norm.reshape(n_even, 1, -1), sgu_w=sgu_w, sgu_b=sgu_b, s5=s5,
        pool_w=pool_w.astype(BF16), pool_scale=pool_scale.reshape(n_odd, 1, -1), conv_w=conv_w,
        conv_b=conv_b.reshape(n_odd, 1, -1),
        ffn_gate=ffn_w_gate, ffn_up=ffn_w_up, ffn_down=ffn_w_down,
    )
    bp = x_prompt.shape[0]
    z_s5 = jnp.zeros((n_even, bp) + state_s5_re.shape[2:], state_s5_re.dtype)
    y_p, p_re, p_im, _, p_pool, p_conv, rounded = _run_trunk(x_prompt, z_s5, z_s5, None, False, W,
                                                             interleaved=False)
    y_s, s_re, s_im, s_v, s_pool, s_conv, _ = _run_trunk(
        x_sample, state_s5_re, state_s5_im, (state_pool, state_conv), True, W, interleaved=True, rounded=rounded)
    return (y_p, y_s, p_re, p_im, p_pool, p_conv, s_re, s_im, s_v, s_pool, s_conv)
```

```python
import functools

import jax
import jax.numpy as jnp
from jax import lax
from jax.experimental import pallas as pl
from jax.experimental.pallas import tpu as pltpu

F32 = jnp.float32
BF16 = jnp.bfloat16

EPS = 1e-6
SGU_CHUNK = 128
POOL_WINDOWS = (2, 4, 8, 16)
POOL_BUF = max(POOL_WINDOWS) - 1
CONV_W = 3

LANES = 128
SUBLANES = 8
BF16_ROWS = 2 * SUBLANES
V7X_VMEM_BYTES = 64 * 1024 * 1024
ROW_TILE = 512
FF_CHUNK = 512
SCAN_PIECES = 4
S5_TILES_PER_STEP = 2
VMEM_LIMIT = V7X_VMEM_BYTES * 7 // 8


def _dot(a, b):
    return jnp.dot(a, b, preferred_element_type=F32)


def _rms(x, g):
    return x * lax.rsqrt(jnp.mean(x * x, axis=-1, keepdims=True) + EPS) * g


def _round_up(n, m):
    return -(-n // m) * m


def _const_spec(shape):
    zeros = (0,) * len(shape)
    return pl.BlockSpec(shape, lambda *_: zeros, pipeline_mode=pl.Buffered(1))


def _layer_spec(stack, layer):
    zeros = (0,) * (stack.ndim - 1)
    return pl.BlockSpec((None,) + stack.shape[1:], lambda *_: (layer,) + zeros, pipeline_mode=pl.Buffered(1))


def _params(*semantics):
    return pltpu.CompilerParams(dimension_semantics=semantics, vmem_limit_bytes=VMEM_LIMIT)


def _poolconv_pieces(xc, z, bg, i, pbuf_ref, cbuf_ref, pw_ref, ps_ref, cw_ref, cb_ref,
                     yc_ref, yd_ref, pout_ref, cout_ref, fullc, fullz, diff, *, nb, prefix_valid):
    hp, hc = POOL_BUF * nb, (CONV_W - 1) * nb
    p0, c0 = pbuf_ref.shape[0], cbuf_ref.shape[0]
    r = xc.shape[0]
    cg = xc.shape[1] // len(POOL_WINDOWS)
    first = i == 0

    def load_tile():
        fullc[0:p0, :] = jnp.where(first, pbuf_ref[...], fullc[0:p0, :])
        fullz[0:c0, :] = jnp.where(first, cbuf_ref[...], fullz[0:c0, :])
        fullc[p0:p0 + r, :] = xc[...]
        fullz[c0:c0 + r, :] = z[...]

    def conv():
        acc = cb_ref[...]
        for k in range(CONV_W):
            back = (CONV_W - 1 - k) * nb
            acc = acc + fullz[c0 - back:c0 - back + r, :] * cw_ref[k:k + 1, :]
        yd_ref[...] = (bg[...] * acc).astype(yd_ref.dtype)

    def window_sum(rows, lanes, w):
        s = fullc[p0:p0 + rows, lanes]
        for k in range(1, w):
            s = s + fullc[p0 - k * nb:p0 - k * nb + rows, lanes]
        return s

    def pool(gi, w):
        lanes = slice(gi * cg, (gi + 1) * cg)
        diff[:, lanes] = window_sum(r, lanes, w) * (1.0 / w) - fullc[p0:p0 + r, lanes]
        if not prefix_valid:
            head = min(p0, r)
            t = lax.broadcasted_iota(jnp.int32, (head, 1), 0) // nb
            n = jnp.minimum(t + 1, w).astype(F32)
            short = window_sum(head, lanes, w) / n - fullc[p0:p0 + head, lanes]
            diff[0:head, lanes] = jnp.where(first, short, diff[0:head, lanes])

    def project():
        for gi in range(len(POOL_WINDOWS)):
            lanes = slice(gi * cg, (gi + 1) * cg)
            yc = _dot(diff[:, lanes].astype(BF16), pw_ref[gi]) * ps_ref[:, lanes]
            yc_ref[:, lanes] = yc.astype(yc_ref.dtype)

    def shift_history():
        pout_ref[...] = fullc[p0 + r - hp:p0 + r, :]
        cout_ref[...] = fullz[c0 + r - hc:c0 + r, :]
        new_pool = fullc[r:r + p0, :]
        new_conv = fullz[r:r + c0, :]
        fullc[0:p0, :] = new_pool
        fullz[0:c0, :] = new_conv

    pools = [functools.partial(pool, gi, w) for gi, w in enumerate(POOL_WINDOWS)]
    return [load_tile, conv] + pools + [project, shift_history]


def _trunk_kernel(*refs, has_mix, nxt, d_half, nb, prefix_valid, tiles_per_group, n_cast):
    refs = list(refs)
    h_ref = refs.pop(0)
    if has_mix:
        ya_ref, yb_ref, wo_ref, gf_ref, wg_ref, wu_ref, wd_ref = refs[:7]
        del refs[:7]
    gn_ref = refs.pop(0)
    if nxt != "final":
        win_ref = refs.pop(0)
    if nxt == "even":
        gs_ref, sw_ref, sb_ref = refs[:3]
        del refs[:3]
    if nxt == "odd":
        odd_in = refs[:6]
        del refs[:6]
    cast_in = refs[:n_cast]
    del refs[:n_cast]
    outs = refs
    _cast_blocks(cast_in, outs[:n_cast])
    del outs[:n_cast]
    s = pl.program_id(0)
    if nxt == "odd":
        odd_out = outs[-7:]

        @pl.when(s == 0)
        def _():
            for ref in odd_out[-3:]:
                ref[...] = jnp.zeros(ref.shape, ref.dtype)

    h = h_ref[...]
    if has_mix:
        h = h + _dot(ya_ref[...], wo_ref[:d_half, :]) + _dot(yb_ref[...], wo_ref[d_half:, :])
        hn = _rms(h, gf_ref[...]).astype(BF16)
        d_ff = wg_ref.shape[1]
        acc = None
        for c0 in range(0, d_ff, FF_CHUNK):
            c1 = min(c0 + FF_CHUNK, d_ff)
            g = _dot(hn, wg_ref[:, c0:c1])
            u = _dot(hn, wu_ref[:, c0:c1])
            a = (g * jax.nn.sigmoid(g) * u).astype(BF16)
            part = _dot(a, wd_ref[c0:c1, :])
            acc = part if acc is None else acc + part
        h = h + acc
        if nxt != "final":
            outs.pop(0)[...] = h
    hn = _rms(h, gn_ref[...])
    if nxt == "final":
        outs[0][...] = hn
        return
    proj = _dot(hn.astype(BF16), win_ref[...])
    if nxt == "even":
        ua_ref, yb_ref = outs[:2]
        ua_ref[...] = proj[:, :d_half]
        ub = proj[:, d_half:2 * d_half]
        vn = _rms(proj[:, 2 * d_half:], gs_ref[...])
        if nb == 1:
            _sgu_chunks(ub, vn, sw_ref, sb_ref, yb_ref)
        else:
            _sgu_short(ub, vn, sw_ref, sb_ref, yb_ref, nb=nb)
            outs[2][...] = vn
    else:
        xc = proj[:, :d_half]
        z = proj[:, 3 * d_half:] * proj[:, d_half:2 * d_half]
        bg = proj[:, 2 * d_half:3 * d_half]
        for piece in _poolconv_pieces(xc, z, bg, s % tiles_per_group, *odd_in, *odd_out, nb=nb,
                                      prefix_valid=prefix_valid):
            piece()


def _cast_stream(cast, n_steps):
    args, in_specs, out_shapes, out_specs = [], [], [], []
    for w, idx in cast:
        n_rows, n_cols = w.shape[1:]
        rpb = next(k for k in range(BF16_ROWS, n_rows + 1, BF16_ROWS)
                   if n_rows % k == 0 and n_rows // k <= n_steps)
        n_blk = n_rows // rpb
        args.append(w)
        in_specs.append(pl.BlockSpec((None, rpb, n_cols),
                                     lambda s, n_blk=n_blk, idx=idx: (idx, s * n_blk // n_steps, 0)))
        out_shapes.append(jax.ShapeDtypeStruct((n_rows, n_cols), BF16))
        out_specs.append(pl.BlockSpec((rpb, n_cols), lambda s, n_blk=n_blk: (s * n_blk // n_steps, 0)))
    return args, in_specs, out_shapes, out_specs


def _cast_blocks(srcs, dsts):
    for src, dst in zip(srcs, dsts):
        dst[...] = src[...].astype(dst.dtype)


def _trunk_call(h, layer, W, *, mix, wts, nxt, nb, prefix_valid, sgu_prm, cast=(), odd_hist=None):
    n_grp, r, d = h.shape
    d_half = d // 2
    tile = min(ROW_TILE if mix is not None else 2 * ROW_TILE, r)
    i = layer // 2
    tpg = r // tile
    n_tiles = n_grp * tpg

    def rows(c):
        return pl.BlockSpec((None, tile, c), lambda s: (s // tpg, s % tpg, 0))

    def hist_spec(n_rows):
        return pl.BlockSpec((None, n_rows, d_half), lambda s: (s // tpg, 0, 0))

    args, specs = [h], [rows(d)]
    if mix is not None:
        ya, yb = mix
        w_out, gate, up, down = wts[:4]
        args += [ya, yb, w_out, W["norm_ffn"], gate, up, down]
        specs += [rows(d_half), rows(d_half), _const_spec(w_out.shape), _layer_spec(W["norm_ffn"], layer - 1)]
        specs += [_const_spec(w.shape) for w in (gate, up, down)]
    if nxt == "final":
        args.append(W["norm_final"])
        specs.append(_const_spec(W["norm_final"].shape))
    else:
        args += [W["norm_mix"], wts[-1]]
        specs += [_layer_spec(W["norm_mix"], layer), _const_spec(wts[-1].shape)]
    scratch = []
    if nxt == "even":
        sgu = [W["sgu_norm"], *sgu_prm]
        args += sgu
        specs += [_layer_spec(w, i) for w in sgu]
        assert (nb == 1 and tile % SGU_CHUNK == 0) or (n_tiles == 1 and tile // nb < SGU_CHUNK)
        out_shapes = [jax.ShapeDtypeStruct((n_grp, r, d_half), F32), jax.ShapeDtypeStruct((n_grp, r, d_half), BF16)]
        if nb > 1:
            out_shapes.append(jax.ShapeDtypeStruct((n_grp, r, d_half), F32))
        out_specs = [rows(d_half)] * len(out_shapes)
    elif nxt == "odd":
        hp, hc = POOL_BUF * nb, (CONV_W - 1) * nb
        p0, c0 = _round_up(hp, SUBLANES), _round_up(hc, SUBLANES)
        if odd_hist is None:
            pool_buf, conv_buf = jnp.zeros((n_grp, p0, d_half), F32), jnp.zeros((n_grp, c0, d_half), F32)
        else:
            pool_buf, conv_buf = odd_hist
            assert (p0, c0) == (hp, hc)
        odd_w = [W["pool_w"], W["pool_scale"], W["conv_w"], W["conv_b"]]
        args += [pool_buf, conv_buf] + odd_w
        specs += [hist_spec(p0), hist_spec(c0)] + [_layer_spec(w, i) for w in odd_w]
        scratch = [pltpu.VMEM((p0 + tile, d_half), F32), pltpu.VMEM((c0 + tile, d_half), F32),
                   pltpu.VMEM((tile, d_half), F32)]
        out_shapes = [jax.ShapeDtypeStruct((n_grp, r, d_half), BF16)] * 2 + [
            jax.ShapeDtypeStruct((n_grp, hp, d_half), F32), jax.ShapeDtypeStruct((n_grp, hc, d_half), F32)]
        out_specs = [rows(d_half)] * 2 + [hist_spec(hp), hist_spec(hc)]
    else:
        out_shapes, out_specs = [], []
    if mix is not None or nxt == "final":
        out_shapes = [jax.ShapeDtypeStruct((n_grp, r, d), F32)] + out_shapes
        out_specs = [rows(d)] + out_specs
    n_cast = len(cast)
    cast_args, cast_in_specs, cast_shapes, cast_specs = _cast_stream(cast, n_tiles)
    args, specs = args + cast_args, specs + cast_in_specs
    out_shapes, out_specs = cast_shapes + out_shapes, cast_specs + out_specs

    outs = pl.pallas_call(
        functools.partial(_trunk_kernel, has_mix=mix is not None, nxt=nxt, d_half=d_half, nb=nb,
                          prefix_valid=prefix_valid, tiles_per_group=tpg, n_cast=n_cast),
        grid=(n_tiles,),
        in_specs=specs,
        out_specs=out_specs,
        out_shape=out_shapes,
        scratch_shapes=scratch,
        compiler_params=_params("arbitrary"),
        name=f"trunk_{'mix' if mix is not None else 'in'}_{nxt}",
    )(*args)
    return outs[n_cast:], tuple(outs[:n_cast])


def _s5_prep_kernel(lr_ref, li_ref, ldt_ref, bre_ref, bim_ref, are_ref, aim_ref, bbre_ref, bbim_ref):
    lr = lr_ref[...]
    li = li_ref[...]
    dt = jnp.exp(ldt_ref[...])
    mag = jnp.exp(lr * dt)
    ang = li * dt
    ab_re = mag * jnp.cos(ang)
    ab_im = mag * jnp.sin(ang)
    den = lr * lr + li * li
    f_re = ((ab_re - 1.0) * lr + ab_im * li) / den
    f_im = (ab_im * lr - (ab_re - 1.0) * li) / den
    are_ref[...] = ab_re
    aim_ref[...] = ab_im
    for g in range(lr.shape[0]):
        fr = f_re[g:g + 1, :]
        fi = f_im[g:g + 1, :]
        br = bre_ref[g]
        bi = bim_ref[g]
        bbre_ref[g] = fr * br - fi * bi
        bbim_ref[g] = fr * bi + fi * br


def _s5_prep(lam_re, lam_im, log_dt, b_re, b_im):
    l, g, n = lam_re.shape
    p = b_re.shape[-1]
    bt_re = jnp.swapaxes(b_re, 2, 3).reshape(l * g, p, n)
    bt_im = jnp.swapaxes(b_im, 2, 3).reshape(l * g, p, n)
    a_re, a_im, bb_re, bb_im = pl.pallas_call(
        _s5_prep_kernel,
        out_shape=[jax.ShapeDtypeStruct((l * g, n), F32)] * 2 + [jax.ShapeDtypeStruct((l * g, p, n), F32)] * 2,
        name="s5_discretize",
    )(lam_re.reshape(l * g, n), lam_im.reshape(l * g, n), log_dt.reshape(l * g, 1), bt_re, bt_im)
    gs = LANES // p
    eye = jnp.eye(gs, dtype=F32)

    def blockdiag_in(bb):
        return jnp.einsum("lkgpn,gh->lkgphn", bb.reshape(l, g // gs, gs, p, n), eye).reshape(
            l, g // gs, gs * p, gs * n).astype(BF16)

    def sublane_rows(a):
        return jnp.broadcast_to(a.reshape(l, 1, g * n), (l, SUBLANES, g * n))

    return sublane_rows(a_re), sublane_rows(a_im), blockdiag_in(bb_re), blockdiag_in(bb_im)


def _blockdiag_out(c):
    l, g, p, n = c.shape
    gs = LANES // p
    eye = jnp.eye(gs, dtype=F32)
    return jnp.einsum("lkgpn,gh->lkgnhp", c.reshape(l, g // gs, gs, p, n), eye).reshape(
        l, g // gs, gs * n, gs * p).astype(BF16)


def _s5_pieces(u_ref, prm, scratch, store_y, *, n_grp, nb, tb):
    are_ref, aim_ref, bbre_ref, bbim_ref, cre_ref, cim_ref, d_ref, gw_ref, gb_ref = prm
    ut, xre, xim, hre_s, him_s = scratch
    yt = ut
    n_seq = n_grp * nb
    n_slab, slab_in, slab_st = bbre_ref.shape

    def ring(k):
        return slice(k % 2 * slab_st, (k % 2 + 1) * slab_st)

    def load_u():
        for k in range(n_slab):
            lanes = slice(k * slab_in, (k + 1) * slab_in)
            if n_grp == 1:
                ut[k] = u_ref[0, :, lanes]
            else:
                for b in range(n_grp):
                    ut[k, pl.ds(b, tb, stride=n_grp), :] = u_ref[b, :, lanes]

    def project_in(k, x, bb_ref):
        x[:, ring(k)] = _dot(ut[k].astype(BF16), bb_ref[k])

    def read_out(k, x, c_ref, rows):
        return _dot(x[rows, ring(k)].astype(BF16), c_ref[k])

    def scan_pieces(k):
        cols, xcols = slice(k * slab_st, (k + 1) * slab_st), ring(k)
        items = [(r0, t) for r0 in range(0, n_seq, SUBLANES) for t in range(tb)]
        per = len(items) // SCAN_PIECES

        def run(chunk):
            hr = hi = cur = None
            for r0, t in chunk:
                if r0 != cur:
                    if cur is not None:
                        hre_s[cur:cur + SUBLANES, cols] = hr
                        him_s[cur:cur + SUBLANES, cols] = hi
                    hr, hi, cur = hre_s[r0:r0 + SUBLANES, cols], him_s[r0:r0 + SUBLANES, cols], r0
                ar = are_ref[:, cols]
                ai = aim_ref[:, cols]
                r = t * n_seq + r0
                nr = ar * hr - ai * hi + xre[r:r + SUBLANES, xcols]
                ni = ar * hi + ai * hr + xim[r:r + SUBLANES, xcols]
                xre[r:r + SUBLANES, xcols] = nr
                xim[r:r + SUBLANES, xcols] = ni
                hr, hi = nr, ni
            hre_s[cur:cur + SUBLANES, cols] = hr
            him_s[cur:cur + SUBLANES, cols] = hi

        return [functools.partial(run, items[p * per:(p + 1) * per]) for p in range(SCAN_PIECES)]

    rows = slice(0, tb * n_seq)
    re_im = ((xre, bbre_ref, cre_ref), (xim, bbim_ref, cim_ref))
    y_parts, box = {}, {}

    def read_out_part(k, x, c):
        y_parts.setdefault(k, []).append(read_out(k, x, c, rows))

    half = tb * n_seq // 2
    row_halves = (slice(0, half), slice(half, 2 * half))

    def gate():
        outs = []
        for rh in row_halves:
            y = [y_parts[k][0][rh] - y_parts[k][1][rh] + d_ref[:, k * slab_in:(k + 1) * slab_in] * ut[k, rh, :]
                 for k in range(n_slab)]
            z = jax.nn.gelu(jnp.concatenate(y, axis=-1))
            outs.append(z * jax.nn.sigmoid(_dot(z.astype(BF16), gw_ref[...]) + gb_ref[...]))
        box["out"] = outs

    def write(k):
        lanes = slice(k * slab_in, (k + 1) * slab_in)
        if n_grp == 1:
            store_y(0, lanes, jnp.concatenate([o[:, lanes] for o in box["out"]], axis=0).astype(BF16))
        else:
            for rh, o in zip(row_halves, box["out"]):
                yt[k, rh, :] = o[:, lanes]
            for b in range(n_grp):
                store_y(b, lanes, yt[k, pl.ds(b, tb, stride=n_grp), :].astype(BF16))

    pieces = [load_u] + [functools.partial(project_in, 0, x, bb) for x, bb, _ in re_im]
    for k in range(n_slab):
        matmuls = [functools.partial(read_out_part, k - 1, x, c) for x, _, c in re_im] if k > 0 else []
        matmuls += [functools.partial(project_in, k + 1, x, bb) for x, bb, _ in re_im] if k + 1 < n_slab else []
        scans = scan_pieces(k)
        for p, piece in enumerate(scans):
            pieces += matmuls[p * len(matmuls) // len(scans):(p + 1) * len(matmuls) // len(scans)] + [piece]
    pieces += [functools.partial(read_out_part, n_slab - 1, x, c) for x, _, c in re_im] + [gate]
    pieces += [functools.partial(write, k) for k in range(n_slab)]
    return pieces


def _s5_kernel(*refs, n_grp, nb, tb, n_cast):
    n_in = 12
    u_ref, h0re_ref, h0im_ref = refs[:3]
    y_ref, hre_out, him_out = refs[n_in + n_cast:n_in + n_cast + 3]
    scratch = refs[n_in + 2 * n_cast + 3:]
    hre_s, him_s = scratch[-2:]
    _cast_blocks(refs[n_in:n_in + n_cast], refs[n_in + n_cast + 3:n_in + 2 * n_cast + 3])

    @pl.when(pl.program_id(0) == 0)
    def _():
        hre_s[...] = h0re_ref[...]
        him_s[...] = h0im_ref[...]

    n_sub = u_ref.shape[1] // (tb * nb)
    ut_all = scratch[0]
    lists = []
    for sub in range(n_sub):
        rows = slice(sub * tb * nb, (sub + 1) * tb * nb)

        def store_y(b, lanes, y, rows=rows):
            y_ref[b, rows, lanes] = y

        lists.append(_s5_pieces(u_ref.at[:, rows, :], refs[3:n_in], (ut_all.at[sub],) + tuple(scratch[1:]),
                                store_y, n_grp=n_grp, nb=nb, tb=tb))
    n_tail = 7
    order = list(lists[0][:-n_tail])
    for sub in range(1, n_sub):
        tail, nxt = lists[sub - 1][-n_tail:], lists[sub]
        for a, b in zip(tail, nxt[:n_tail]):
            order += [a, b]
        order += nxt[n_tail:-n_tail]
    order += lists[-1][-n_tail:]
    for piece in order:
        piece()
    hre_out[...] = hre_s[...]
    him_out[...] = him_s[...]


def _s5_call(u, h0_re, h0_im, consts, layer, *, nb, cast=()):
    n_grp, r, d_a = u.shape
    assert n_grp == 1 or nb == 1
    t = r // nb
    n_seq = n_grp * nb
    n_state, slab_st = consts[0].shape[2], consts[2].shape[3]
    tb = max(1, min(t, ROW_TILE // n_seq))
    rows = tb * n_seq
    n_sub = S5_TILES_PER_STEP if (t // tb) % S5_TILES_PER_STEP == 0 else 1
    n_steps = t // tb // n_sub
    state_spec = pl.BlockSpec((n_seq, n_state), lambda i: (0, 0))
    blk = pl.BlockSpec((n_grp, n_sub * tb * nb, d_a), lambda i: (0, i, 0))
    cast_args, cast_in_specs, cast_shapes, cast_specs = _cast_stream(cast, n_steps)
    y, hre, him, *rounded = pl.pallas_call(
        functools.partial(_s5_kernel, n_grp=n_grp, nb=nb, tb=tb, n_cast=len(cast)),
        grid=(n_steps,),
        in_specs=[blk, state_spec, state_spec] + [_layer_spec(c, layer) for c in consts] + cast_in_specs,
        out_specs=[blk, state_spec, state_spec] + cast_specs,
        out_shape=[jax.ShapeDtypeStruct(u.shape, BF16),
                   jax.ShapeDtypeStruct((n_seq, n_state), F32),
                   jax.ShapeDtypeStruct((n_seq, n_state), F32)] + cast_shapes,
        scratch_shapes=[pltpu.VMEM((n_sub, d_a // LANES, rows, LANES), F32),
                        pltpu.VMEM((rows, 2 * slab_st), F32), pltpu.VMEM((rows, 2 * slab_st), F32),
                        pltpu.VMEM((n_seq, n_state), F32), pltpu.VMEM((n_seq, n_state), F32)],
        compiler_params=_params("arbitrary"),
        name="s5_mixer",
    )(u, h0_re, h0_im, *consts, *cast_args)
    return y, hre, him, tuple(rounded)


def _sgu_chunks(u, vn, w_ref, bias_ref, y_ref):
    n_heads, cl, _ = w_ref.shape
    hd = vn.shape[1] // n_heads
    row = lax.broadcasted_iota(jnp.int32, (cl, cl), 0)
    col = lax.broadcasted_iota(jnp.int32, (cl, cl), 1)
    w = [jnp.where(col <= row, w_ref[h], 0.0).astype(BF16) for h in range(n_heads)]
    heads_per_slab = LANES // hd
    lane = lax.broadcasted_iota(jnp.int32, (cl, LANES), 1)
    for c in range(vn.shape[0] // cl):
        r = slice(c * cl, (c + 1) * cl)
        slabs = []
        for s in range(n_heads // heads_per_slab):
            v = vn[r, s * LANES:(s + 1) * LANES].astype(BF16)
            mixed = _dot(w[s * heads_per_slab], v)
            for j in range(1, heads_per_slab):
                mixed = jnp.where(lane >= j * hd, _dot(w[s * heads_per_slab + j], v), mixed)
            slabs.append(mixed)
        mixed = jnp.concatenate(slabs, axis=-1) + bias_ref[...]
        y_ref[r, :] = (u[r, :] * mixed).astype(y_ref.dtype)


def _sgu_short(u, vn, wl_ref, bias_ref, y_ref, *, nb):
    for i in range(wl_ref.shape[0]):
        mixed = bias_ref[i:i + 1, :]
        for j in range(i + 1):
            mixed = mixed + wl_ref[i, j:j + 1, :] * vn[j * nb:(j + 1) * nb, :]
        y_ref[i * nb:(i + 1) * nb, :] = (u[i * nb:(i + 1) * nb, :] * mixed).astype(y_ref.dtype)


def _sgu_params(w_s, b_s, d_b, t):
    hd = d_b // w_s.shape[1]
    cl = min(t, SGU_CHUNK)
    bias = jnp.repeat(jnp.swapaxes(b_s[:, :, :cl], 1, 2), hd, axis=2)
    if t % SGU_CHUNK == 0:
        return w_s, bias
    assert t < SGU_CHUNK
    return jnp.repeat(jnp.transpose(w_s[:, :, :cl, :cl], (0, 2, 3, 1)), hd, axis=3), bias


def _interleave(a):
    return jnp.swapaxes(a, 0, 1).reshape(1, a.shape[0] * a.shape[1], a.shape[2])


def _deinterleave(a, nb):
    return jnp.swapaxes(a.reshape(-1, nb, a.shape[2]), 0, 1)


def _run_trunk(x, s5_re, s5_im, hist_bufs, prefix_valid, W, *, interleaved, rounded=None):
    batch, t, d = x.shape
    depth = W["norm_mix"].shape[0]
    nb = batch if interleaved else 1
    pack = _interleave if interleaved else (lambda a: a)
    unpack = (lambda a: _deinterleave(a, nb)) if interleaved else (lambda a: a)
    common = dict(nb=nb, prefix_valid=prefix_valid, sgu_prm=_sgu_params(W["sgu_w"], W["sgu_b"], d // 2, t))

    def kind(layer):
        return "final" if layer == depth else ("even" if layer % 2 == 0 else "odd")

    def hist(layer):
        if kind(layer) != "odd" or hist_bufs is None:
            return None
        return tuple(pack(buf[layer // 2]) for buf in hist_bufs)

    def matrices(call):
        prev, need = call - 1, []
        if call > 0:
            need += [(W["w_out_even" if prev % 2 == 0 else "w_out_odd"], prev // 2),
                     (W["ffn_gate"], prev), (W["ffn_up"], prev), (W["ffn_down"], prev)]
        if call < depth:
            need.append((W["w_in_even" if call % 2 == 0 else "w_in_odd"], call // 2))
        return need

    convert = rounded is None
    if convert:
        rounded = [[w[idx].astype(BF16) for w, idx in matrices(0)]]

    s5_rounds_first = convert and kind(0) == "even"

    def cast(call):
        return matrices(call + 1) if convert and call < depth and not (call == 0 and s5_rounds_first) else ()

    h = pack(x)
    mixer_in, nxt_w = _trunk_call(h, 0, W, mix=None, wts=rounded[0], nxt=kind(0), odd_hist=hist(0), cast=cast(0),
                                  **common)
    rounded += [nxt_w] if nxt_w else []
    new_re, new_im, new_v, new_pool, new_conv = [], [], [], [], []
    for layer in range(depth):
        i = layer // 2
        if layer % 2 == 0:
            u_a, y_b = mixer_in[:2]
            y_a, hre, him, nxt_w = _s5_call(u_a, s5_re[i].reshape(batch, -1), s5_im[i].reshape(batch, -1), W["s5"],
                                            i, nb=nb, cast=matrices(1) if layer == 0 and s5_rounds_first else ())
            rounded += [nxt_w] if nxt_w else []
            new_re.append(hre.reshape(s5_re[i].shape))
            new_im.append(him.reshape(s5_im[i].shape))
            if interleaved:
                new_v.append(unpack(mixer_in[2]))
            mix = (y_a, y_b)
        else:
            y_c, y_d, pout, cout = mixer_in
            new_pool.append(unpack(pout))
            new_conv.append(unpack(cout))
            mix = (y_c, y_d)
        outs, nxt_w = _trunk_call(h, layer + 1, W, mix=mix, wts=rounded[layer + 1], nxt=kind(layer + 1),
                                  odd_hist=hist(layer + 1), cast=cast(layer + 1), **common)
        rounded += [nxt_w] if nxt_w else []
        h, mixer_in = outs[0], outs[1:]
    return (unpack(h), jnp.stack(new_re), jnp.stack(new_im), jnp.stack(new_v) if new_v else None,
            jnp.stack(new_pool), jnp.stack(new_conv), rounded)


def kernel(x_prompt, x_sample, state_s5_re, state_s5_im, state_pool, state_conv, norm_mix, norm_ffn, norm_final, w_in_even, w_out_even, s5_lambda_re, s5_lambda_im, s5_log_dt, s5_b_re, s5_b_im, s5_c_re, s5_c_im, s5_d, s5_glu_w, s5_glu_b, sgu_norm, sgu_w, sgu_b, w_in_odd, w_out_odd, pool_w, pool_scale, conv_w, conv_b, ffn_w_gate, ffn_w_up, ffn_w_down):
    n_even, n_odd = w_in_even.shape[0], w_in_odd.shape[0]
    depth = norm_mix.shape[0]
    s5 = [*_s5_prep(s5_lambda_re, s5_lambda_im, s5_log_dt, s5_b_re, s5_b_im),
          _blockdiag_out(s5_c_re), _blockdiag_out(s5_c_im), s5_d.reshape(n_even, 1, -1),
          s5_glu_w.astype(BF16), s5_glu_b.reshape(n_even, 1, -1)]
    d = norm_mix.shape[1]
    W = dict(
        norm_mix=norm_mix.reshape(depth, 1, d), norm_ffn=norm_ffn.reshape(depth, 1, d),
        norm_final=norm_final.reshape(1, d),
        w_in_even=w_in_even, w_out_even=w_out_even, w_in_odd=w_in_odd, w_out_odd=w_out_odd,
        sgu_norm=sgu_norm.reshape(n_even, 1, -1), sgu_w=sgu_w, sgu_b=sgu_b, s5=s5,
        pool_w=pool_w.astype(BF16), pool_scale=pool_scale.reshape(n_odd, 1, -1), conv_w=conv_w,
        conv_b=conv_b.reshape(n_odd, 1, -1),
        ffn_gate=ffn_w_gate, ffn_up=ffn_w_up, ffn_down=ffn_w_down,
    )
    bp = x_prompt.shape[0]
    z_s5 = jnp.zeros((n_even, bp) + state_s5_re.shape[2:], state_s5_re.dtype)
    y_p, p_re, p_im, _, p_pool, p_conv, rounded = _run_trunk(x_prompt, z_s5, z_s5, None, False, W,
                                                             interleaved=False)
    y_s, s_re, s_im, s_v, s_pool, s_conv, _ = _run_trunk(
        x_sample, state_s5_re, state_s5_im, (state_pool, state_conv), True, W, interleaved=True, rounded=rounded)
    return (y_p, y_s, p_re, p_im, p_pool, p_conv, s_re, s_im, s_v, s_pool, s_conv)
```

```python
import functools

import jax
import jax.numpy as jnp
from jax import lax
from jax.experimental import pallas as pl
from jax.experimental.pallas import tpu as pltpu

F32 = jnp.float32
BF16 = jnp.bfloat16

EPS = 1e-6
SGU_CHUNK = 128
POOL_WINDOWS = (2, 4, 8, 16)
POOL_BUF = max(POOL_WINDOWS) - 1
CONV_W = 3

LANES = 128
SUBLANES = 8
BF16_ROWS = 2 * SUBLANES
V7X_VMEM_BYTES = 64 * 1024 * 1024
ROW_TILE = 512
FF_CHUNK = 512
SCAN_PIECES = 4
S5_TILES_PER_STEP = 4
VMEM_LIMIT = V7X_VMEM_BYTES * 7 // 8


def _dot(a, b):
    return jnp.dot(a, b, preferred_element_type=F32)


def _rms(x, g):
    return x * lax.rsqrt(jnp.mean(x * x, axis=-1, keepdims=True) + EPS) * g


def _round_up(n, m):
    return -(-n // m) * m


def _const_spec(shape):
    zeros = (0,) * len(shape)
    return pl.BlockSpec(shape, lambda *_: zeros, pipeline_mode=pl.Buffered(1))


def _layer_spec(stack, layer):
    zeros = (0,) * (stack.ndim - 1)
    return pl.BlockSpec((None,) + stack.shape[1:], lambda *_: (layer,) + zeros, pipeline_mode=pl.Buffered(1))


def _params(*semantics):
    return pltpu.CompilerParams(dimension_semantics=semantics, vmem_limit_bytes=VMEM_LIMIT)


def _poolconv_pieces(xc, z, bg, i, pbuf_ref, cbuf_ref, pw_ref, ps_ref, cw_ref, cb_ref,
                     yc_ref, yd_ref, pout_ref, cout_ref, fullc, fullz, diff, *, nb, prefix_valid):
    hp, hc = POOL_BUF * nb, (CONV_W - 1) * nb
    p0, c0 = pbuf_ref.shape[0], cbuf_ref.shape[0]
    r = xc.shape[0]
    cg = xc.shape[1] // len(POOL_WINDOWS)
    first = i == 0

    def load_tile():
        fullc[0:p0, :] = jnp.where(first, pbuf_ref[...], fullc[0:p0, :])
        fullz[0:c0, :] = jnp.where(first, cbuf_ref[...], fullz[0:c0, :])
        fullc[p0:p0 + r, :] = xc[...]
        fullz[c0:c0 + r, :] = z[...]

    def conv():
        acc = cb_ref[...]
        for k in range(CONV_W):
            back = (CONV_W - 1 - k) * nb
            acc = acc + fullz[c0 - back:c0 - back + r, :] * cw_ref[k:k + 1, :]
        yd_ref[...] = (bg[...] * acc).astype(yd_ref.dtype)

    def window_sum(rows, lanes, w):
        s = fullc[p0:p0 + rows, lanes]
        for k in range(1, w):
            s = s + fullc[p0 - k * nb:p0 - k * nb + rows, lanes]
        return s

    def pool(gi, w):
        lanes = slice(gi * cg, (gi + 1) * cg)
        diff[:, lanes] = window_sum(r, lanes, w) * (1.0 / w) - fullc[p0:p0 + r, lanes]
        if not prefix_valid:
            head = min(p0, r)
            t = lax.broadcasted_iota(jnp.int32, (head, 1), 0) // nb
            n = jnp.minimum(t + 1, w).astype(F32)
            short = window_sum(head, lanes, w) / n - fullc[p0:p0 + head, lanes]
            diff[0:head, lanes] = jnp.where(first, short, diff[0:head, lanes])

    def project():
        for gi in range(len(POOL_WINDOWS)):
            lanes = slice(gi * cg, (gi + 1) * cg)
            yc = _dot(diff[:, lanes].astype(BF16), pw_ref[gi]) * ps_ref[:, lanes]
            yc_ref[:, lanes] = yc.astype(yc_ref.dtype)

    def shift_history():
        pout_ref[...] = fullc[p0 + r - hp:p0 + r, :]
        cout_ref[...] = fullz[c0 + r - hc:c0 + r, :]
        new_pool = fullc[r:r + p0, :]
        new_conv = fullz[r:r + c0, :]
        fullc[0:p0, :] = new_pool
        fullz[0:c0, :] = new_conv

    pools = [functools.partial(pool, gi, w) for gi, w in enumerate(POOL_WINDOWS)]
    return [load_tile, conv] + pools + [project, shift_history]


def _trunk_kernel(*refs, has_mix, nxt, d_half, nb, prefix_valid, tiles_per_group, n_cast):
    refs = list(refs)
    h_ref = refs.pop(0)
    if has_mix:
        ya_ref, yb_ref, wo_ref, gf_ref, wg_ref, wu_ref, wd_ref = refs[:7]
        del refs[:7]
    gn_ref = refs.pop(0)
    if nxt != "final":
        win_ref = refs.pop(0)
    if nxt == "even":
        gs_ref, sw_ref, sb_ref = refs[:3]
        del refs[:3]
    if nxt == "odd":
        odd_in = refs[:6]
        del refs[:6]
    cast_in = refs[:n_cast]
    del refs[:n_cast]
    outs = refs
    _cast_blocks(cast_in, outs[:n_cast])
    del outs[:n_cast]
    s = pl.program_id(0)
    if nxt == "odd":
        odd_out = outs[-7:]

        @pl.when(s == 0)
        def _():
            for ref in odd_out[-3:]:
                ref[...] = jnp.zeros(ref.shape, ref.dtype)

    h = h_ref[...]
    if has_mix:
        h = h + _dot(ya_ref[...], wo_ref[:d_half, :]) + _dot(yb_ref[...], wo_ref[d_half:, :])
        hn = _rms(h, gf_ref[...]).astype(BF16)
        d_ff = wg_ref.shape[1]
        acc = None
        for c0 in range(0, d_ff, FF_CHUNK):
            c1 = min(c0 + FF_CHUNK, d_ff)
            g = _dot(hn, wg_ref[:, c0:c1])
            u = _dot(hn, wu_ref[:, c0:c1])
            a = (g * jax.nn.sigmoid(g) * u).astype(BF16)
            part = _dot(a, wd_ref[c0:c1, :])
            acc = part if acc is None else acc + part
        h = h + acc
        if nxt != "final":
            outs.pop(0)[...] = h
    hn = _rms(h, gn_ref[...])
    if nxt == "final":
        outs[0][...] = hn
        return
    proj = _dot(hn.astype(BF16), win_ref[...])
    if nxt == "even":
        ua_ref, yb_ref = outs[:2]
        ua_ref[...] = proj[:, :d_half]
        ub = proj[:, d_half:2 * d_half]
        vn = _rms(proj[:, 2 * d_half:], gs_ref[...])
        if nb == 1:
            _sgu_chunks(ub, vn, sw_ref, sb_ref, yb_ref)
        else:
            _sgu_short(ub, vn, sw_ref, sb_ref, yb_ref, nb=nb)
            outs[2][...] = vn
    else:
        xc = proj[:, :d_half]
        z = proj[:, 3 * d_half:] * proj[:, d_half:2 * d_half]
        bg = proj[:, 2 * d_half:3 * d_half]
        for piece in _poolconv_pieces(xc, z, bg, s % tiles_per_group, *odd_in, *odd_out, nb=nb,
                                      prefix_valid=prefix_valid):
            piece()


def _cast_stream(cast, n_steps):
    args, in_specs, out_shapes, out_specs = [], [], [], []
    for w, idx in cast:
        n_rows, n_cols = w.shape[1:]
        rpb = next(k for k in range(BF16_ROWS, n_rows + 1, BF16_ROWS)
                   if n_rows % k == 0 and n_rows // k <= n_steps)
        n_blk = n_rows // rpb
        args.append(w)
        in_specs.append(pl.BlockSpec((None, rpb, n_cols),
                                     lambda s, n_blk=n_blk, idx=idx: (idx, s * n_blk // n_steps, 0)))
        out_shapes.append(jax.ShapeDtypeStruct((n_rows, n_cols), BF16))
        out_specs.append(pl.BlockSpec((rpb, n_cols), lambda s, n_blk=n_blk: (s * n_blk // n_steps, 0)))
    return args, in_specs, out_shapes, out_specs


def _cast_blocks(srcs, dsts):
    for src, dst in zip(srcs, dsts):
        dst[...] = src[...].astype(dst.dtype)


def _trunk_call(h, layer, W, *, mix, wts, nxt, nb, prefix_valid, sgu_prm, cast=(), odd_hist=None):
    n_grp, r, d = h.shape
    d_half = d // 2
    tile = min(ROW_TILE if mix is not None else 2 * ROW_TILE, r)
    i = layer // 2
    tpg = r // tile
    n_tiles = n_grp * tpg

    def rows(c):
        return pl.BlockSpec((None, tile, c), lambda s: (s // tpg, s % tpg, 0))

    def hist_spec(n_rows):
        return pl.BlockSpec((None, n_rows, d_half), lambda s: (s // tpg, 0, 0))

    args, specs = [h], [rows(d)]
    if mix is not None:
        ya, yb = mix
        w_out, gate, up, down = wts[:4]
        args += [ya, yb, w_out, W["norm_ffn"], gate, up, down]
        specs += [rows(d_half), rows(d_half), _const_spec(w_out.shape), _layer_spec(W["norm_ffn"], layer - 1)]
        specs += [_const_spec(w.shape) for w in (gate, up, down)]
    if nxt == "final":
        args.append(W["norm_final"])
        specs.append(_const_spec(W["norm_final"].shape))
    else:
        args += [W["norm_mix"], wts[-1]]
        specs += [_layer_spec(W["norm_mix"], layer), _const_spec(wts[-1].shape)]
    scratch = []
    if nxt == "even":
        sgu = [W["sgu_norm"], *sgu_prm]
        args += sgu
        specs += [_layer_spec(w, i) for w in sgu]
        assert (nb == 1 and tile % SGU_CHUNK == 0) or (n_tiles == 1 and tile // nb < SGU_CHUNK)
        out_shapes = [jax.ShapeDtypeStruct((n_grp, r, d_half), F32), jax.ShapeDtypeStruct((n_grp, r, d_half), BF16)]
        if nb > 1:
            out_shapes.append(jax.ShapeDtypeStruct((n_grp, r, d_half), F32))
        out_specs = [rows(d_half)] * len(out_shapes)
    elif nxt == "odd":
        hp, hc = POOL_BUF * nb, (CONV_W - 1) * nb
        p0, c0 = _round_up(hp, SUBLANES), _round_up(hc, SUBLANES)
        if odd_hist is None:
            pool_buf, conv_buf = jnp.zeros((n_grp, p0, d_half), F32), jnp.zeros((n_grp, c0, d_half), F32)
        else:
            pool_buf, conv_buf = odd_hist
            assert (p0, c0) == (hp, hc)
        odd_w = [W["pool_w"], W["pool_scale"], W["conv_w"], W["conv_b"]]
        args += [pool_buf, conv_buf] + odd_w
        specs += [hist_spec(p0), hist_spec(c0)] + [_layer_spec(w, i) for w in odd_w]
        scratch = [pltpu.VMEM((p0 + tile, d_half), F32), pltpu.VMEM((c0 + tile, d_half), F32),
                   pltpu.VMEM((tile, d_half), F32)]
        out_shapes = [jax.ShapeDtypeStruct((n_grp, r, d_half), BF16)] * 2 + [
            jax.ShapeDtypeStruct((n_grp, hp, d_half), F32), jax.ShapeDtypeStruct((n_grp, hc, d_half), F32)]
        out_specs = [rows(d_half)] * 2 + [hist_spec(hp), hist_spec(hc)]
    else:
        out_shapes, out_specs = [], []
    if mix is not None or nxt == "final":
        out_shapes = [jax.ShapeDtypeStruct((n_grp, r, d), F32)] + out_shapes
        out_specs = [rows(d)] + out_specs
    n_cast = len(cast)
    cast_args, cast_in_specs, cast_shapes, cast_specs = _cast_stream(cast, n_tiles)
    args, specs = args + cast_args, specs + cast_in_specs
    out_shapes, out_specs = cast_shapes + out_shapes, cast_specs + out_specs

    outs = pl.pallas_call(
        functools.partial(_trunk_kernel, has_mix=mix is not None, nxt=nxt, d_half=d_half, nb=nb,
                          prefix_valid=prefix_valid, tiles_per_group=tpg, n_cast=n_cast),
        grid=(n_tiles,),
        in_specs=specs,
        out_specs=out_specs,
        out_shape=out_shapes,
        scratch_shapes=scratch,
        compiler_params=_params("arbitrary"),
        name=f"trunk_{'mix' if mix is not None else 'in'}_{nxt}",
    )(*args)
    return outs[n_cast:], tuple(outs[:n_cast])


def _s5_prep_kernel(lr_ref, li_ref, ldt_ref, bre_ref, bim_ref, are_ref, aim_ref, bbre_ref, bbim_ref):
    lr = lr_ref[...]
    li = li_ref[...]
    dt = jnp.exp(ldt_ref[...])
    mag = jnp.exp(lr * dt)
    ang = li * dt
    ab_re = mag * jnp.cos(ang)
    ab_im = mag * jnp.sin(ang)
    den = lr * lr + li * li
    f_re = ((ab_re - 1.0) * lr + ab_im * li) / den
    f_im = (ab_im * lr - (ab_re - 1.0) * li) / den
    are_ref[...] = ab_re
    aim_ref[...] = ab_im
    for g in range(lr.shape[0]):
        fr = f_re[g:g + 1, :]
        fi = f_im[g:g + 1, :]
        br = bre_ref[g]
        bi = bim_ref[g]
        bbre_ref[g] = fr * br - fi * bi
        bbim_ref[g] = fr * bi + fi * br


def _s5_prep(lam_re, lam_im, log_dt, b_re, b_im):
    l, g, n = lam_re.shape
    p = b_re.shape[-1]
    bt_re = jnp.swapaxes(b_re, 2, 3).reshape(l * g, p, n)
    bt_im = jnp.swapaxes(b_im, 2, 3).reshape(l * g, p, n)
    a_re, a_im, bb_re, bb_im = pl.pallas_call(
        _s5_prep_kernel,
        out_shape=[jax.ShapeDtypeStruct((l * g, n), F32)] * 2 + [jax.ShapeDtypeStruct((l * g, p, n), F32)] * 2,
        name="s5_discretize",
    )(lam_re.reshape(l * g, n), lam_im.reshape(l * g, n), log_dt.reshape(l * g, 1), bt_re, bt_im)
    gs = LANES // p
    eye = jnp.eye(gs, dtype=F32)

    def blockdiag_in(bb):
        return jnp.einsum("lkgpn,gh->lkgphn", bb.reshape(l, g // gs, gs, p, n), eye).reshape(
            l, g // gs, gs * p, gs * n).astype(BF16)

    def sublane_rows(a):
        return jnp.broadcast_to(a.reshape(l, 1, g * n), (l, SUBLANES, g * n))

    return sublane_rows(a_re), sublane_rows(a_im), blockdiag_in(bb_re), blockdiag_in(bb_im)


def _blockdiag_out(c):
    l, g, p, n = c.shape
    gs = LANES // p
    eye = jnp.eye(gs, dtype=F32)
    return jnp.einsum("lkgpn,gh->lkgnhp", c.reshape(l, g // gs, gs, p, n), eye).reshape(
        l, g // gs, gs * n, gs * p).astype(BF16)


def _s5_pieces(u_ref, prm, scratch, store_y, *, n_grp, nb, tb):
    are_ref, aim_ref, bbre_ref, bbim_ref, cre_ref, cim_ref, d_ref, gw_ref, gb_ref = prm
    ut, xre, xim, hre_s, him_s = scratch
    yt = ut
    n_seq = n_grp * nb
    n_slab, slab_in, slab_st = bbre_ref.shape

    def ring(k):
        return slice(k % 2 * slab_st, (k % 2 + 1) * slab_st)

    def load_u():
        for k in range(n_slab):
            lanes = slice(k * slab_in, (k + 1) * slab_in)
            if n_grp == 1:
                ut[k] = u_ref[0, :, lanes]
            else:
                for b in range(n_grp):
                    ut[k, pl.ds(b, tb, stride=n_grp), :] = u_ref[b, :, lanes]

    def project_in(k, x, bb_ref):
        x[:, ring(k)] = _dot(ut[k].astype(BF16), bb_ref[k])

    def read_out(k, x, c_ref, rows):
        return _dot(x[rows, ring(k)].astype(BF16), c_ref[k])

    def scan_pieces(k):
        cols, xcols = slice(k * slab_st, (k + 1) * slab_st), ring(k)
        items = [(r0, t) for r0 in range(0, n_seq, SUBLANES) for t in range(tb)]
        per = len(items) // SCAN_PIECES

        def run(chunk):
            hr = hi = cur = None
            for r0, t in chunk:
                if r0 != cur:
                    if cur is not None:
                        hre_s[cur:cur + SUBLANES, cols] = hr
                        him_s[cur:cur + SUBLANES, cols] = hi
                    hr, hi, cur = hre_s[r0:r0 + SUBLANES, cols], him_s[r0:r0 + SUBLANES, cols], r0
                ar = are_ref[:, cols]
                ai = aim_ref[:, cols]
                r = t * n_seq + r0
                nr = ar * hr - ai * hi + xre[r:r + SUBLANES, xcols]
                ni = ar * hi + ai * hr + xim[r:r + SUBLANES, xcols]
                xre[r:r + SUBLANES, xcols] = nr
                xim[r:r + SUBLANES, xcols] = ni
                hr, hi = nr, ni
            hre_s[cur:cur + SUBLANES, cols] = hr
            him_s[cur:cur + SUBLANES, cols] = hi

        return [functools.partial(run, items[p * per:(p + 1) * per]) for p in range(SCAN_PIECES)]

    rows = slice(0, tb * n_seq)
    re_im = ((xre, bbre_ref, cre_ref), (xim, bbim_ref, cim_ref))
    y_parts, box = {}, {}

    def read_out_part(k, x, c):
        y_parts.setdefault(k, []).append(read_out(k, x, c, rows))

    half = tb * n_seq // 2
    row_halves = (slice(0, half), slice(half, 2 * half))

    def gate():
        outs = []
        for rh in row_halves:
            y = [y_parts[k][0][rh] - y_parts[k][1][rh] + d_ref[:, k * slab_in:(k + 1) * slab_in] * ut[k, rh, :]
                 for k in range(n_slab)]
            z = jax.nn.gelu(jnp.concatenate(y, axis=-1))
            outs.append(z * jax.nn.sigmoid(_dot(z.astype(BF16), gw_ref[...]) + gb_ref[...]))
        box["out"] = outs

    def write(k):
        lanes = slice(k * slab_in, (k + 1) * slab_in)
        if n_grp == 1:
            store_y(0, lanes, jnp.concatenate([o[:, lanes] for o in box["out"]], axis=0).astype(BF16))
        else:
            for rh, o in zip(row_halves, box["out"]):
                yt[k, rh, :] = o[:, lanes]
            for b in range(n_grp):
                store_y(b, lanes, yt[k, pl.ds(b, tb, stride=n_grp), :].astype(BF16))

    pieces = [load_u] + [functools.partial(project_in, 0, x, bb) for x, bb, _ in re_im]
    for k in range(n_slab):
        matmuls = [functools.partial(read_out_part, k - 1, x, c) for x, _, c in re_im] if k > 0 else []
        matmuls += [functools.partial(project_in, k + 1, x, bb) for x, bb, _ in re_im] if k + 1 < n_slab else []
        scans = scan_pieces(k)
        for p, piece in enumerate(scans):
            pieces += matmuls[p * len(matmuls) // len(scans):(p + 1) * len(matmuls) // len(scans)] + [piece]
    pieces += [functools.partial(read_out_part, n_slab - 1, x, c) for x, _, c in re_im] + [gate]
    pieces += [functools.partial(write, k) for k in range(n_slab)]
    return pieces


def _s5_kernel(*refs, n_grp, nb, tb, n_cast):
    n_in = 12
    u_ref, h0re_ref, h0im_ref = refs[:3]
    y_ref, hre_out, him_out = refs[n_in + n_cast:n_in + n_cast + 3]
    scratch = refs[n_in + 2 * n_cast + 3:]
    hre_s, him_s = scratch[-2:]
    _cast_blocks(refs[n_in:n_in + n_cast], refs[n_in + n_cast + 3:n_in + 2 * n_cast + 3])

    @pl.when(pl.program_id(0) == 0)
    def _():
        hre_s[...] = h0re_ref[...]
        him_s[...] = h0im_ref[...]

    n_sub = u_ref.shape[1] // (tb * nb)
    ut_all = scratch[0]
    lists = []
    for sub in range(n_sub):
        rows = slice(sub * tb * nb, (sub + 1) * tb * nb)

        def store_y(b, lanes, y, rows=rows):
            y_ref[b, rows, lanes] = y

        lists.append(_s5_pieces(u_ref.at[:, rows, :], refs[3:n_in], (ut_all.at[sub],) + tuple(scratch[1:]),
                                store_y, n_grp=n_grp, nb=nb, tb=tb))
    n_tail = 7
    order = list(lists[0][:-n_tail])
    for sub in range(1, n_sub):
        tail, nxt = lists[sub - 1][-n_tail:], lists[sub]
        for a, b in zip(tail, nxt[:n_tail]):
            order += [a, b]
        order += nxt[n_tail:-n_tail]
    order += lists[-1][-n_tail:]
    for piece in order:
        piece()
    hre_out[...] = hre_s[...]
    him_out[...] = him_s[...]


def _s5_call(u, h0_re, h0_im, consts, layer, *, nb, cast=()):
    n_grp, r, d_a = u.shape
    assert n_grp == 1 or nb == 1
    t = r // nb
    n_seq = n_grp * nb
    n_state, slab_st = consts[0].shape[2], consts[2].shape[3]
    tb = max(1, min(t, ROW_TILE // n_seq))
    rows = tb * n_seq
    n_sub = S5_TILES_PER_STEP if (t // tb) % S5_TILES_PER_STEP == 0 else 1
    n_steps = t // tb // n_sub
    state_spec = pl.BlockSpec((n_seq, n_state), lambda i: (0, 0))
    blk = pl.BlockSpec((n_grp, n_sub * tb * nb, d_a), lambda i: (0, i, 0))
    cast_args, cast_in_specs, cast_shapes, cast_specs = _cast_stream(cast, n_steps)
    y, hre, him, *rounded = pl.pallas_call(
        functools.partial(_s5_kernel, n_grp=n_grp, nb=nb, tb=tb, n_cast=len(cast)),
        grid=(n_steps,),
        in_specs=[blk, state_spec, state_spec] + [_layer_spec(c, layer) for c in consts] + cast_in_specs,
        out_specs=[blk, state_spec, state_spec] + cast_specs,
        out_shape=[jax.ShapeDtypeStruct(u.shape, BF16),
                   jax.ShapeDtypeStruct((n_seq, n_state), F32),
                   jax.ShapeDtypeStruct((n_seq, n_state), F32)] + cast_shapes,
        scratch_shapes=[pltpu.VMEM((n_sub, d_a // LANES, rows, LANES), F32),
                        pltpu.VMEM((rows, 2 * slab_st), F32), pltpu.VMEM((rows, 2 * slab_st), F32),
                        pltpu.VMEM((n_seq, n_state), F32), pltpu.VMEM((n_seq, n_state), F32)],
        compiler_params=_params("arbitrary"),
        name="s5_mixer",
    )(u, h0_re, h0_im, *consts, *cast_args)
    return y, hre, him, tuple(rounded)


def _sgu_chunks(u, vn, w_ref, bias_ref, y_ref):
    n_heads, cl, _ = w_ref.shape
    hd = vn.shape[1] // n_heads
    row = lax.broadcasted_iota(jnp.int32, (cl, cl), 0)
    col = lax.broadcasted_iota(jnp.int32, (cl, cl), 1)
    w = [jnp.where(col <= row, w_ref[h], 0.0).astype(BF16) for h in range(n_heads)]
    heads_per_slab = LANES // hd
    lane = lax.broadcasted_iota(jnp.int32, (cl, LANES), 1)
    for c in range(vn.shape[0] // cl):
        r = slice(c * cl, (c + 1) * cl)
        slabs = []
        for s in range(n_heads // heads_per_slab):
            v = vn[r, s * LANES:(s + 1) * LANES].astype(BF16)
            mixed = _dot(w[s * heads_per_slab], v)
            for j in range(1, heads_per_slab):
                mixed = jnp.where(lane >= j * hd, _dot(w[s * heads_per_slab + j], v), mixed)
            slabs.append(mixed)
        mixed = jnp.concatenate(slabs, axis=-1) + bias_ref[...]
        y_ref[r, :] = (u[r, :] * mixed).astype(y_ref.dtype)


def _sgu_short(u, vn, wl_ref, bias_ref, y_ref, *, nb):
    for i in range(wl_ref.shape[0]):
        mixed = bias_ref[i:i + 1, :]
        for j in range(i + 1):
            mixed = mixed + wl_ref[i, j:j + 1, :] * vn[j * nb:(j + 1) * nb, :]
        y_ref[i * nb:(i + 1) * nb, :] = (u[i * nb:(i + 1) * nb, :] * mixed).astype(y_ref.dtype)


def _sgu_params(w_s, b_s, d_b, t):
    hd = d_b // w_s.shape[1]
    cl = min(t, SGU_CHUNK)
    bias = jnp.repeat(jnp.swapaxes(b_s[:, :, :cl], 1, 2), hd, axis=2)
    if t % SGU_CHUNK == 0:
        return w_s, bias
    assert t < SGU_CHUNK
    return jnp.repeat(jnp.transpose(w_s[:, :, :cl, :cl], (0, 2, 3, 1)), hd, axis=3), bias


def _interleave(a):
    return jnp.swapaxes(a, 0, 1).reshape(1, a.shape[0] * a.shape[1], a.shape[2])


def _deinterleave(a, nb):
    return jnp.swapaxes(a.reshape(-1, nb, a.shape[2]), 0, 1)


def _run_trunk(x, s5_re, s5_im, hist_bufs, prefix_valid, W, *, interleaved, rounded=None):
    batch, t, d = x.shape
    depth = W["norm_mix"].shape[0]
    nb = batch if interleaved else 1
    pack = _interleave if interleaved else (lambda a: a)
    unpack = (lambda a: _deinterleave(a, nb)) if interleaved else (lambda a: a)
    common = dict(nb=nb, prefix_valid=prefix_valid, sgu_prm=_sgu_params(W["sgu_w"], W["sgu_b"], d // 2, t))

    def kind(layer):
        return "final" if layer == depth else ("even" if layer % 2 == 0 else "odd")

    def hist(layer):
        if kind(layer) != "odd" or hist_bufs is None:
            return None
        return tuple(pack(buf[layer // 2]) for buf in hist_bufs)

    def matrices(call):
        prev, need = call - 1, []
        if call > 0:
            need += [(W["w_out_even" if prev % 2 == 0 else "w_out_odd"], prev // 2),
                     (W["ffn_gate"], prev), (W["ffn_up"], prev), (W["ffn_down"], prev)]
        if call < depth:
            need.append((W["w_in_even" if call % 2 == 0 else "w_in_odd"], call // 2))
        return need

    convert = rounded is None
    if convert:
        rounded = [[w[idx].astype(BF16) for w, idx in matrices(0)]]

    s5_rounds_first = convert and kind(0) == "even"

    def cast(call):
        return matrices(call + 1) if convert and call < depth and not (call == 0 and s5_rounds_first) else ()

    h = pack(x)
    mixer_in, nxt_w = _trunk_call(h, 0, W, mix=None, wts=rounded[0], nxt=kind(0), odd_hist=hist(0), cast=cast(0),
                                  **common)
    rounded += [nxt_w] if nxt_w else []
    new_re, new_im, new_v, new_pool, new_conv = [], [], [], [], []
    for layer in range(depth):
        i = layer // 2
        if layer % 2 == 0:
            u_a, y_b = mixer_in[:2]
            y_a, hre, him, nxt_w = _s5_call(u_a, s5_re[i].reshape(batch, -1), s5_im[i].reshape(batch, -1), W["s5"],
                                            i, nb=nb, cast=matrices(1) if layer == 0 and s5_rounds_first else ())
            rounded += [nxt_w] if nxt_w else []
            new_re.append(hre.reshape(s5_re[i].shape))
            new_im.append(him.reshape(s5_im[i].shape))
            if interleaved:
                new_v.append(unpack(mixer_in[2]))
            mix = (y_a, y_b)
        else:
            y_c, y_d, pout, cout = mixer_in
            new_pool.append(unpack(pout))
            new_conv.append(unpack(cout))
            mix = (y_c, y_d)
        outs, nxt_w = _trunk_call(h, layer + 1, W, mix=mix, wts=rounded[layer + 1], nxt=kind(layer + 1),
                                  odd_hist=hist(layer + 1), cast=cast(layer + 1), **common)
        rounded += [nxt_w] if nxt_w else []
        h, mixer_in = outs[0], outs[1:]
    return (unpack(h), jnp.stack(new_re), jnp.stack(new_im), jnp.stack(new_v) if new_v else None,
            jnp.stack(new_pool), jnp.stack(new_conv), rounded)


def kernel(x_prompt, x_sample, state_s5_re, state_s5_im, state_pool, state_conv, norm_mix, norm_ffn, norm_final, w_in_even, w_out_even, s5_lambda_re, s5_lambda_im, s5_log_dt, s5_b_re, s5_b_im, s5_c_re, s5_c_im, s5_d, s5_glu_w, s5_glu_b, sgu_norm, sgu_w, sgu_b, w_in_odd, w_out_odd, pool_w, pool_scale, conv_w, conv_b, ffn_w_gate, ffn_w_up, ffn_w_down):
    n_even, n_odd = w_in_even.shape[0], w_in_odd.shape[0]
    depth = norm_mix.shape[0]
    s5 = [*_s5_prep(s5_lambda_re, s5_lambda_im, s5_log_dt, s5_b_re, s5_b_im),
          _blockdiag_out(s5_c_re), _blockdiag_out(s5_c_im), s5_d.reshape(n_even, 1, -1),
          s5_glu_w.astype(BF16), s5_glu_b.reshape(n_even, 1, -1)]
    d = norm_mix.shape[1]
    W = dict(
        norm_mix=norm_mix.reshape(depth, 1, d), norm_ffn=norm_ffn.reshape(depth, 1, d),
        norm_final=norm_final.reshape(1, d),
        w_in_even=w_in_even, w_out_even=w_out_even, w_in_odd=w_in_odd, w_out_odd=w_out_odd,
        sgu_norm=sgu_norm.reshape(n_even, 1, -1), sgu_w=sgu_w, sgu_b=sgu_b, s5=s5,
        pool_w=pool_w.astype(BF16), pool_scale=pool_scale.reshape(n_odd, 1, -1), conv_w=conv_w,
        conv_b=conv_b.reshape(n_odd, 1, -1),
        ffn_gate=ffn_w_gate, ffn_up=ffn_w_up, ffn_down=ffn_w_down,
    )
    bp = x_prompt.shape[0]
    z_s5 = jnp.zeros((n_even, bp) + state_s5_re.shape[2:], state_s5_re.dtype)
    y_p, p_re, p_im, _, p_pool, p_conv, rounded = _run_trunk(x_prompt, z_s5, z_s5, None, False, W,
                                                             interleaved=False)
    y_s, s_re, s_im, s_v, s_pool, s_conv, _ = _run_trunk(
        x_sample, state_s5_re, state_s5_im, (state_pool, state_conv), True, W, interleaved=True, rounded=rounded)
    return (y_p, y_s, p_re, p_im, p_pool, p_conv, s_re, s_im, s_v, s_pool, s_conv)
```

```python
import functools

import jax
import jax.numpy as jnp
from jax import lax
from jax.experimental import pallas as pl
from jax.experimental.pallas import tpu as pltpu

F32 = jnp.float32
BF16 = jnp.bfloat16

EPS = 1e-6
SGU_CHUNK = 128
POOL_WINDOWS = (2, 4, 8, 16)
POOL_BUF = max(POOL_WINDOWS) - 1
CONV_W = 3

LANES = 128
SUBLANES = 8
BF16_ROWS = 2 * SUBLANES
V7X_VMEM_BYTES = 64 * 1024 * 1024
ROW_TILE = 512
FF_CHUNK = 256
SCAN_PIECES = 4
S5_TILES_PER_STEP = 2
VMEM_LIMIT = V7X_VMEM_BYTES * 7 // 8


def _dot(a, b):
    return jnp.dot(a, b, preferred_element_type=F32)


def _rms(x, g):
    return x * lax.rsqrt(jnp.mean(x * x, axis=-1, keepdims=True) + EPS) * g


def _round_up(n, m):
    return -(-n // m) * m


def _const_spec(shape):
    zeros = (0,) * len(shape)
    return pl.BlockSpec(shape, lambda *_: zeros, pipeline_mode=pl.Buffered(1))


def _layer_spec(stack, layer):
    zeros = (0,) * (stack.ndim - 1)
    return pl.BlockSpec((None,) + stack.shape[1:], lambda *_: (layer,) + zeros, pipeline_mode=pl.Buffered(1))


def _params(*semantics):
    return pltpu.CompilerParams(dimension_semantics=semantics, vmem_limit_bytes=VMEM_LIMIT)


def _poolconv_pieces(xc, z, bg, i, pbuf_ref, cbuf_ref, pw_ref, ps_ref, cw_ref, cb_ref,
                     yc_ref, yd_ref, pout_ref, cout_ref, fullc, fullz, diff, *, nb, prefix_valid):
    hp, hc = POOL_BUF * nb, (CONV_W - 1) * nb
    p0, c0 = pbuf_ref.shape[0], cbuf_ref.shape[0]
    r = xc.shape[0]
    cg = xc.shape[1] // len(POOL_WINDOWS)
    first = i == 0

    def load_tile():
        fullc[0:p0, :] = jnp.where(first, pbuf_ref[...], fullc[0:p0, :])
        fullz[0:c0, :] = jnp.where(first, cbuf_ref[...], fullz[0:c0, :])
        fullc[p0:p0 + r, :] = xc[...]
        fullz[c0:c0 + r, :] = z[...]

    def conv():
        acc = cb_ref[...]
        for k in range(CONV_W):
            back = (CONV_W - 1 - k) * nb
            acc = acc + fullz[c0 - back:c0 - back + r, :] * cw_ref[k:k + 1, :]
        yd_ref[...] = (bg[...] * acc).astype(yd_ref.dtype)

    def window_sum(rows, lanes, w):
        s = fullc[p0:p0 + rows, lanes]
        for k in range(1, w):
            s = s + fullc[p0 - k * nb:p0 - k * nb + rows, lanes]
        return s

    def pool(gi, w):
        lanes = slice(gi * cg, (gi + 1) * cg)
        diff[:, lanes] = window_sum(r, lanes, w) * (1.0 / w) - fullc[p0:p0 + r, lanes]
        if not prefix_valid:
            head = min(p0, r)
            t = lax.broadcasted_iota(jnp.int32, (head, 1), 0) // nb
            n = jnp.minimum(t + 1, w).astype(F32)
            short = window_sum(head, lanes, w) / n - fullc[p0:p0 + head, lanes]
            diff[0:head, lanes] = jnp.where(first, short, diff[0:head, lanes])

    def project():
        for gi in range(len(POOL_WINDOWS)):
            lanes = slice(gi * cg, (gi + 1) * cg)
            yc = _dot(diff[:, lanes].astype(BF16), pw_ref[gi]) * ps_ref[:, lanes]
            yc_ref[:, lanes] = yc.astype(yc_ref.dtype)

    def shift_history():
        pout_ref[...] = fullc[p0 + r - hp:p0 + r, :]
        cout_ref[...] = fullz[c0 + r - hc:c0 + r, :]
        new_pool = fullc[r:r + p0, :]
        new_conv = fullz[r:r + c0, :]
        fullc[0:p0, :] = new_pool
        fullz[0:c0, :] = new_conv

    pools = [functools.partial(pool, gi, w) for gi, w in enumerate(POOL_WINDOWS)]
    return [load_tile, conv] + pools + [project, shift_history]


def _trunk_kernel(*refs, has_mix, nxt, d_half, nb, prefix_valid, tiles_per_group, n_cast):
    refs = list(refs)
    h_ref = refs.pop(0)
    if has_mix:
        ya_ref, yb_ref, wo_ref, gf_ref, wg_ref, wu_ref, wd_ref = refs[:7]
        del refs[:7]
    gn_ref = refs.pop(0)
    if nxt != "final":
        win_ref = refs.pop(0)
    if nxt == "even":
        gs_ref, sw_ref, sb_ref = refs[:3]
        del refs[:3]
    if nxt == "odd":
        odd_in = refs[:6]
        del refs[:6]
    cast_in = refs[:n_cast]
    del refs[:n_cast]
    outs = refs
    _cast_blocks(cast_in, outs[:n_cast])
    del outs[:n_cast]
    s = pl.program_id(0)
    if nxt == "odd":
        odd_out = outs[-7:]

        @pl.when(s == 0)
        def _():
            for ref in odd_out[-3:]:
                ref[...] = jnp.zeros(ref.shape, ref.dtype)

    h = h_ref[...]
    if has_mix:
        h = h + _dot(ya_ref[...], wo_ref[:d_half, :]) + _dot(yb_ref[...], wo_ref[d_half:, :])
        hn = _rms(h, gf_ref[...]).astype(BF16)
        d_ff = wg_ref.shape[1]
        acc = None
        for c0 in range(0, d_ff, FF_CHUNK):
            c1 = min(c0 + FF_CHUNK, d_ff)
            g = _dot(hn, wg_ref[:, c0:c1])
            u = _dot(hn, wu_ref[:, c0:c1])
            a = (g * jax.nn.sigmoid(g) * u).astype(BF16)
            part = _dot(a, wd_ref[c0:c1, :])
            acc = part if acc is None else acc + part
        h = h + acc
        if nxt != "final":
            outs.pop(0)[...] = h
    hn = _rms(h, gn_ref[...])
    if nxt == "final":
        outs[0][...] = hn
        return
    proj = _dot(hn.astype(BF16), win_ref[...])
    if nxt == "even":
        ua_ref, yb_ref = outs[:2]
        ua_ref[...] = proj[:, :d_half]
        ub = proj[:, d_half:2 * d_half]
        vn = _rms(proj[:, 2 * d_half:], gs_ref[...])
        if nb == 1:
            _sgu_chunks(ub, vn, sw_ref, sb_ref, yb_ref)
        else:
            _sgu_short(ub, vn, sw_ref, sb_ref, yb_ref, nb=nb)
            outs[2][...] = vn
    else:
        xc = proj[:, :d_half]
        z = proj[:, 3 * d_half:] * proj[:, d_half:2 * d_half]
        bg = proj[:, 2 * d_half:3 * d_half]
        for piece in _poolconv_pieces(xc, z, bg, s % tiles_per_group, *odd_in, *odd_out, nb=nb,
                                      prefix_valid=prefix_valid):
            piece()


def _cast_stream(cast, n_steps):
    args, in_specs, out_shapes, out_specs = [], [], [], []
    for w, idx in cast:
        n_rows, n_cols = w.shape[1:]
        rpb = next(k for k in range(BF16_ROWS, n_rows + 1, BF16_ROWS)
                   if n_rows % k == 0 and n_rows // k <= n_steps)
        n_blk = n_rows // rpb
        args.append(w)
        in_specs.append(pl.BlockSpec((None, rpb, n_cols),
                                     lambda s, n_blk=n_blk, idx=idx: (idx, s * n_blk // n_steps, 0)))
        out_shapes.append(jax.ShapeDtypeStruct((n_rows, n_cols), BF16))
        out_specs.append(pl.BlockSpec((rpb, n_cols), lambda s, n_blk=n_blk: (s * n_blk // n_steps, 0)))
    return args, in_specs, out_shapes, out_specs


def _cast_blocks(srcs, dsts):
    for src, dst in zip(srcs, dsts):
        dst[...] = src[...].astype(dst.dtype)


def _trunk_call(h, layer, W, *, mix, wts, nxt, nb, prefix_valid, sgu_prm, cast=(), odd_hist=None):
    n_grp, r, d = h.shape
    d_half = d // 2
    tile = min(ROW_TILE if mix is not None else 2 * ROW_TILE, r)
    i = layer // 2
    tpg = r // tile
    n_tiles = n_grp * tpg

    def rows(c):
        return pl.BlockSpec((None, tile, c), lambda s: (s // tpg, s % tpg, 0))

    def hist_spec(n_rows):
        return pl.BlockSpec((None, n_rows, d_half), lambda s: (s // tpg, 0, 0))

    args, specs = [h], [rows(d)]
    if mix is not None:
        ya, yb = mix
        w_out, gate, up, down = wts[:4]
        args += [ya, yb, w_out, W["norm_ffn"], gate, up, down]
        specs += [rows(d_half), rows(d_half), _const_spec(w_out.shape), _layer_spec(W["norm_ffn"], layer - 1)]
        specs += [_const_spec(w.shape) for w in (gate, up, down)]
    if nxt == "final":
        args.append(W["norm_final"])
        specs.append(_const_spec(W["norm_final"].shape))
    else:
        args += [W["norm_mix"], wts[-1]]
        specs += [_layer_spec(W["norm_mix"], layer), _const_spec(wts[-1].shape)]
    scratch = []
    if nxt == "even":
        sgu = [W["sgu_norm"], *sgu_prm]
        args += sgu
        specs += [_layer_spec(w, i) for w in sgu]
        assert (nb == 1 and tile % SGU_CHUNK == 0) or (n_tiles == 1 and tile // nb < SGU_CHUNK)
        out_shapes = [jax.ShapeDtypeStruct((n_grp, r, d_half), F32), jax.ShapeDtypeStruct((n_grp, r, d_half), BF16)]
        if nb > 1:
            out_shapes.append(jax.ShapeDtypeStruct((n_grp, r, d_half), F32))
        out_specs = [rows(d_half)] * len(out_shapes)
    elif nxt == "odd":
        hp, hc = POOL_BUF * nb, (CONV_W - 1) * nb
        p0, c0 = _round_up(hp, SUBLANES), _round_up(hc, SUBLANES)
        if odd_hist is None:
            pool_buf, conv_buf = jnp.zeros((n_grp, p0, d_half), F32), jnp.zeros((n_grp, c0, d_half), F32)
        else:
            pool_buf, conv_buf = odd_hist
            assert (p0, c0) == (hp, hc)
        odd_w = [W["pool_w"], W["pool_scale"], W["conv_w"], W["conv_b"]]
        args += [pool_buf, conv_buf] + odd_w
        specs += [hist_spec(p0), hist_spec(c0)] + [_layer_spec(w, i) for w in odd_w]
        scratch = [pltpu.VMEM((p0 + tile, d_half), F32), pltpu.VMEM((c0 + tile, d_half), F32),
                   pltpu.VMEM((tile, d_half), F32)]
        out_shapes = [jax.ShapeDtypeStruct((n_grp, r, d_half), BF16)] * 2 + [
            jax.ShapeDtypeStruct((n_grp, hp, d_half), F32), jax.ShapeDtypeStruct((n_grp, hc, d_half), F32)]
        out_specs = [rows(d_half)] * 2 + [hist_spec(hp), hist_spec(hc)]
    else:
        out_shapes, out_specs = [], []
    if mix is not None or nxt == "final":
        out_shapes = [jax.ShapeDtypeStruct((n_grp, r, d), F32)] + out_shapes
        out_specs = [rows(d)] + out_specs
    n_cast = len(cast)
    cast_args, cast_in_specs, cast_shapes, cast_specs = _cast_stream(cast, n_tiles)
    args, specs = args + cast_args, specs + cast_in_specs
    out_shapes, out_specs = cast_shapes + out_shapes, cast_specs + out_specs

    outs = pl.pallas_call(
        functools.partial(_trunk_kernel, has_mix=mix is not None, nxt=nxt, d_half=d_half, nb=nb,
                          prefix_valid=prefix_valid, tiles_per_group=tpg, n_cast=n_cast),
        grid=(n_tiles,),
        in_specs=specs,
        out_specs=out_specs,
        out_shape=out_shapes,
        scratch_shapes=scratch,
        compiler_params=_params("arbitrary"),
        name=f"trunk_{'mix' if mix is not None else 'in'}_{nxt}",
    )(*args)
    return outs[n_cast:], tuple(outs[:n_cast])


def _s5_prep_kernel(lr_ref, li_ref, ldt_ref, bre_ref, bim_ref, are_ref, aim_ref, bbre_ref, bbim_ref):
    lr = lr_ref[...]
    li = li_ref[...]
    dt = jnp.exp(ldt_ref[...])
    mag = jnp.exp(lr * dt)
    ang = li * dt
    ab_re = mag * jnp.cos(ang)
    ab_im = mag * jnp.sin(ang)
    den = lr * lr + li * li
    f_re = ((ab_re - 1.0) * lr + ab_im * li) / den
    f_im = (ab_im * lr - (ab_re - 1.0) * li) / den
    are_ref[...] = ab_re
    aim_ref[...] = ab_im
    for g in range(lr.shape[0]):
        fr = f_re[g:g + 1, :]
        fi = f_im[g:g + 1, :]
        br = bre_ref[g]
        bi = bim_ref[g]
        bbre_ref[g] = fr * br - fi * bi
        bbim_ref[g] = fr * bi + fi * br


def _s5_prep(lam_re, lam_im, log_dt, b_re, b_im):
    l, g, n = lam_re.shape
    p = b_re.shape[-1]
    bt_re = jnp.swapaxes(b_re, 2, 3).reshape(l * g, p, n)
    bt_im = jnp.swapaxes(b_im, 2, 3).reshape(l * g, p, n)
    a_re, a_im, bb_re, bb_im = pl.pallas_call(
        _s5_prep_kernel,
        out_shape=[jax.ShapeDtypeStruct((l * g, n), F32)] * 2 + [jax.ShapeDtypeStruct((l * g, p, n), F32)] * 2,
        name="s5_discretize",
    )(lam_re.reshape(l * g, n), lam_im.reshape(l * g, n), log_dt.reshape(l * g, 1), bt_re, bt_im)
    gs = LANES // p
    eye = jnp.eye(gs, dtype=F32)

    def blockdiag_in(bb):
        return jnp.einsum("lkgpn,gh->lkgphn", bb.reshape(l, g // gs, gs, p, n), eye).reshape(
            l, g // gs, gs * p, gs * n).astype(BF16)

    def sublane_rows(a):
        return jnp.broadcast_to(a.reshape(l, 1, g * n), (l, SUBLANES, g * n))

    return sublane_rows(a_re), sublane_rows(a_im), blockdiag_in(bb_re), blockdiag_in(bb_im)


def _blockdiag_out(c):
    l, g, p, n = c.shape
    gs = LANES // p
    eye = jnp.eye(gs, dtype=F32)
    return jnp.einsum("lkgpn,gh->lkgnhp", c.reshape(l, g // gs, gs, p, n), eye).reshape(
        l, g // gs, gs * n, gs * p).astype(BF16)


def _s5_pieces(u_ref, prm, scratch, store_y, *, n_grp, nb, tb):
    are_ref, aim_ref, bbre_ref, bbim_ref, cre_ref, cim_ref, d_ref, gw_ref, gb_ref = prm
    ut, xre, xim, hre_s, him_s = scratch
    yt = ut
    n_seq = n_grp * nb
    n_slab, slab_in, slab_st = bbre_ref.shape

    def ring(k):
        return slice(k % 2 * slab_st, (k % 2 + 1) * slab_st)

    def load_u():
        for k in range(n_slab):
            lanes = slice(k * slab_in, (k + 1) * slab_in)
            if n_grp == 1:
                ut[k] = u_ref[0, :, lanes]
            else:
                for b in range(n_grp):
                    ut[k, pl.ds(b, tb, stride=n_grp), :] = u_ref[b, :, lanes]

    def project_in(k, x, bb_ref):
        x[:, ring(k)] = _dot(ut[k].astype(BF16), bb_ref[k])

    def read_out(k, x, c_ref, rows):
        return _dot(x[rows, ring(k)].astype(BF16), c_ref[k])

    def scan_pieces(k):
        cols, xcols = slice(k * slab_st, (k + 1) * slab_st), ring(k)
        items = [(r0, t) for r0 in range(0, n_seq, SUBLANES) for t in range(tb)]
        per = len(items) // SCAN_PIECES

        def run(chunk):
            hr = hi = cur = None
            for r0, t in chunk:
                if r0 != cur:
                    if cur is not None:
                        hre_s[cur:cur + SUBLANES, cols] = hr
                        him_s[cur:cur + SUBLANES, cols] = hi
                    hr, hi, cur = hre_s[r0:r0 + SUBLANES, cols], him_s[r0:r0 + SUBLANES, cols], r0
                ar = are_ref[:, cols]
                ai = aim_ref[:, cols]
                r = t * n_seq + r0
                nr = ar * hr - ai * hi + xre[r:r + SUBLANES, xcols]
                ni = ar * hi + ai * hr + xim[r:r + SUBLANES, xcols]
                xre[r:r + SUBLANES, xcols] = nr
                xim[r:r + SUBLANES, xcols] = ni
                hr, hi = nr, ni
            hre_s[cur:cur + SUBLANES, cols] = hr
            him_s[cur:cur + SUBLANES, cols] = hi

        return [functools.partial(run, items[p * per:(p + 1) * per]) for p in range(SCAN_PIECES)]

    rows = slice(0, tb * n_seq)
    re_im = ((xre, bbre_ref, cre_ref), (xim, bbim_ref, cim_ref))
    y_parts, box = {}, {}

    def read_out_part(k, x, c):
        y_parts.setdefault(k, []).append(read_out(k, x, c, rows))

    half = tb * n_seq // 2
    row_halves = (slice(0, half), slice(half, 2 * half))

    def gate():
        outs = []
        for rh in row_halves:
            y = [y_parts[k][0][rh] - y_parts[k][1][rh] + d_ref[:, k * slab_in:(k + 1) * slab_in] * ut[k, rh, :]
                 for k in range(n_slab)]
            z = jax.nn.gelu(jnp.concatenate(y, axis=-1))
            outs.append(z * jax.nn.sigmoid(_dot(z.astype(BF16), gw_ref[...]) + gb_ref[...]))
        box["out"] = outs

    def write(k):
        lanes = slice(k * slab_in, (k + 1) * slab_in)
        if n_grp == 1:
            store_y(0, lanes, jnp.concatenate([o[:, lanes] for o in box["out"]], axis=0).astype(BF16))
        else:
            for rh, o in zip(row_halves, box["out"]):
                yt[k, rh, :] = o[:, lanes]
            for b in range(n_grp):
                store_y(b, lanes, yt[k, pl.ds(b, tb, stride=n_grp), :].astype(BF16))

    pieces = [load_u] + [functools.partial(project_in, 0, x, bb) for x, bb, _ in re_im]
    for k in range(n_slab):
        matmuls = [functools.partial(read_out_part, k - 1, x, c) for x, _, c in re_im] if k > 0 else []
        matmuls += [functools.partial(project_in, k + 1, x, bb) for x, bb, _ in re_im] if k + 1 < n_slab else []
        scans = scan_pieces(k)
        for p, piece in enumerate(scans):
            pieces += matmuls[p * len(matmuls) // len(scans):(p + 1) * len(matmuls) // len(scans)] + [piece]
    pieces += [functools.partial(read_out_part, n_slab - 1, x, c) for x, _, c in re_im] + [gate]
    pieces += [functools.partial(write, k) for k in range(n_slab)]
    return pieces


def _s5_kernel(*refs, n_grp, nb, tb, n_cast):
    n_in = 12
    u_ref, h0re_ref, h0im_ref = refs[:3]
    y_ref, hre_out, him_out = refs[n_in + n_cast:n_in + n_cast + 3]
    scratch = refs[n_in + 2 * n_cast + 3:]
    hre_s, him_s = scratch[-2:]
    _cast_blocks(refs[n_in:n_in + n_cast], refs[n_in + n_cast + 3:n_in + 2 * n_cast + 3])

    @pl.when(pl.program_id(0) == 0)
    def _():
        hre_s[...] = h0re_ref[...]
        him_s[...] = h0im_ref[...]

    n_sub = u_ref.shape[1] // (tb * nb)
    ut_all = scratch[0]
    lists = []
    for sub in range(n_sub):
        rows = slice(sub * tb * nb, (sub + 1) * tb * nb)

        def store_y(b, lanes, y, rows=rows):
            y_ref[b, rows, lanes] = y

        lists.append(_s5_pieces(u_ref.at[:, rows, :], refs[3:n_in], (ut_all.at[sub],) + tuple(scratch[1:]),
                                store_y, n_grp=n_grp, nb=nb, tb=tb))
    n_tail = 7
    order = list(lists[0][:-n_tail])
    for sub in range(1, n_sub):
        tail, nxt = lists[sub - 1][-n_tail:], lists[sub]
        for a, b in zip(tail, nxt[:n_tail]):
            order += [a, b]
        order += nxt[n_tail:-n_tail]
    order += lists[-1][-n_tail:]
    for piece in order:
        piece()
    hre_out[...] = hre_s[...]
    him_out[...] = him_s[...]


def _s5_call(u, h0_re, h0_im, consts, layer, *, nb, cast=()):
    n_grp, r, d_a = u.shape
    assert n_grp == 1 or nb == 1
    t = r // nb
    n_seq = n_grp * nb
    n_state, slab_st = consts[0].shape[2], consts[2].shape[3]
    tb = max(1, min(t, ROW_TILE // n_seq))
    rows = tb * n_seq
    n_sub = S5_TILES_PER_STEP if (t // tb) % S5_TILES_PER_STEP == 0 else 1
    n_steps = t // tb // n_sub
    state_spec = pl.BlockSpec((n_seq, n_state), lambda i: (0, 0))
    blk = pl.BlockSpec((n_grp, n_sub * tb * nb, d_a), lambda i: (0, i, 0))
    cast_args, cast_in_specs, cast_shapes, cast_specs = _cast_stream(cast, n_steps)
    y, hre, him, *rounded = pl.pallas_call(
        functools.partial(_s5_kernel, n_grp=n_grp, nb=nb, tb=tb, n_cast=len(cast)),
        grid=(n_steps,),
        in_specs=[blk, state_spec, state_spec] + [_layer_spec(c, layer) for c in consts] + cast_in_specs,
        out_specs=[blk, state_spec, state_spec] + cast_specs,
        out_shape=[jax.ShapeDtypeStruct(u.shape, BF16),
                   jax.ShapeDtypeStruct((n_seq, n_state), F32),
                   jax.ShapeDtypeStruct((n_seq, n_state), F32)] + cast_shapes,
        scratch_shapes=[pltpu.VMEM((n_sub, d_a // LANES, rows, LANES), F32),
                        pltpu.VMEM((rows, 2 * slab_st), F32), pltpu.VMEM((rows, 2 * slab_st), F32),
                        pltpu.VMEM((n_seq, n_state), F32), pltpu.VMEM((n_seq, n_state), F32)],
        compiler_params=_params("arbitrary"),
        name="s5_mixer",
    )(u, h0_re, h0_im, *consts, *cast_args)
    return y, hre, him, tuple(rounded)


def _sgu_chunks(u, vn, w_ref, bias_ref, y_ref):
    n_heads, cl, _ = w_ref.shape
    hd = vn.shape[1] // n_heads
    row = lax.broadcasted_iota(jnp.int32, (cl, cl), 0)
    col = lax.broadcasted_iota(jnp.int32, (cl, cl), 1)
    w = [jnp.where(col <= row, w_ref[h], 0.0).astype(BF16) for h in range(n_heads)]
    heads_per_slab = LANES // hd
    lane = lax.broadcasted_iota(jnp.int32, (cl, LANES), 1)
    for c in range(vn.shape[0] // cl):
        r = slice(c * cl, (c + 1) * cl)
        slabs = []
        for s in range(n_heads // heads_per_slab):
            v = vn[r, s * LANES:(s + 1) * LANES].astype(BF16)
            mixed = _dot(w[s * heads_per_slab], v)
            for j in range(1, heads_per_slab):
                mixed = jnp.where(lane >= j * hd, _dot(w[s * heads_per_slab + j], v), mixed)
            slabs.append(mixed)
        mixed = jnp.concatenate(slabs, axis=-1) + bias_ref[...]
        y_ref[r, :] = (u[r, :] * mixed).astype(y_ref.dtype)


def _sgu_short(u, vn, wl_ref, bias_ref, y_ref, *, nb):
    for i in range(wl_ref.shape[0]):
        mixed = bias_ref[i:i + 1, :]
        for j in range(i + 1):
            mixed = mixed + wl_ref[i, j:j + 1, :] * vn[j * nb:(j + 1) * nb, :]
        y_ref[i * nb:(i + 1) * nb, :] = (u[i * nb:(i + 1) * nb, :] * mixed).astype(y_ref.dtype)


def _sgu_params(w_s, b_s, d_b, t):
    hd = d_b // w_s.shape[1]
    cl = min(t, SGU_CHUNK)
    bias = jnp.repeat(jnp.swapaxes(b_s[:, :, :cl], 1, 2), hd, axis=2)
    if t % SGU_CHUNK == 0:
        return w_s, bias
    assert t < SGU_CHUNK
    return jnp.repeat(jnp.transpose(w_s[:, :, :cl, :cl], (0, 2, 3, 1)), hd, axis=3), bias


def _interleave(a):
    return jnp.swapaxes(a, 0, 1).reshape(1, a.shape[0] * a.shape[1], a.shape[2])


def _deinterleave(a, nb):
    return jnp.swapaxes(a.reshape(-1, nb, a.shape[2]), 0, 1)


def _run_trunk(x, s5_re, s5_im, hist_bufs, prefix_valid, W, *, interleaved, rounded=None):
    batch, t, d = x.shape
    depth = W["norm_mix"].shape[0]
    nb = batch if interleaved else 1
    pack = _interleave if interleaved else (lambda a: a)
    unpack = (lambda a: _deinterleave(a, nb)) if interleaved else (lambda a: a)
    common = dict(nb=nb, prefix_valid=prefix_valid, sgu_prm=_sgu_params(W["sgu_w"], W["sgu_b"], d // 2, t))

    def kind(layer):
        return "final" if layer == depth else ("even" if layer % 2 == 0 else "odd")

    def hist(layer):
        if kind(layer) != "odd" or hist_bufs is None:
            return None
        return tuple(pack(buf[layer // 2]) for buf in hist_bufs)

    def matrices(call):
        prev, need = call - 1, []
        if call > 0:
            need += [(W["w_out_even" if prev % 2 == 0 else "w_out_odd"], prev // 2),
                     (W["ffn_gate"], prev), (W["ffn_up"], prev), (W["ffn_down"], prev)]
        if call < depth:
            need.append((W["w_in_even" if call % 2 == 0 else "w_in_odd"], call // 2))
        return need

    convert = rounded is None
    if convert:
        rounded = [[w[idx].astype(BF16) for w, idx in matrices(0)]]

    s5_rounds_first = convert and kind(0) == "even"

    def cast(call):
        return matrices(call + 1) if convert and call < depth and not (call == 0 and s5_rounds_first) else ()

    h = pack(x)
    mixer_in, nxt_w = _trunk_call(h, 0, W, mix=None, wts=rounded[0], nxt=kind(0), odd_hist=hist(0), cast=cast(0),
                                  **common)
    rounded += [nxt_w] if nxt_w else []
    new_re, new_im, new_v, new_pool, new_conv = [], [], [], [], []
    for layer in range(depth):
        i = layer // 2
        if layer % 2 == 0:
            u_a, y_b = mixer_in[:2]
            y_a, hre, him, nxt_w = _s5_call(u_a, s5_re[i].reshape(batch, -1), s5_im[i].reshape(batch, -1), W["s5"],
                                            i, nb=nb, cast=matrices(1) if layer == 0 and s5_rounds_first else ())
            rounded += [nxt_w] if nxt_w else []
            new_re.append(hre.reshape(s5_re[i].shape))
            new_im.append(him.reshape(s5_im[i].shape))
            if interleaved:
                new_v.append(unpack(mixer_in[2]))
            mix = (y_a, y_b)
        else:
            y_c, y_d, pout, cout = mixer_in
            new_pool.append(unpack(pout))
            new_conv.append(unpack(cout))
            mix = (y_c, y_d)
        outs, nxt_w = _trunk_call(h, layer + 1, W, mix=mix, wts=rounded[layer + 1], nxt=kind(layer + 1),
                                  odd_hist=hist(layer + 1), cast=cast(layer + 1), **common)
        rounded += [nxt_w] if nxt_w else []
        h, mixer_in = outs[0], outs[1:]
    return (unpack(h), jnp.stack(new_re), jnp.stack(new_im), jnp.stack(new_v) if new_v else None,
            jnp.stack(new_pool), jnp.stack(new_conv), rounded)


def kernel(x_prompt, x_sample, state_s5_re, state_s5_im, state_pool, state_conv, norm_mix, norm_ffn, norm_final, w_in_even, w_out_even, s5_lambda_re, s5_lambda_im, s5_log_dt, s5_b_re, s5_b_im, s5_c_re, s5_c_im, s5_d, s5_glu_w, s5_glu_b, sgu_norm, sgu_w, sgu_b, w_in_odd, w_out_odd, pool_w, pool_scale, conv_w, conv_b, ffn_w_gate, ffn_w_up, ffn_w_down):
    n_even, n_odd = w_in_even.shape[0], w_in_odd.shape[0]
    depth = norm_mix.shape[0]
    s5 = [*_s5_prep(s5_lambda_re, s5_lambda_im, s5_log_dt, s5_b_re, s5_b_im),
          _blockdiag_out(s5_c_re), _blockdiag_out(s5_c_im), s5_d.reshape(n_even, 1, -1),
          s5_glu_w.astype(BF16), s5_glu_b.reshape(n_even, 1, -1)]
    d = norm_mix.shape[1]
    W = dict(
        norm_mix=norm_mix.reshape(depth, 1, d), norm_ffn=norm_ffn.reshape(depth, 1, d),
        norm_final=norm_final.reshape(1, d),
        w_in_even=w_in_even, w_out_even=w_out_even, w_in_odd=w_in_odd, w_out_odd=w_out_odd,
        sgu_norm=sgu_norm.reshape(n_even, 1, -1), sgu_w=sgu_w, sgu_b=sgu_b, s5=s5,
        pool_w=pool_w.astype(BF16), pool_scale=pool_scale.reshape(n_odd, 1, -1), conv_w=conv_w,
        conv_b=conv_b.reshape(n_odd, 1, -1),
        ffn_gate=ffn_w_gate, ffn_up=ffn_w_up, ffn_down=ffn_w_down,
    )
    bp = x_prompt.shape[0]
    z_s5 = jnp.zeros((n_even, bp) + state_s5_re.shape[2:], state_s5_re.dtype)
    y_p, p_re, p_im, _, p_pool, p_conv, rounded = _run_trunk(x_prompt, z_s5, z_s5, None, False, W,
                                                             interleaved=False)
    y_s, s_re, s_im, s_v, s_pool, s_conv, _ = _run_trunk(
        x_sample, state_s5_re, state_s5_im, (state_pool, state_conv), True, W, interleaved=True, rounded=rounded)
    return (y_p, y_s, p_re, p_im, p_pool, p_conv, s_re, s_im, s_v, s_pool, s_conv)
```

```python
import functools

import jax
import jax.numpy as jnp
from jax import lax
from jax.experimental import pallas as pl
from jax.experimental.pallas import tpu as pltpu

F32 = jnp.float32
BF16 = jnp.bfloat16

EPS = 1e-6
SGU_CHUNK = 128
POOL_WINDOWS = (2, 4, 8, 16)
POOL_BUF = max(POOL_WINDOWS) - 1
CONV_W = 3

LANES = 128
SUBLANES = 8
BF16_ROWS = 2 * SUBLANES
V7X_VMEM_BYTES = 64 * 1024 * 1024
ROW_TILE = 512
FF_CHUNK = 256
SCAN_PIECES = 4
S5_TILES_PER_STEP = 2
VMEM_LIMIT = V7X_VMEM_BYTES * 7 // 8


def _dot(a, b):
    return jnp.dot(a, b, preferred_element_type=F32)


def _rms(x, g):
    return x * lax.rsqrt(jnp.mean(x * x, axis=-1, keepdims=True) + EPS) * g


def _round_up(n, m):
    return -(-n // m) * m


def _const_spec(shape):
    zeros = (0,) * len(shape)
    return pl.BlockSpec(shape, lambda *_: zeros, pipeline_mode=pl.Buffered(1))


def _layer_spec(stack, layer):
    zeros = (0,) * (stack.ndim - 1)
    return pl.BlockSpec((None,) + stack.shape[1:], lambda *_: (layer,) + zeros, pipeline_mode=pl.Buffered(1))


def _params(*semantics):
    return pltpu.CompilerParams(dimension_semantics=semantics, vmem_limit_bytes=VMEM_LIMIT)


def _poolconv_pieces(xc, z, bg, i, pbuf_ref, cbuf_ref, pw_ref, ps_ref, cw_ref, cb_ref,
                     yc_ref, yd_ref, pout_ref, cout_ref, fullc, fullz, diff, *, nb, prefix_valid):
    hp, hc = POOL_BUF * nb, (CONV_W - 1) * nb
    p0, c0 = pbuf_ref.shape[0], cbuf_ref.shape[0]
    r = xc.shape[0]
    cg = xc.shape[1] // len(POOL_WINDOWS)
    first = i == 0

    def load_pool():
        fullc[0:p0, :] = jnp.where(first, pbuf_ref[...], fullc[0:p0, :])
        fullc[p0:p0 + r, :] = xc

    def load_conv():
        fullz[0:c0, :] = jnp.where(first, cbuf_ref[...], fullz[0:c0, :])
        fullz[c0:c0 + r, :] = z()

    def conv():
        acc = cb_ref[...]
        for k in range(CONV_W):
            back = (CONV_W - 1 - k) * nb
            acc = acc + fullz[c0 - back:c0 - back + r, :] * cw_ref[k:k + 1, :]
        yd_ref[...] = (bg() * acc).astype(yd_ref.dtype)

    def window_sum(rows, lanes, w):
        s = fullc[p0:p0 + rows, lanes]
        for k in range(1, w):
            s = s + fullc[p0 - k * nb:p0 - k * nb + rows, lanes]
        return s

    def pool(gi, w):
        lanes = slice(gi * cg, (gi + 1) * cg)
        diff[:, lanes] = window_sum(r, lanes, w) * (1.0 / w) - fullc[p0:p0 + r, lanes]
        if not prefix_valid:
            head = min(p0, r)
            t = lax.broadcasted_iota(jnp.int32, (head, 1), 0) // nb
            n = jnp.minimum(t + 1, w).astype(F32)
            short = window_sum(head, lanes, w) / n - fullc[p0:p0 + head, lanes]
            diff[0:head, lanes] = jnp.where(first, short, diff[0:head, lanes])

    def project():
        for gi in range(len(POOL_WINDOWS)):
            lanes = slice(gi * cg, (gi + 1) * cg)
            yc = _dot(diff[:, lanes].astype(BF16), pw_ref[gi]) * ps_ref[:, lanes]
            yc_ref[:, lanes] = yc.astype(yc_ref.dtype)

    def shift_history():
        pout_ref[...] = fullc[p0 + r - hp:p0 + r, :]
        cout_ref[...] = fullz[c0 + r - hc:c0 + r, :]
        new_pool = fullc[r:r + p0, :]
        new_conv = fullz[r:r + c0, :]
        fullc[0:p0, :] = new_pool
        fullz[0:c0, :] = new_conv

    pools = [functools.partial(pool, gi, w) for gi, w in enumerate(POOL_WINDOWS)]
    return [load_pool] + pools + [project], [load_conv, conv, shift_history]


def _trunk_kernel(*refs, has_mix, nxt, d_half, nb, prefix_valid, tiles_per_group, n_cast):
    refs = list(refs)
    h_ref = refs.pop(0)
    if has_mix:
        ya_ref, yb_ref, wo_ref, gf_ref, wg_ref, wu_ref, wd_ref = refs[:7]
        del refs[:7]
    gn_ref = refs.pop(0)
    if nxt != "final":
        win_ref = refs.pop(0)
    if nxt == "even":
        gs_ref, sw_ref, sb_ref = refs[:3]
        del refs[:3]
    if nxt == "odd":
        odd_in = refs[:6]
        del refs[:6]
    cast_in = refs[:n_cast]
    del refs[:n_cast]
    outs = refs
    _cast_blocks(cast_in, outs[:n_cast])
    del outs[:n_cast]
    s = pl.program_id(0)
    if nxt == "odd":
        odd_out = outs[-7:]

        @pl.when(s == 0)
        def _():
            for ref in odd_out[-3:]:
                ref[...] = jnp.zeros(ref.shape, ref.dtype)

    h = h_ref[...]
    if has_mix:
        h = h + _dot(ya_ref[...], wo_ref[:d_half, :]) + _dot(yb_ref[...], wo_ref[d_half:, :])
        hn = _rms(h, gf_ref[...]).astype(BF16)
        d_ff = wg_ref.shape[1]
        acc = None
        for c0 in range(0, d_ff, FF_CHUNK):
            c1 = min(c0 + FF_CHUNK, d_ff)
            g = _dot(hn, wg_ref[:, c0:c1])
            u = _dot(hn, wu_ref[:, c0:c1])
            a = (g * jax.nn.sigmoid(g) * u).astype(BF16)
            part = _dot(a, wd_ref[c0:c1, :])
            acc = part if acc is None else acc + part
        h = h + acc
        if nxt != "final":
            outs.pop(0)[...] = h
    hn = _rms(h, gn_ref[...])
    if nxt == "final":
        outs[0][...] = hn
        return
    hb = hn.astype(BF16)
    if nxt == "odd":
        part = {}

        def project(name, block):
            part[name] = _dot(hb, win_ref[:, block * d_half:(block + 1) * d_half])

        project("xc", 0)
        pool_pieces, conv_pieces = _poolconv_pieces(
            part["xc"], lambda: part["cg"] * part["xd"], lambda: part["bg"], s % tiles_per_group, *odd_in, *odd_out,
            nb=nb, prefix_valid=prefix_valid)
        later = [functools.partial(project, name, block) for name, block in (("xd", 1), ("cg", 3), ("bg", 2))]
        for piece in pool_pieces:
            if later:
                later.pop(0)()
            piece()
        for piece in later + conv_pieces:
            piece()
        return
    proj = _dot(hb, win_ref[...])
    ua_ref, yb_ref = outs[:2]
    ua_ref[...] = proj[:, :d_half]
    ub = proj[:, d_half:2 * d_half]
    vn = _rms(proj[:, 2 * d_half:], gs_ref[...])
    if nb == 1:
        _sgu_chunks(ub, vn, sw_ref, sb_ref, yb_ref)
    else:
        _sgu_short(ub, vn, sw_ref, sb_ref, yb_ref, nb=nb)
        outs[2][...] = vn


def _cast_stream(cast, n_steps):
    args, in_specs, out_shapes, out_specs = [], [], [], []
    for w, idx in cast:
        n_rows, n_cols = w.shape[1:]
        rpb = next(k for k in range(BF16_ROWS, n_rows + 1, BF16_ROWS)
                   if n_rows % k == 0 and n_rows // k <= n_steps)
        n_blk = n_rows // rpb
        args.append(w)
        in_specs.append(pl.BlockSpec((None, rpb, n_cols),
                                     lambda s, n_blk=n_blk, idx=idx: (idx, s * n_blk // n_steps, 0)))
        out_shapes.append(jax.ShapeDtypeStruct((n_rows, n_cols), BF16))
        out_specs.append(pl.BlockSpec((rpb, n_cols), lambda s, n_blk=n_blk: (s * n_blk // n_steps, 0)))
    return args, in_specs, out_shapes, out_specs


def _cast_blocks(srcs, dsts):
    for src, dst in zip(srcs, dsts):
        dst[...] = src[...].astype(dst.dtype)


def _trunk_call(h, layer, W, *, mix, wts, nxt, nb, prefix_valid, sgu_prm, cast=(), odd_hist=None):
    n_grp, r, d = h.shape
    d_half = d // 2
    tile = min(ROW_TILE if mix is not None else 2 * ROW_TILE, r)
    i = layer // 2
    tpg = r // tile
    n_tiles = n_grp * tpg

    def rows(c):
        return pl.BlockSpec((None, tile, c), lambda s: (s // tpg, s % tpg, 0))

    def hist_spec(n_rows):
        return pl.BlockSpec((None, n_rows, d_half), lambda s: (s // tpg, 0, 0))

    args, specs = [h], [rows(d)]
    if mix is not None:
        ya, yb = mix
        w_out, gate, up, down = wts[:4]
        args += [ya, yb, w_out, W["norm_ffn"], gate, up, down]
        specs += [rows(d_half), rows(d_half), _const_spec(w_out.shape), _layer_spec(W["norm_ffn"], layer - 1)]
        specs += [_const_spec(w.shape) for w in (gate, up, down)]
    if nxt == "final":
        args.append(W["norm_final"])
        specs.append(_const_spec(W["norm_final"].shape))
    else:
        args += [W["norm_mix"], wts[-1]]
        specs += [_layer_spec(W["norm_mix"], layer), _const_spec(wts[-1].shape)]
    scratch = []
    if nxt == "even":
        sgu = [W["sgu_norm"], *sgu_prm]
        args += sgu
        specs += [_layer_spec(w, i) for w in sgu]
        assert (nb == 1 and tile % SGU_CHUNK == 0) or (n_tiles == 1 and tile // nb < SGU_CHUNK)
        out_shapes = [jax.ShapeDtypeStruct((n_grp, r, d_half), F32), jax.ShapeDtypeStruct((n_grp, r, d_half), BF16)]
        if nb > 1:
            out_shapes.append(jax.ShapeDtypeStruct((n_grp, r, d_half), F32))
        out_specs = [rows(d_half)] * len(out_shapes)
    elif nxt == "odd":
        hp, hc = POOL_BUF * nb, (CONV_W - 1) * nb
        p0, c0 = _round_up(hp, SUBLANES), _round_up(hc, SUBLANES)
        if odd_hist is None:
            pool_buf, conv_buf = jnp.zeros((n_grp, p0, d_half), F32), jnp.zeros((n_grp, c0, d_half), F32)
        else:
            pool_buf, conv_buf = odd_hist
            assert (p0, c0) == (hp, hc)
        odd_w = [W["pool_w"], W["pool_scale"], W["conv_w"], W["conv_b"]]
        args += [pool_buf, conv_buf] + odd_w
        specs += [hist_spec(p0), hist_spec(c0)] + [_layer_spec(w, i) for w in odd_w]
        scratch = [pltpu.VMEM((p0 + tile, d_half), F32), pltpu.VMEM((c0 + tile, d_half), F32),
                   pltpu.VMEM((tile, d_half), F32)]
        out_shapes = [jax.ShapeDtypeStruct((n_grp, r, d_half), BF16)] * 2 + [
            jax.ShapeDtypeStruct((n_grp, hp, d_half), F32), jax.ShapeDtypeStruct((n_grp, hc, d_half), F32)]
        out_specs = [rows(d_half)] * 2 + [hist_spec(hp), hist_spec(hc)]
    else:
        out_shapes, out_specs = [], []
    if mix is not None or nxt == "final":
        out_shapes = [jax.ShapeDtypeStruct((n_grp, r, d), F32)] + out_shapes
        out_specs = [rows(d)] + out_specs
    n_cast = len(cast)
    cast_args, cast_in_specs, cast_shapes, cast_specs = _cast_stream(cast, n_tiles)
    args, specs = args + cast_args, specs + cast_in_specs
    out_shapes, out_specs = cast_shapes + out_shapes, cast_specs + out_specs

    outs = pl.pallas_call(
        functools.partial(_trunk_kernel, has_mix=mix is not None, nxt=nxt, d_half=d_half, nb=nb,
                          prefix_valid=prefix_valid, tiles_per_group=tpg, n_cast=n_cast),
        grid=(n_tiles,),
        in_specs=specs,
        out_specs=out_specs,
        out_shape=out_shapes,
        scratch_shapes=scratch,
        compiler_params=_params("arbitrary"),
        name=f"trunk_{'mix' if mix is not None else 'in'}_{nxt}",
    )(*args)
    return outs[n_cast:], tuple(outs[:n_cast])


def _s5_prep_kernel(lr_ref, li_ref, ldt_ref, bre_ref, bim_ref, are_ref, aim_ref, bbre_ref, bbim_ref):
    lr = lr_ref[...]
    li = li_ref[...]
    dt = jnp.exp(ldt_ref[...])
    mag = jnp.exp(lr * dt)
    ang = li * dt
    ab_re = mag * jnp.cos(ang)
    ab_im = mag * jnp.sin(ang)
    den = lr * lr + li * li
    f_re = ((ab_re - 1.0) * lr + ab_im * li) / den
    f_im = (ab_im * lr - (ab_re - 1.0) * li) / den
    are_ref[...] = ab_re
    aim_ref[...] = ab_im
    for g in range(lr.shape[0]):
        fr = f_re[g:g + 1, :]
        fi = f_im[g:g + 1, :]
        br = bre_ref[g]
        bi = bim_ref[g]
        bbre_ref[g] = fr * br - fi * bi
        bbim_ref[g] = fr * bi + fi * br


def _s5_prep(lam_re, lam_im, log_dt, b_re, b_im):
    l, g, n = lam_re.shape
    p = b_re.shape[-1]
    bt_re = jnp.swapaxes(b_re, 2, 3).reshape(l * g, p, n)
    bt_im = jnp.swapaxes(b_im, 2, 3).reshape(l * g, p, n)
    a_re, a_im, bb_re, bb_im = pl.pallas_call(
        _s5_prep_kernel,
        out_shape=[jax.ShapeDtypeStruct((l * g, n), F32)] * 2 + [jax.ShapeDtypeStruct((l * g, p, n), F32)] * 2,
        name="s5_discretize",
    )(lam_re.reshape(l * g, n), lam_im.reshape(l * g, n), log_dt.reshape(l * g, 1), bt_re, bt_im)
    gs = LANES // p
    eye = jnp.eye(gs, dtype=F32)

    def blockdiag_in(bb):
        return jnp.einsum("lkgpn,gh->lkgphn", bb.reshape(l, g // gs, gs, p, n), eye).reshape(
            l, g // gs, gs * p, gs * n).astype(BF16)

    def sublane_rows(a):
        return jnp.broadcast_to(a.reshape(l, 1, g * n), (l, SUBLANES, g * n))

    return sublane_rows(a_re), sublane_rows(a_im), blockdiag_in(bb_re), blockdiag_in(bb_im)


def _blockdiag_out(c):
    l, g, p, n = c.shape
    gs = LANES // p
    eye = jnp.eye(gs, dtype=F32)
    return jnp.einsum("lkgpn,gh->lkgnhp", c.reshape(l, g // gs, gs, p, n), eye).reshape(
        l, g // gs, gs * n, gs * p).astype(BF16)


def _s5_pieces(u_ref, prm, scratch, store_y, *, n_grp, nb, tb):
    are_ref, aim_ref, bbre_ref, bbim_ref, cre_ref, cim_ref, d_ref, gw_ref, gb_ref = prm
    ut, xre, xim, hre_s, him_s = scratch
    yt = ut
    n_seq = n_grp * nb
    n_slab, slab_in, slab_st = bbre_ref.shape

    def ring(k):
        return slice(k % 2 * slab_st, (k % 2 + 1) * slab_st)

    def load_u():
        for k in range(n_slab):
            lanes = slice(k * slab_in, (k + 1) * slab_in)
            if n_grp == 1:
                ut[k] = u_ref[0, :, lanes]
            else:
                for b in range(n_grp):
                    ut[k, pl.ds(b, tb, stride=n_grp), :] = u_ref[b, :, lanes]

    def project_in(k, x, bb_ref):
        x[:, ring(k)] = _dot(ut[k].astype(BF16), bb_ref[k])

    def read_out(k, x, c_ref, rows):
        return _dot(x[rows, ring(k)].astype(BF16), c_ref[k])

    def scan_pieces(k):
        cols, xcols = slice(k * slab_st, (k + 1) * slab_st), ring(k)
        items = [(r0, t) for r0 in range(0, n_seq, SUBLANES) for t in range(tb)]
        per = len(items) // SCAN_PIECES

        def run(chunk):
            hr = hi = cur = None
            for r0, t in chunk:
                if r0 != cur:
                    if cur is not None:
                        hre_s[cur:cur + SUBLANES, cols] = hr
                        him_s[cur:cur + SUBLANES, cols] = hi
                    hr, hi, cur = hre_s[r0:r0 + SUBLANES, cols], him_s[r0:r0 + SUBLANES, cols], r0
                ar = are_ref[:, cols]
                ai = aim_ref[:, cols]
                r = t * n_seq + r0
                nr = ar * hr - ai * hi + xre[r:r + SUBLANES, xcols]
                ni = ar * hi + ai * hr + xim[r:r + SUBLANES, xcols]
                xre[r:r + SUBLANES, xcols] = nr
                xim[r:r + SUBLANES, xcols] = ni
                hr, hi = nr, ni
            hre_s[cur:cur + SUBLANES, cols] = hr
            him_s[cur:cur + SUBLANES, cols] = hi

        return [functools.partial(run, items[p * per:(p + 1) * per]) for p in range(SCAN_PIECES)]

    rows = slice(0, tb * n_seq)
    re_im = ((xre, bbre_ref, cre_ref), (xim, bbim_ref, cim_ref))
    y_parts, box = {}, {}

    def read_out_part(k, x, c):
        y_parts.setdefault(k, []).append(read_out(k, x, c, rows))

    half = tb * n_seq // 2
    row_halves = (slice(0, half), slice(half, 2 * half))

    def gate():
        outs = []
        for rh in row_halves:
            y = [y_parts[k][0][rh] - y_parts[k][1][rh] + d_ref[:, k * slab_in:(k + 1) * slab_in] * ut[k, rh, :]
                 for k in range(n_slab)]
            z = jax.nn.gelu(jnp.concatenate(y, axis=-1))
            outs.append(z * jax.nn.sigmoid(_dot(z.astype(BF16), gw_ref[...]) + gb_ref[...]))
        box["out"] = outs

    def write(k):
        lanes = slice(k * slab_in, (k + 1) * slab_in)
        if n_grp == 1:
            store_y(0, lanes, jnp.concatenate([o[:, lanes] for o in box["out"]], axis=0).astype(BF16))
        else:
            for rh, o in zip(row_halves, box["out"]):
                yt[k, rh, :] = o[:, lanes]
            for b in range(n_grp):
                store_y(b, lanes, yt[k, pl.ds(b, tb, stride=n_grp), :].astype(BF16))

    pieces = [load_u] + [functools.partial(project_in, 0, x, bb) for x, bb, _ in re_im]
    for k in range(n_slab):
        matmuls = [functools.partial(read_out_part, k - 1, x, c) for x, _, c in re_im] if k > 0 else []
        matmuls += [functools.partial(project_in, k + 1, x, bb) for x, bb, _ in re_im] if k + 1 < n_slab else []
        scans = scan_pieces(k)
        for p, piece in enumerate(scans):
            pieces += matmuls[p * len(matmuls) // len(scans):(p + 1) * len(matmuls) // len(scans)] + [piece]
    pieces += [functools.partial(read_out_part, n_slab - 1, x, c) for x, _, c in re_im] + [gate]
    pieces += [functools.partial(write, k) for k in range(n_slab)]
    return pieces


def _s5_kernel(*refs, n_grp, nb, tb, n_cast):
    n_in = 12
    u_ref, h0re_ref, h0im_ref = refs[:3]
    y_ref, hre_out, him_out = refs[n_in + n_cast:n_in + n_cast + 3]
    scratch = refs[n_in + 2 * n_cast + 3:]
    hre_s, him_s = scratch[-2:]
    _cast_blocks(refs[n_in:n_in + n_cast], refs[n_in + n_cast + 3:n_in + 2 * n_cast + 3])

    @pl.when(pl.program_id(0) == 0)
    def _():
        hre_s[...] = h0re_ref[...]
        him_s[...] = h0im_ref[...]

    n_sub = u_ref.shape[1] // (tb * nb)
    ut_all = scratch[0]
    lists = []
    for sub in range(n_sub):
        rows = slice(sub * tb * nb, (sub + 1) * tb * nb)

        def store_y(b, lanes, y, rows=rows):
            y_ref[b, rows, lanes] = y

        lists.append(_s5_pieces(u_ref.at[:, rows, :], refs[3:n_in], (ut_all.at[sub],) + tuple(scratch[1:]),
                                store_y, n_grp=n_grp, nb=nb, tb=tb))
    n_tail = 7
    order = list(lists[0][:-n_tail])
    for sub in range(1, n_sub):
        tail, nxt = lists[sub - 1][-n_tail:], lists[sub]
        for a, b in zip(tail, nxt[:n_tail]):
            order += [a, b]
        order += nxt[n_tail:-n_tail]
    order += lists[-1][-n_tail:]
    for piece in order:
        piece()
    hre_out[...] = hre_s[...]
    him_out[...] = him_s[...]


def _s5_call(u, h0_re, h0_im, consts, layer, *, nb, cast=()):
    n_grp, r, d_a = u.shape
    assert n_grp == 1 or nb == 1
    t = r // nb
    n_seq = n_grp * nb
    n_state, slab_st = consts[0].shape[2], consts[2].shape[3]
    tb = max(1, min(t, ROW_TILE // n_seq))
    rows = tb * n_seq
    n_sub = S5_TILES_PER_STEP if (t // tb) % S5_TILES_PER_STEP == 0 else 1
    n_steps = t // tb // n_sub
    state_spec = pl.BlockSpec((n_seq, n_state), lambda i: (0, 0))
    blk = pl.BlockSpec((n_grp, n_sub * tb * nb, d_a), lambda i: (0, i, 0))
    cast_args, cast_in_specs, cast_shapes, cast_specs = _cast_stream(cast, n_steps)
    y, hre, him, *rounded = pl.pallas_call(
        functools.partial(_s5_kernel, n_grp=n_grp, nb=nb, tb=tb, n_cast=len(cast)),
        grid=(n_steps,),
        in_specs=[blk, state_spec, state_spec] + [_layer_spec(c, layer) for c in consts] + cast_in_specs,
        out_specs=[blk, state_spec, state_spec] + cast_specs,
        out_shape=[jax.ShapeDtypeStruct(u.shape, BF16),
                   jax.ShapeDtypeStruct((n_seq, n_state), F32),
                   jax.ShapeDtypeStruct((n_seq, n_state), F32)] + cast_shapes,
        scratch_shapes=[pltpu.VMEM((n_sub, d_a // LANES, rows, LANES), F32),
                        pltpu.VMEM((rows, 2 * slab_st), F32), pltpu.VMEM((rows, 2 * slab_st), F32),
                        pltpu.VMEM((n_seq, n_state), F32), pltpu.VMEM((n_seq, n_state), F32)],
        compiler_params=_params("arbitrary"),
        name="s5_mixer",
    )(u, h0_re, h0_im, *consts, *cast_args)
    return y, hre, him, tuple(rounded)


def _sgu_chunks(u, vn, w_ref, bias_ref, y_ref):
    n_heads, cl, _ = w_ref.shape
    hd = vn.shape[1] // n_heads
    row = lax.broadcasted_iota(jnp.int32, (cl, cl), 0)
    col = lax.broadcasted_iota(jnp.int32, (cl, cl), 1)
    w = [jnp.where(col <= row, w_ref[h], 0.0).astype(BF16) for h in range(n_heads)]
    heads_per_slab = LANES // hd
    lane = lax.broadcasted_iota(jnp.int32, (cl, LANES), 1)
    for c in range(vn.shape[0] // cl):
        r = slice(c * cl, (c + 1) * cl)
        slabs = []
        for s in range(n_heads // heads_per_slab):
            v = vn[r, s * LANES:(s + 1) * LANES].astype(BF16)
            mixed = _dot(w[s * heads_per_slab], v)
            for j in range(1, heads_per_slab):
                mixed = jnp.where(lane >= j * hd, _dot(w[s * heads_per_slab + j], v), mixed)
            slabs.append(mixed)
        mixed = jnp.concatenate(slabs, axis=-1) + bias_ref[...]
        y_ref[r, :] = (u[r, :] * mixed).astype(y_ref.dtype)


def _sgu_short(u, vn, wl_ref, bias_ref, y_ref, *, nb):
    for i in range(wl_ref.shape[0]):
        mixed = bias_ref[i:i + 1, :]
        for j in range(i + 1):
            mixed = mixed + wl_ref[i, j:j + 1, :] * vn[j * nb:(j + 1) * nb, :]
        y_ref[i * nb:(i + 1) * nb, :] = (u[i * nb:(i + 1) * nb, :] * mixed).astype(y_ref.dtype)


def _sgu_params(w_s, b_s, d_b, t):
    hd = d_b // w_s.shape[1]
    cl = min(t, SGU_CHUNK)
    bias = jnp.repeat(jnp.swapaxes(b_s[:, :, :cl], 1, 2), hd, axis=2)
    if t % SGU_CHUNK == 0:
        return w_s, bias
    assert t < SGU_CHUNK
    return jnp.repeat(jnp.transpose(w_s[:, :, :cl, :cl], (0, 2, 3, 1)), hd, axis=3), bias


def _interleave(a):
    return jnp.swapaxes(a, 0, 1).reshape(1, a.shape[0] * a.shape[1], a.shape[2])


def _deinterleave(a, nb):
    return jnp.swapaxes(a.reshape(-1, nb, a.shape[2]), 0, 1)


def _run_trunk(x, s5_re, s5_im, hist_bufs, prefix_valid, W, *, interleaved, rounded=None):
    batch, t, d = x.shape
    depth = W["norm_mix"].shape[0]
    nb = batch if interleaved else 1
    pack = _interleave if interleaved else (lambda a: a)
    unpack = (lambda a: _deinterleave(a, nb)) if interleaved else (lambda a: a)
    common = dict(nb=nb, prefix_valid=prefix_valid, sgu_prm=_sgu_params(W["sgu_w"], W["sgu_b"], d // 2, t))

    def kind(layer):
        return "final" if layer == depth else ("even" if layer % 2 == 0 else "odd")

    def hist(layer):
        if kind(layer) != "odd" or hist_bufs is None:
            return None
        return tuple(pack(buf[layer // 2]) for buf in hist_bufs)

    def matrices(call):
        prev, need = call - 1, []
        if call > 0:
            need += [(W["w_out_even" if prev % 2 == 0 else "w_out_odd"], prev // 2),
                     (W["ffn_gate"], prev), (W["ffn_up"], prev), (W["ffn_down"], prev)]
        if call < depth:
            need.append((W["w_in_even" if call % 2 == 0 else "w_in_odd"], call // 2))
        return need

    convert = rounded is None
    if convert:
        rounded = [[w[idx].astype(BF16) for w, idx in matrices(0)]]

    s5_rounds_first = convert and kind(0) == "even"

    def cast(call):
        return matrices(call + 1) if convert and call < depth and not (call == 0 and s5_rounds_first) else ()

    h = pack(x)
    mixer_in, nxt_w = _trunk_call(h, 0, W, mix=None, wts=rounded[0], nxt=kind(0), odd_hist=hist(0), cast=cast(0),
                                  **common)
    rounded += [nxt_w] if nxt_w else []
    new_re, new_im, new_v, new_pool, new_conv = [], [], [], [], []
    for layer in range(depth):
        i = layer // 2
        if layer % 2 == 0:
            u_a, y_b = mixer_in[:2]
            y_a, hre, him, nxt_w = _s5_call(u_a, s5_re[i].reshape(batch, -1), s5_im[i].reshape(batch, -1), W["s5"],
                                            i, nb=nb, cast=matrices(1) if layer == 0 and s5_rounds_first else ())
            rounded += [nxt_w] if nxt_w else []
            new_re.append(hre.reshape(s5_re[i].shape))
            new_im.append(him.reshape(s5_im[i].shape))
            if interleaved:
                new_v.append(unpack(mixer_in[2]))
            mix = (y_a, y_b)
        else:
            y_c, y_d, pout, cout = mixer_in
            new_pool.append(unpack(pout))
            new_conv.append(unpack(cout))
            mix = (y_c, y_d)
        outs, nxt_w = _trunk_call(h, layer + 1, W, mix=mix, wts=rounded[layer + 1], nxt=kind(layer + 1),
                                  odd_hist=hist(layer + 1), cast=cast(layer + 1), **common)
        rounded += [nxt_w] if nxt_w else []
        h, mixer_in = outs[0], outs[1:]
    return (unpack(h), jnp.stack(new_re), jnp.stack(new_im), jnp.stack(new_v) if new_v else None,
            jnp.stack(new_pool), jnp.stack(new_conv), rounded)


def kernel(x_prompt, x_sample, state_s5_re, state_s5_im, state_pool, state_conv, norm_mix, norm_ffn, norm_final, w_in_even, w_out_even, s5_lambda_re, s5_lambda_im, s5_log_dt, s5_b_re, s5_b_im, s5_c_re, s5_c_im, s5_d, s5_glu_w, s5_glu_b, sgu_norm, sgu_w, sgu_b, w_in_odd, w_out_odd, pool_w, pool_scale, conv_w, conv_b, ffn_w_gate, ffn_w_up, ffn_w_down):
    n_even, n_odd = w_in_even.shape[0], w_in_odd.shape[0]
    depth = norm_mix.shape[0]
    s5 = [*_s5_prep(s5_lambda_re, s5_lambda_im, s5_log_dt, s5_b_re, s5_b_im),
          _blockdiag_out(s5_c_re), _blockdiag_out(s5_c_im), s5_d.reshape(n_even, 1, -1),
          s5_glu_w.astype(BF16), s5_glu_b.reshape(n_even, 1, -1)]
    d = norm_mix.shape[1]
    W = dict(
        norm_mix=norm_mix.reshape(depth, 1, d), norm_ffn=norm_ffn.reshape(depth, 1, d),
        norm_final=norm_final.reshape(1, d),
        w_in_even=w_in_even, w_out_even=w_out_even, w_in_odd=w_in_odd, w_out_odd=w_out_odd,
        sgu_norm=sgu_norm.reshape(n_even, 1, -1), sgu_w=sgu_w, sgu_b=sgu_b, s5=s5,
        pool_w=pool_w.astype(BF16), pool_scale=pool_scale.reshape(n_odd, 1, -1), conv_w=conv_w,
        conv_b=conv_b.reshape(n_odd, 1, -1),
        ffn_gate=ffn_w_gate, ffn_up=ffn_w_up, ffn_down=ffn_w_down,
    )
    bp = x_prompt.shape[0]
    z_s5 = jnp.zeros((n_even, bp) + state_s5_re.shape[2:], state_s5_re.dtype)
    y_p, p_re, p_im, _, p_pool, p_conv, rounded = _run_trunk(x_prompt, z_s5, z_s5, None, False, W,
                                                             interleaved=False)
    y_s, s_re, s_im, s_v, s_pool, s_conv, _ = _run_trunk(
        x_sample, state_s5_re, state_s5_im, (state_pool, state_conv), True, W, interleaved=True, rounded=rounded)
    return (y_p, y_s, p_re, p_im, p_pool, p_conv, s_re, s_im, s_v, s_pool, s_conv)
```

```python
import functools

import jax
import jax.numpy as jnp
from jax import lax
from jax.experimental import pallas as pl
from jax.experimental.pallas import tpu as pltpu

F32 = jnp.float32
BF16 = jnp.bfloat16

EPS = 1e-6
SGU_CHUNK = 128
POOL_WINDOWS = (2, 4, 8, 16)
POOL_BUF = max(POOL_WINDOWS) - 1
CONV_W = 3

LANES = 128
SUBLANES = 8
BF16_ROWS = 2 * SUBLANES
V7X_VMEM_BYTES = 64 * 1024 * 1024
ROW_TILE = 512
FF_CHUNK = 256
SCAN_PIECES = 4
S5_TILES_PER_STEP = 2
VMEM_LIMIT = V7X_VMEM_BYTES * 7 // 8


def _dot(a, b):
    return jnp.dot(a, b, preferred_element_type=F32)


def _rms(x, g):
    return x * lax.rsqrt(jnp.mean(x * x, axis=-1, keepdims=True) + EPS) * g


def _round_up(n, m):
    return -(-n // m) * m


def _const_spec(shape):
    zeros = (0,) * len(shape)
    return pl.BlockSpec(shape, lambda *_: zeros, pipeline_mode=pl.Buffered(1))


def _layer_spec(stack, layer):
    zeros = (0,) * (stack.ndim - 1)
    return pl.BlockSpec((None,) + stack.shape[1:], lambda *_: (layer,) + zeros, pipeline_mode=pl.Buffered(1))


def _params(*semantics):
    return pltpu.CompilerParams(dimension_semantics=semantics, vmem_limit_bytes=VMEM_LIMIT)


def _poolconv_pieces(xc, z, bg, i, pbuf_ref, cbuf_ref, pw_ref, ps_ref, cw_ref, cb_ref,
                     yc_ref, yd_ref, pout_ref, cout_ref, fullc, fullz, diff, *, nb, prefix_valid):
    hp, hc = POOL_BUF * nb, (CONV_W - 1) * nb
    p0, c0 = pbuf_ref.shape[0], cbuf_ref.shape[0]
    r = xc.shape[0]
    cg = xc.shape[1] // len(POOL_WINDOWS)
    first = i == 0

    def load_pool():
        fullc[0:p0, :] = jnp.where(first, pbuf_ref[...], fullc[0:p0, :])
        fullc[p0:p0 + r, :] = xc

    def load_conv():
        fullz[0:c0, :] = jnp.where(first, cbuf_ref[...], fullz[0:c0, :])
        fullz[c0:c0 + r, :] = z()

    def conv():
        acc = cb_ref[...]
        for k in range(CONV_W):
            back = (CONV_W - 1 - k) * nb
            acc = acc + fullz[c0 - back:c0 - back + r, :] * cw_ref[k:k + 1, :]
        yd_ref[...] = (bg() * acc).astype(yd_ref.dtype)

    def window_sum(rows, lanes, w):
        s = fullc[p0:p0 + rows, lanes]
        for k in range(1, w):
            s = s + fullc[p0 - k * nb:p0 - k * nb + rows, lanes]
        return s

    def pool(gi, w):
        lanes = slice(gi * cg, (gi + 1) * cg)
        diff[:, lanes] = window_sum(r, lanes, w) * (1.0 / w) - fullc[p0:p0 + r, lanes]
        if not prefix_valid:
            head = min(p0, r)
            t = lax.broadcasted_iota(jnp.int32, (head, 1), 0) // nb
            n = jnp.minimum(t + 1, w).astype(F32)
            short = window_sum(head, lanes, w) / n - fullc[p0:p0 + head, lanes]
            diff[0:head, lanes] = jnp.where(first, short, diff[0:head, lanes])

    def project():
        for gi in range(len(POOL_WINDOWS)):
            lanes = slice(gi * cg, (gi + 1) * cg)
            yc = _dot(diff[:, lanes].astype(BF16), pw_ref[gi]) * ps_ref[:, lanes]
            yc_ref[:, lanes] = yc.astype(yc_ref.dtype)

    def shift_history():
        pout_ref[...] = fullc[p0 + r - hp:p0 + r, :]
        cout_ref[...] = fullz[c0 + r - hc:c0 + r, :]
        new_pool = fullc[r:r + p0, :]
        new_conv = fullz[r:r + c0, :]
        fullc[0:p0, :] = new_pool
        fullz[0:c0, :] = new_conv

    pools = [functools.partial(pool, gi, w) for gi, w in enumerate(POOL_WINDOWS)]
    return [load_pool] + pools + [project], [load_conv, conv, shift_history]


def _trunk_kernel(*refs, has_mix, nxt, d_half, nb, prefix_valid, tiles_per_group, n_cast):
    refs = list(refs)
    h_ref = refs.pop(0)
    if has_mix:
        ya_ref, yb_ref, wo_ref, gf_ref, wg_ref, wu_ref, wd_ref = refs[:7]
        del refs[:7]
    gn_ref = refs.pop(0)
    if nxt != "final":
        win_ref = refs.pop(0)
    if nxt == "even":
        gs_ref, sw_ref, sb_ref = refs[:3]
        del refs[:3]
    if nxt == "odd":
        odd_in = refs[:6]
        del refs[:6]
    cast_in = refs[:n_cast]
    del refs[:n_cast]
    outs = refs
    _cast_blocks(cast_in, outs[:n_cast])
    del outs[:n_cast]
    s = pl.program_id(0)
    if nxt == "odd":
        odd_out = outs[-7:]

        @pl.when(s == 0)
        def _():
            for ref in odd_out[-3:]:
                ref[...] = jnp.zeros(ref.shape, ref.dtype)

    h = h_ref[...]
    if has_mix:
        h = h + _dot(ya_ref[...], wo_ref[:d_half, :]) + _dot(yb_ref[...], wo_ref[d_half:, :])
        hn = _rms(h, gf_ref[...]).astype(BF16)
        d_ff = wg_ref.shape[1]
        acc = None
        for c0 in range(0, d_ff, FF_CHUNK):
            c1 = min(c0 + FF_CHUNK, d_ff)
            g = _dot(hn, wg_ref[:, c0:c1])
            u = _dot(hn, wu_ref[:, c0:c1])
            a = (g * jax.nn.sigmoid(g) * u).astype(BF16)
            part = _dot(a, wd_ref[c0:c1, :])
            acc = part if acc is None else acc + part
        h = h + acc
        if nxt != "final":
            outs.pop(0)[...] = h
    hn = _rms(h, gn_ref[...])
    if nxt == "final":
        outs[0][...] = hn
        return
    hb = hn.astype(BF16)
    if nxt == "odd":
        part = {}

        def project(name, block):
            part[name] = _dot(hb, win_ref[:, block * d_half:(block + 1) * d_half])

        project("xc", 0)
        pool_pieces, conv_pieces = _poolconv_pieces(
            part["xc"], lambda: part["cg"] * part["xd"], lambda: part["bg"], s % tiles_per_group, *odd_in, *odd_out,
            nb=nb, prefix_valid=prefix_valid)
        later = [functools.partial(project, name, block) for name, block in (("xd", 1), ("cg", 3), ("bg", 2))]
        for piece in pool_pieces:
            if later:
                later.pop(0)()
            piece()
        for piece in later + conv_pieces:
            piece()
        return
    ua_ref, yb_ref = outs[:2]
    vn = _rms(_dot(hb, win_ref[:, 2 * d_half:]), gs_ref[...])
    ub = _dot(hb, win_ref[:, d_half:2 * d_half])
    if nb == 1:
        _sgu_chunks(ub, vn, sw_ref, sb_ref, yb_ref)
    else:
        _sgu_short(ub, vn, sw_ref, sb_ref, yb_ref, nb=nb)
        outs[2][...] = vn
    ua_ref[...] = _dot(hb, win_ref[:, :d_half])


def _cast_stream(cast, n_steps):
    args, in_specs, out_shapes, out_specs = [], [], [], []
    for w, idx in cast:
        n_rows, n_cols = w.shape[1:]
        rpb = next(k for k in range(BF16_ROWS, n_rows + 1, BF16_ROWS)
                   if n_rows % k == 0 and n_rows // k <= n_steps)
        n_blk = n_rows // rpb
        args.append(w)
        in_specs.append(pl.BlockSpec((None, rpb, n_cols),
                                     lambda s, n_blk=n_blk, idx=idx: (idx, s * n_blk // n_steps, 0)))
        out_shapes.append(jax.ShapeDtypeStruct((n_rows, n_cols), BF16))
        out_specs.append(pl.BlockSpec((rpb, n_cols), lambda s, n_blk=n_blk: (s * n_blk // n_steps, 0)))
    return args, in_specs, out_shapes, out_specs


def _cast_blocks(srcs, dsts):
    for src, dst in zip(srcs, dsts):
        dst[...] = src[...].astype(dst.dtype)


def _trunk_call(h, layer, W, *, mix, wts, nxt, nb, prefix_valid, sgu_prm, cast=(), odd_hist=None):
    n_grp, r, d = h.shape
    d_half = d // 2
    tile = min(ROW_TILE if mix is not None else 2 * ROW_TILE, r)
    i = layer // 2
    tpg = r // tile
    n_tiles = n_grp * tpg

    def rows(c):
        return pl.BlockSpec((None, tile, c), lambda s: (s // tpg, s % tpg, 0))

    def hist_spec(n_rows):
        return pl.BlockSpec((None, n_rows, d_half), lambda s: (s // tpg, 0, 0))

    args, specs = [h], [rows(d)]
    if mix is not None:
        ya, yb = mix
        w_out, gate, up, down = wts[:4]
        args += [ya, yb, w_out, W["norm_ffn"], gate, up, down]
        specs += [rows(d_half), rows(d_half), _const_spec(w_out.shape), _layer_spec(W["norm_ffn"], layer - 1)]
        specs += [_const_spec(w.shape) for w in (gate, up, down)]
    if nxt == "final":
        args.append(W["norm_final"])
        specs.append(_const_spec(W["norm_final"].shape))
    else:
        args += [W["norm_mix"], wts[-1]]
        specs += [_layer_spec(W["norm_mix"], layer), _const_spec(wts[-1].shape)]
    scratch = []
    if nxt == "even":
        sgu = [W["sgu_norm"], *sgu_prm]
        args += sgu
        specs += [_layer_spec(w, i) for w in sgu]
        assert (nb == 1 and tile % SGU_CHUNK == 0) or (n_tiles == 1 and tile // nb < SGU_CHUNK)
        out_shapes = [jax.ShapeDtypeStruct((n_grp, r, d_half), F32), jax.ShapeDtypeStruct((n_grp, r, d_half), BF16)]
        if nb > 1:
            out_shapes.append(jax.ShapeDtypeStruct((n_grp, r, d_half), F32))
        out_specs = [rows(d_half)] * len(out_shapes)
    elif nxt == "odd":
        hp, hc = POOL_BUF * nb, (CONV_W - 1) * nb
        p0, c0 = _round_up(hp, SUBLANES), _round_up(hc, SUBLANES)
        if odd_hist is None:
            pool_buf, conv_buf = jnp.zeros((n_grp, p0, d_half), F32), jnp.zeros((n_grp, c0, d_half), F32)
        else:
            pool_buf, conv_buf = odd_hist
            assert (p0, c0) == (hp, hc)
        odd_w = [W["pool_w"], W["pool_scale"], W["conv_w"], W["conv_b"]]
        args += [pool_buf, conv_buf] + odd_w
        specs += [hist_spec(p0), hist_spec(c0)] + [_layer_spec(w, i) for w in odd_w]
        scratch = [pltpu.VMEM((p0 + tile, d_half), F32), pltpu.VMEM((c0 + tile, d_half), F32),
                   pltpu.VMEM((tile, d_half), F32)]
        out_shapes = [jax.ShapeDtypeStruct((n_grp, r, d_half), BF16)] * 2 + [
            jax.ShapeDtypeStruct((n_grp, hp, d_half), F32), jax.ShapeDtypeStruct((n_grp, hc, d_half), F32)]
        out_specs = [rows(d_half)] * 2 + [hist_spec(hp), hist_spec(hc)]
    else:
        out_shapes, out_specs = [], []
    if mix is not None or nxt == "final":
        out_shapes = [jax.ShapeDtypeStruct((n_grp, r, d), F32)] + out_shapes
        out_specs = [rows(d)] + out_specs
    n_cast = len(cast)
    cast_args, cast_in_specs, cast_shapes, cast_specs = _cast_stream(cast, n_tiles)
    args, specs = args + cast_args, specs + cast_in_specs
    out_shapes, out_specs = cast_shapes + out_shapes, cast_specs + out_specs

    outs = pl.pallas_call(
        functools.partial(_trunk_kernel, has_mix=mix is not None, nxt=nxt, d_half=d_half, nb=nb,
                          prefix_valid=prefix_valid, tiles_per_group=tpg, n_cast=n_cast),
        grid=(n_tiles,),
        in_specs=specs,
        out_specs=out_specs,
        out_shape=out_shapes,
        scratch_shapes=scratch,
        compiler_params=_params("arbitrary"),
        name=f"trunk_{'mix' if mix is not None else 'in'}_{nxt}",
    )(*args)
    return outs[n_cast:], tuple(outs[:n_cast])


def _s5_prep_kernel(lr_ref, li_ref, ldt_ref, bre_ref, bim_ref, are_ref, aim_ref, bbre_ref, bbim_ref):
    lr = lr_ref[...]
    li = li_ref[...]
    dt = jnp.exp(ldt_ref[...])
    mag = jnp.exp(lr * dt)
    ang = li * dt
    ab_re = mag * jnp.cos(ang)
    ab_im = mag * jnp.sin(ang)
    den = lr * lr + li * li
    f_re = ((ab_re - 1.0) * lr + ab_im * li) / den
    f_im = (ab_im * lr - (ab_re - 1.0) * li) / den
    are_ref[...] = ab_re
    aim_ref[...] = ab_im
    for g in range(lr.shape[0]):
        fr = f_re[g:g + 1, :]
        fi = f_im[g:g + 1, :]
        br = bre_ref[g]
        bi = bim_ref[g]
        bbre_ref[g] = fr * br - fi * bi
        bbim_ref[g] = fr * bi + fi * br


def _s5_prep(lam_re, lam_im, log_dt, b_re, b_im):
    l, g, n = lam_re.shape
    p = b_re.shape[-1]
    bt_re = jnp.swapaxes(b_re, 2, 3).reshape(l * g, p, n)
    bt_im = jnp.swapaxes(b_im, 2, 3).reshape(l * g, p, n)
    a_re, a_im, bb_re, bb_im = pl.pallas_call(
        _s5_prep_kernel,
        out_shape=[jax.ShapeDtypeStruct((l * g, n), F32)] * 2 + [jax.ShapeDtypeStruct((l * g, p, n), F32)] * 2,
        name="s5_discretize",
    )(lam_re.reshape(l * g, n), lam_im.reshape(l * g, n), log_dt.reshape(l * g, 1), bt_re, bt_im)
    gs = LANES // p
    eye = jnp.eye(gs, dtype=F32)

    def blockdiag_in(bb):
        return jnp.einsum("lkgpn,gh->lkgphn", bb.reshape(l, g // gs, gs, p, n), eye).reshape(
            l, g // gs, gs * p, gs * n).astype(BF16)

    def sublane_rows(a):
        return jnp.broadcast_to(a.reshape(l, 1, g * n), (l, SUBLANES, g * n))

    return sublane_rows(a_re), sublane_rows(a_im), blockdiag_in(bb_re), blockdiag_in(bb_im)


def _blockdiag_out(c):
    l, g, p, n = c.shape
    gs = LANES // p
    eye = jnp.eye(gs, dtype=F32)
    return jnp.einsum("lkgpn,gh->lkgnhp", c.reshape(l, g // gs, gs, p, n), eye).reshape(
        l, g // gs, gs * n, gs * p).astype(BF16)


def _s5_pieces(u_ref, prm, scratch, store_y, *, n_grp, nb, tb):
    are_ref, aim_ref, bbre_ref, bbim_ref, cre_ref, cim_ref, d_ref, gw_ref, gb_ref = prm
    ut, xre, xim, hre_s, him_s = scratch
    yt = ut
    n_seq = n_grp * nb
    n_slab, slab_in, slab_st = bbre_ref.shape

    def ring(k):
        return slice(k % 2 * slab_st, (k % 2 + 1) * slab_st)

    def load_u():
        for k in range(n_slab):
            lanes = slice(k * slab_in, (k + 1) * slab_in)
            if n_grp == 1:
                ut[k] = u_ref[0, :, lanes]
            else:
                for b in range(n_grp):
                    ut[k, pl.ds(b, tb, stride=n_grp), :] = u_ref[b, :, lanes]

    def project_in(k, x, bb_ref):
        x[:, ring(k)] = _dot(ut[k].astype(BF16), bb_ref[k])

    def read_out(k, x, c_ref, rows):
        return _dot(x[rows, ring(k)].astype(BF16), c_ref[k])

    def scan_pieces(k):
        cols, xcols = slice(k * slab_st, (k + 1) * slab_st), ring(k)
        items = [(r0, t) for r0 in range(0, n_seq, SUBLANES) for t in range(tb)]
        per = len(items) // SCAN_PIECES

        def run(chunk):
            hr = hi = cur = None
            for r0, t in chunk:
                if r0 != cur:
                    if cur is not None:
                        hre_s[cur:cur + SUBLANES, cols] = hr
                        him_s[cur:cur + SUBLANES, cols] = hi
                    hr, hi, cur = hre_s[r0:r0 + SUBLANES, cols], him_s[r0:r0 + SUBLANES, cols], r0
                ar = are_ref[:, cols]
                ai = aim_ref[:, cols]
                r = t * n_seq + r0
                nr = ar * hr - ai * hi + xre[r:r + SUBLANES, xcols]
                ni = ar * hi + ai * hr + xim[r:r + SUBLANES, xcols]
                xre[r:r + SUBLANES, xcols] = nr
                xim[r:r + SUBLANES, xcols] = ni
                hr, hi = nr, ni
            hre_s[cur:cur + SUBLANES, cols] = hr
            him_s[cur:cur + SUBLANES, cols] = hi

        return [functools.partial(run, items[p * per:(p + 1) * per]) for p in range(SCAN_PIECES)]

    rows = slice(0, tb * n_seq)
    re_im = ((xre, bbre_ref, cre_ref), (xim, bbim_ref, cim_ref))
    y_parts, box = {}, {}

    def read_out_part(k, x, c):
        y_parts.setdefault(k, []).append(read_out(k, x, c, rows))

    half = tb * n_seq // 2
    row_halves = (slice(0, half), slice(half, 2 * half))

    def gate():
        outs = []
        for rh in row_halves:
            y = [y_parts[k][0][rh] - y_parts[k][1][rh] + d_ref[:, k * slab_in:(k + 1) * slab_in] * ut[k, rh, :]
                 for k in range(n_slab)]
            z = jax.nn.gelu(jnp.concatenate(y, axis=-1))
            outs.append(z * jax.nn.sigmoid(_dot(z.astype(BF16), gw_ref[...]) + gb_ref[...]))
        box["out"] = outs

    def write(k):
        lanes = slice(k * slab_in, (k + 1) * slab_in)
        if n_grp == 1:
            store_y(0, lanes, jnp.concatenate([o[:, lanes] for o in box["out"]], axis=0).astype(BF16))
        else:
            for rh, o in zip(row_halves, box["out"]):
                yt[k, rh, :] = o[:, lanes]
            for b in range(n_grp):
                store_y(b, lanes, yt[k, pl.ds(b, tb, stride=n_grp), :].astype(BF16))

    pieces = [load_u] + [functools.partial(project_in, 0, x, bb) for x, bb, _ in re_im]
    for k in range(n_slab):
        matmuls = [functools.partial(read_out_part, k - 1, x, c) for x, _, c in re_im] if k > 0 else []
        matmuls += [functools.partial(project_in, k + 1, x, bb) for x, bb, _ in re_im] if k + 1 < n_slab else []
        scans = scan_pieces(k)
        for p, piece in enumerate(scans):
            pieces += matmuls[p * len(matmuls) // len(scans):(p + 1) * len(matmuls) // len(scans)] + [piece]
    pieces += [functools.partial(read_out_part, n_slab - 1, x, c) for x, _, c in re_im] + [gate]
    pieces += [functools.partial(write, k) for k in range(n_slab)]
    return pieces


def _s5_kernel(*refs, n_grp, nb, tb, n_cast):
    n_in = 12
    u_ref, h0re_ref, h0im_ref = refs[:3]
    y_ref, hre_out, him_out = refs[n_in + n_cast:n_in + n_cast + 3]
    scratch = refs[n_in + 2 * n_cast + 3:]
    hre_s, him_s = scratch[-2:]
    _cast_blocks(refs[n_in:n_in + n_cast], refs[n_in + n_cast + 3:n_in + 2 * n_cast + 3])

    @pl.when(pl.program_id(0) == 0)
    def _():
        hre_s[...] = h0re_ref[...]
        him_s[...] = h0im_ref[...]

    n_sub = u_ref.shape[1] // (tb * nb)
    ut_all = scratch[0]
    lists = []
    for sub in range(n_sub):
        rows = slice(sub * tb * nb, (sub + 1) * tb * nb)

        def store_y(b, lanes, y, rows=rows):
            y_ref[b, rows, lanes] = y

        lists.append(_s5_pieces(u_ref.at[:, rows, :], refs[3:n_in], (ut_all.at[sub],) + tuple(scratch[1:]),
                                store_y, n_grp=n_grp, nb=nb, tb=tb))
    n_tail = 7
    order = list(lists[0][:-n_tail])
    for sub in range(1, n_sub):
        tail, nxt = lists[sub - 1][-n_tail:], lists[sub]
        for a, b in zip(tail, nxt[:n_tail]):
            order += [a, b]
        order += nxt[n_tail:-n_tail]
    order += lists[-1][-n_tail:]
    for piece in order:
        piece()
    hre_out[...] = hre_s[...]
    him_out[...] = him_s[...]


def _s5_call(u, h0_re, h0_im, consts, layer, *, nb, cast=()):
    n_grp, r, d_a = u.shape
    assert n_grp == 1 or nb == 1
    t = r // nb
    n_seq = n_grp * nb
    n_state, slab_st = consts[0].shape[2], consts[2].shape[3]
    tb = max(1, min(t, ROW_TILE // n_seq))
    rows = tb * n_seq
    n_sub = S5_TILES_PER_STEP if (t // tb) % S5_TILES_PER_STEP == 0 else 1
    n_steps = t // tb // n_sub
    state_spec = pl.BlockSpec((n_seq, n_state), lambda i: (0, 0))
    blk = pl.BlockSpec((n_grp, n_sub * tb * nb, d_a), lambda i: (0, i, 0))
    cast_args, cast_in_specs, cast_shapes, cast_specs = _cast_stream(cast, n_steps)
    y, hre, him, *rounded = pl.pallas_call(
        functools.partial(_s5_kernel, n_grp=n_grp, nb=nb, tb=tb, n_cast=len(cast)),
        grid=(n_steps,),
        in_specs=[blk, state_spec, state_spec] + [_layer_spec(c, layer) for c in consts] + cast_in_specs,
        out_specs=[blk, state_spec, state_spec] + cast_specs,
        out_shape=[jax.ShapeDtypeStruct(u.shape, BF16),
                   jax.ShapeDtypeStruct((n_seq, n_state), F32),
                   jax.ShapeDtypeStruct((n_seq, n_state), F32)] + cast_shapes,
        scratch_shapes=[pltpu.VMEM((n_sub, d_a // LANES, rows, LANES), F32),
                        pltpu.VMEM((rows, 2 * slab_st), F32), pltpu.VMEM((rows, 2 * slab_st), F32),
                        pltpu.VMEM((n_seq, n_state), F32), pltpu.VMEM((n_seq, n_state), F32)],
        compiler_params=_params("arbitrary"),
        name="s5_mixer",
    )(u, h0_re, h0_im, *consts, *cast_args)
    return y, hre, him, tuple(rounded)


def _sgu_chunks(u, vn, w_ref, bias_ref, y_ref):
    n_heads, cl, _ = w_ref.shape
    hd = vn.shape[1] // n_heads
    row = lax.broadcasted_iota(jnp.int32, (cl, cl), 0)
    col = lax.broadcasted_iota(jnp.int32, (cl, cl), 1)
    w = [jnp.where(col <= row, w_ref[h], 0.0).astype(BF16) for h in range(n_heads)]
    heads_per_slab = LANES // hd
    lane = lax.broadcasted_iota(jnp.int32, (cl, LANES), 1)
    for c in range(vn.shape[0] // cl):
        r = slice(c * cl, (c + 1) * cl)
        slabs = []
        for s in range(n_heads // heads_per_slab):
            v = vn[r, s * LANES:(s + 1) * LANES].astype(BF16)
            mixed = _dot(w[s * heads_per_slab], v)
            for j in range(1, heads_per_slab):
                mixed = jnp.where(lane >= j * hd, _dot(w[s * heads_per_slab + j], v), mixed)
            slabs.append(mixed)
        mixed = jnp.concatenate(slabs, axis=-1) + bias_ref[...]
        y_ref[r, :] = (u[r, :] * mixed).astype(y_ref.dtype)


def _sgu_short(u, vn, wl_ref, bias_ref, y_ref, *, nb):
    for i in range(wl_ref.shape[0]):
        mixed = bias_ref[i:i + 1, :]
        for j in range(i + 1):
            mixed = mixed + wl_ref[i, j:j + 1, :] * vn[j * nb:(j + 1) * nb, :]
        y_ref[i * nb:(i + 1) * nb, :] = (u[i * nb:(i + 1) * nb, :] * mixed).astype(y_ref.dtype)


def _sgu_params(w_s, b_s, d_b, t):
    hd = d_b // w_s.shape[1]
    cl = min(t, SGU_CHUNK)
    bias = jnp.repeat(jnp.swapaxes(b_s[:, :, :cl], 1, 2), hd, axis=2)
    if t % SGU_CHUNK == 0:
        return w_s, bias
    assert t < SGU_CHUNK
    return jnp.repeat(jnp.transpose(w_s[:, :, :cl, :cl], (0, 2, 3, 1)), hd, axis=3), bias


def _interleave(a):
    return jnp.swapaxes(a, 0, 1).reshape(1, a.shape[0] * a.shape[1], a.shape[2])


def _deinterleave(a, nb):
    return jnp.swapaxes(a.reshape(-1, nb, a.shape[2]), 0, 1)


def _run_trunk(x, s5_re, s5_im, hist_bufs, prefix_valid, W, *, interleaved, rounded=None):
    batch, t, d = x.shape
    depth = W["norm_mix"].shape[0]
    nb = batch if interleaved else 1
    pack = _interleave if interleaved else (lambda a: a)
    unpack = (lambda a: _deinterleave(a, nb)) if interleaved else (lambda a: a)
    common = dict(nb=nb, prefix_valid=prefix_valid, sgu_prm=_sgu_params(W["sgu_w"], W["sgu_b"], d // 2, t))

    def kind(layer):
        return "final" if layer == depth else ("even" if layer % 2 == 0 else "odd")

    def hist(layer):
        if kind(layer) != "odd" or hist_bufs is None:
            return None
        return tuple(pack(buf[layer // 2]) for buf in hist_bufs)

    def matrices(call):
        prev, need = call - 1, []
        if call > 0:
            need += [(W["w_out_even" if prev % 2 == 0 else "w_out_odd"], prev // 2),
                     (W["ffn_gate"], prev), (W["ffn_up"], prev), (W["ffn_down"], prev)]
        if call < depth:
            need.append((W["w_in_even" if call % 2 == 0 else "w_in_odd"], call // 2))
        return need

    convert = rounded is None
    if convert:
        rounded = [[w[idx].astype(BF16) for w, idx in matrices(0)]]

    s5_rounds_first = convert and kind(0) == "even"

    def cast(call):
        return matrices(call + 1) if convert and call < depth and not (call == 0 and s5_rounds_first) else ()

    h = pack(x)
    mixer_in, nxt_w = _trunk_call(h, 0, W, mix=None, wts=rounded[0], nxt=kind(0), odd_hist=hist(0), cast=cast(0),
                                  **common)
    rounded += [nxt_w] if nxt_w else []
    new_re, new_im, new_v, new_pool, new_conv = [], [], [], [], []
    for layer in range(depth):
        i = layer // 2
        if layer % 2 == 0:
            u_a, y_b = mixer_in[:2]
            y_a, hre, him, nxt_w = _s5_call(u_a, s5_re[i].reshape(batch, -1), s5_im[i].reshape(batch, -1), W["s5"],
                                            i, nb=nb, cast=matrices(1) if layer == 0 and s5_rounds_first else ())
            rounded += [nxt_w] if nxt_w else []
            new_re.append(hre.reshape(s5_re[i].shape))
            new_im.append(him.reshape(s5_im[i].shape))
            if interleaved:
                new_v.append(unpack(mixer_in[2]))
            mix = (y_a, y_b)
        else:
            y_c, y_d, pout, cout = mixer_in
            new_pool.append(unpack(pout))
            new_conv.append(unpack(cout))
            mix = (y_c, y_d)
        outs, nxt_w = _trunk_call(h, layer + 1, W, mix=mix, wts=rounded[layer + 1], nxt=kind(layer + 1),
                                  odd_hist=hist(layer + 1), cast=cast(layer + 1), **common)
        rounded += [nxt_w] if nxt_w else []
        h, mixer_in = outs[0], outs[1:]
    return (unpack(h), jnp.stack(new_re), jnp.stack(new_im), jnp.stack(new_v) if new_v else None,
            jnp.stack(new_pool), jnp.stack(new_conv), rounded)


def kernel(x_prompt, x_sample, state_s5_re, state_s5_im, state_pool, state_conv, norm_mix, norm_ffn, norm_final, w_in_even, w_out_even, s5_lambda_re, s5_lambda_im, s5_log_dt, s5_b_re, s5_b_im, s5_c_re, s5_c_im, s5_d, s5_glu_w, s5_glu_b, sgu_norm, sgu_w, sgu_b, w_in_odd, w_out_odd, pool_w, pool_scale, conv_w, conv_b, ffn_w_gate, ffn_w_up, ffn_w_down):
    n_even, n_odd = w_in_even.shape[0], w_in_odd.shape[0]
    depth = norm_mix.shape[0]
    s5 = [*_s5_prep(s5_lambda_re, s5_lambda_im, s5_log_dt, s5_b_re, s5_b_im),
          _blockdiag_out(s5_c_re), _blockdiag_out(s5_c_im), s5_d.reshape(n_even, 1, -1),
          s5_glu_w.astype(BF16), s5_glu_b.reshape(n_even, 1, -1)]
    d = norm_mix.shape[1]
    W = dict(
        norm_mix=norm_mix.reshape(depth, 1, d), norm_ffn=norm_ffn.reshape(depth, 1, d),
        norm_final=norm_final.reshape(1, d),
        w_in_even=w_in_even, w_out_even=w_out_even, w_in_odd=w_in_odd, w_out_odd=w_out_odd,
        sgu_norm=sgu_norm.reshape(n_even, 1, -1), sgu_w=sgu_w, sgu_b=sgu_b, s5=s5,
        pool_w=pool_w.astype(BF16), pool_scale=pool_scale.reshape(n_odd, 1, -1), conv_w=conv_w,
        conv_b=conv_b.reshape(n_odd, 1, -1),
        ffn_gate=ffn_w_gate, ffn_up=ffn_w_up, ffn_down=ffn_w_down,
    )
    bp = x_prompt.shape[0]
    z_s5 = jnp.zeros((n_even, bp) + state_s5_re.shape[2:], state_s5_re.dtype)
    y_p, p_re, p_im, _, p_pool, p_conv, rounded = _run_trunk(x_prompt, z_s5, z_s5, None, False, W,
                                                             interleaved=False)
    y_s, s_re, s_im, s_v, s_pool, s_conv, _ = _run_trunk(
        x_sample, state_s5_re, state_s5_im, (state_pool, state_conv), True, W, interleaved=True, rounded=rounded)
    return (y_p, y_s, p_re, p_im, p_pool, p_conv, s_re, s_im, s_v, s_pool, s_conv)
```
